```python
import jax, jax.numpy as jnp
from jax import lax
import numpy as np

D_MODEL = 1024
BATCH = 8
SEQ = 16384
DEPTH = 4

N_A_LAYERS = DEPTH // 2
N_B_LAYERS = DEPTH - N_A_LAYERS

POOL_WINDOWS = (2, 4, 8, 16)
N_POOL_GROUPS = len(POOL_WINDOWS)
POOL_GROUP = D_MODEL // N_POOL_GROUPS

N_HEADS = 16
QK_NOPE_DIM = 128
QK_ROPE_DIM = 64
QK_DIM = QK_NOPE_DIM + QK_ROPE_DIM
V_DIM = 128
Q_LORA_RANK = 256
KV_LORA_RANK = 128
ROPE_THETA = 10000.0
Q_BLOCK = 128

D_FF = 2816
RMS_EPS = 1e-6

kernel_name = "yoco_pool_mla_macaron_trunk"


def rmsnorm(x, g):
    xf = x.astype(jnp.float32)
    y = xf * lax.rsqrt(jnp.mean(xf * xf, axis=-1, keepdims=True) + RMS_EPS)
    return (y * g.astype(jnp.float32)).astype(x.dtype)


def swiglu(u, wg, wu, wd):
    return (jax.nn.silu(u @ wg) * (u @ wu)) @ wd


def rope_tables(seq):
    pos = jnp.arange(seq, dtype=jnp.float32)
    inv_freq = ROPE_THETA ** (-jnp.arange(0, QK_ROPE_DIM, 2, dtype=jnp.float32) / QK_ROPE_DIM)
    ang = pos[:, None] * inv_freq[None, :]
    return jnp.cos(ang), jnp.sin(ang)


def apply_rope(x, cos, sin):
    half = x.shape[-1] // 2
    x1, x2 = x[..., :half], x[..., half:]
    c, s = cos.astype(x.dtype), sin.astype(x.dtype)
    return jnp.concatenate([x1 * c - x2 * s, x1 * s + x2 * c], axis=-1)


def pool_mixer(u, w_groups, scale):
    B, S, D = u.shape
    uf = u.astype(jnp.float32)
    c = jnp.cumsum(uf, axis=1)
    count = jnp.arange(1, S + 1, dtype=jnp.float32)[None, :, None]
    outs = []
    for g, w in enumerate(POOL_WINDOWS):
        sl = slice(g * POOL_GROUP, (g + 1) * POOL_GROUP)
        cg = c[..., sl]
        lag = jnp.pad(cg, ((0, 0), (w, 0), (0, 0)))[:, :S]
        mean = (cg - lag) / jnp.minimum(count, float(w))
        outs.append(mean - uf[..., sl])
    y = jnp.stack(outs, axis=2).astype(u.dtype)
    z = jnp.einsum('bsgc,gcd->bsgd', y, w_groups).reshape(B, S, D)
    return z * scale


def mla_shared_kv(h, kv_in_norm, w_dkv, ckv_norm, w_uk, w_uv, cos, sin):
    u = rmsnorm(h, kv_in_norm)
    kv_a = u @ w_dkv
    c_kv = rmsnorm(kv_a[..., :KV_LORA_RANK], ckv_norm)
    k_rope = apply_rope(kv_a[..., KV_LORA_RANK:], cos[None], sin[None])
    k_nope = jnp.einsum('bsr,rhd->bshd', c_kv, w_uk)
    v = jnp.einsum('bsr,rhd->bshd', c_kv, w_uv)
    return k_nope, k_rope, v


def mla_attention(u, q_lora_norm, w_dq, w_uq, w_o, k_nope, k_rope, v, cos, sin):
    B, S, _ = u.shape
    cq = rmsnorm(u @ w_dq, q_lora_norm)
    q = jnp.einsum('bsr,rhd->bshd', cq, w_uq)
    q_nope = q[..., :QK_NOPE_DIM]
    q_rope = apply_rope(q[..., QK_NOPE_DIM:], cos[None, :, None], sin[None, :, None])
    nblk = S // Q_BLOCK
    qn = q_nope.reshape(B, nblk, Q_BLOCK, N_HEADS, QK_NOPE_DIM).transpose(1, 0, 2, 3, 4)
    qr = q_rope.reshape(B, nblk, Q_BLOCK, N_HEADS, QK_ROPE_DIM).transpose(1, 0, 2, 3, 4)
    scale = QK_DIM ** -0.5
    key_pos = jnp.arange(S)

    def block(args):
        i, qn_b, qr_b = args
        s = (jnp.einsum('bqhd,bkhd->bhqk', qn_b, k_nope)
             + jnp.einsum('bqhr,bkr->bhqk', qr_b, k_rope))
        s = s.astype(jnp.float32) * scale
        q_pos = i * Q_BLOCK + jnp.arange(Q_BLOCK)
        mask = key_pos[None, :] <= q_pos[:, None]
        s = jnp.where(mask[None, None], s, -jnp.inf)
        p = jax.nn.softmax(s, axis=-1).astype(v.dtype)
        return jnp.einsum('bhqk,bkhd->bqhd', p, v)

    o = lax.map(block, (jnp.arange(nblk), qn, qr))
    o = o.transpose(1, 0, 2, 3, 4).reshape(B, S, N_HEADS * V_DIM)
    return o @ w_o


def _fwd_setup_inputs(seed: int = 0) -> dict:
    key = jax.random.key(seed)
    ks = jax.random.split(key, 32)
    f32 = jnp.float32

    def w(k, shape, fan_in):
        return jax.random.normal(k, shape, f32) * (fan_in ** -0.5)

    def gain(k, shape):
        return 1.0 + 0.02 * jax.random.normal(k, shape, f32)

    D, F, G, Dg = D_MODEL, D_FF, N_POOL_GROUPS, POOL_GROUP
    return {
        "x": jax.random.normal(ks[0], (BATCH, SEQ, D), f32),
        "ffn_pre_norm": gain(ks[1], (DEPTH, D)),
        "ffn_pre_wg": w(ks[2], (DEPTH, D, F), D),
        "ffn_pre_wu": w(ks[3], (DEPTH, D, F), D),
        "ffn_pre_wd": w(ks[4], (DEPTH, F, D), F),
        "mix_norm": gain(ks[5], (DEPTH, D)),
        "ffn_post_norm": gain(ks[6], (DEPTH, D)),
        "ffn_post_wg": w(ks[7], (DEPTH, D, F), D),
        "ffn_post_wu": w(ks[8], (DEPTH, D, F), D),
        "ffn_post_wd": w(ks[9], (DEPTH, F, D), F),
        "pool_w": w(ks[10], (N_A_LAYERS, G, Dg, Dg), Dg),
        "pool_scale": gain(ks[11], (N_A_LAYERS, D)),
        "kv_in_norm": gain(ks[12], (D,)),
        "w_dkv": w(ks[13], (D, KV_LORA_RANK + QK_ROPE_DIM), D),
        "ckv_norm": gain(ks[14], (KV_LORA_RANK,)),
        "w_uk": w(ks[15], (KV_LORA_RANK, N_HEADS, QK_NOPE_DIM), KV_LORA_RANK),
        "w_uv": w(ks[16], (KV_LORA_RANK, N_HEADS, V_DIM), KV_LORA_RANK),
        "q_lora_norm": gain(ks[17], (N_B_LAYERS, Q_LORA_RANK)),
        "w_dq": w(ks[18], (N_B_LAYERS, D, Q_LORA_RANK), D),
        "w_uq": w(ks[19], (N_B_LAYERS, Q_LORA_RANK, N_HEADS, QK_DIM), Q_LORA_RANK),
        "w_o": w(ks[20], (N_B_LAYERS, N_HEADS * V_DIM, D), N_HEADS * V_DIM),
        "final_norm": gain(ks[21], (D,)),
    }


def _fwd_reference(x, ffn_pre_norm, ffn_pre_wg, ffn_pre_wu, ffn_pre_wd, mix_norm,
              ffn_post_norm, ffn_post_wg, ffn_post_wu, ffn_post_wd,
              pool_w, pool_scale, kv_in_norm, w_dkv, ckv_norm, w_uk, w_uv,
              q_lora_norm, w_dq, w_uq, w_o, final_norm):
    S = x.shape[1]
    cos, sin = rope_tables(S)
    h = x
    k_nope = k_rope = v = None
    for l in range(DEPTH):
        h = h + 0.5 * swiglu(rmsnorm(h, ffn_pre_norm[l]), ffn_pre_wg[l], ffn_pre_wu[l], ffn_pre_wd[l])
        u = rmsnorm(h, mix_norm[l])
        if l < N_A_LAYERS:
            h = h + pool_mixer(u, pool_w[l], pool_scale[l])
        else:
            j = l - N_A_LAYERS
            h = h + mla_attention(u, q_lora_norm[j], w_dq[j], w_uq[j], w_o[j],
                                  k_nope, k_rope, v, cos, sin)
        h = h + 0.5 * swiglu(rmsnorm(h, ffn_post_norm[l]), ffn_post_wg[l], ffn_post_wu[l], ffn_post_wd[l])
        if l == N_A_LAYERS - 1:
            k_nope, k_rope, v = mla_shared_kv(h, kv_in_norm, w_dkv, ckv_norm, w_uk, w_uv, cos, sin)
    return rmsnorm(h, final_norm)


import jax as _jax
import jax.numpy as _jnp

TWIN_FORMAT = 'train_step'
FWD_PARAMS = ['x', 'ffn_pre_norm', 'ffn_pre_wg', 'ffn_pre_wu', 'ffn_pre_wd', 'mix_norm', 'ffn_post_norm', 'ffn_post_wg', 'ffn_post_wu', 'ffn_post_wd', 'pool_w', 'pool_scale', 'kv_in_norm', 'w_dkv', 'ckv_norm', 'w_uk', 'w_uv', 'q_lora_norm', 'w_dq', 'w_uq', 'w_o', 'final_norm']
TWIN_WEIGHTS = ['ffn_pre_norm', 'ffn_pre_wg', 'ffn_pre_wu', 'ffn_pre_wd', 'mix_norm', 'ffn_post_norm', 'ffn_post_wg', 'ffn_post_wu', 'ffn_post_wd', 'pool_w', 'pool_scale', 'kv_in_norm', 'w_dkv', 'ckv_norm', 'w_uk', 'w_uv', 'q_lora_norm', 'w_dq', 'w_uq', 'w_o', 'final_norm']
TWIN_DIFF_INPUT = 'x'
TWIN_INPUTS = ['x', 'ffn_pre_norm', 'ffn_pre_wg', 'ffn_pre_wu', 'ffn_pre_wd', 'mix_norm', 'ffn_post_norm', 'ffn_post_wg', 'ffn_post_wu', 'ffn_post_wd', 'pool_w', 'pool_scale', 'kv_in_norm', 'w_dkv', 'ckv_norm', 'w_uk', 'w_uv', 'q_lora_norm', 'w_dq', 'w_uq', 'w_o', 'final_norm', 'loss_target', 'm_ffn_pre_norm', 'm_ffn_pre_wg', 'm_ffn_pre_wu', 'm_ffn_pre_wd', 'm_mix_norm', 'm_ffn_post_norm', 'm_ffn_post_wg', 'm_ffn_post_wu', 'm_ffn_post_wd', 'm_pool_w', 'm_pool_scale', 'm_kv_in_norm', 'm_w_dkv', 'm_ckv_norm', 'm_w_uk', 'm_w_uv', 'm_q_lora_norm', 'm_w_dq', 'm_w_uq', 'm_w_o', 'm_final_norm', 'v_ffn_pre_norm', 'v_ffn_pre_wg', 'v_ffn_pre_wu', 'v_ffn_pre_wd', 'v_mix_norm', 'v_ffn_post_norm', 'v_ffn_post_wg', 'v_ffn_post_wu', 'v_ffn_post_wd', 'v_pool_w', 'v_pool_scale', 'v_kv_in_norm', 'v_w_dkv', 'v_ckv_norm', 'v_w_uk', 'v_w_uv', 'v_q_lora_norm', 'v_w_dq', 'v_w_uq', 'v_w_o', 'v_final_norm']
TWIN_OUTPUTS = ['loss', 'grad_x', 'grad_ffn_pre_norm', 'grad_ffn_pre_wg', 'grad_ffn_pre_wu', 'grad_ffn_pre_wd', 'grad_mix_norm', 'grad_ffn_post_norm', 'grad_ffn_post_wg', 'grad_ffn_post_wu', 'grad_ffn_post_wd', 'grad_pool_w', 'grad_pool_scale', 'grad_kv_in_norm', 'grad_w_dkv', 'grad_ckv_norm', 'grad_w_uk', 'grad_w_uv', 'grad_q_lora_norm', 'grad_w_dq', 'grad_w_uq', 'grad_w_o', 'grad_final_norm', 'delta_ffn_pre_norm', 'delta_ffn_pre_wg', 'delta_ffn_pre_wu', 'delta_ffn_pre_wd', 'delta_mix_norm', 'delta_ffn_post_norm', 'delta_ffn_post_wg', 'delta_ffn_post_wu', 'delta_ffn_post_wd', 'delta_pool_w', 'delta_pool_scale', 'delta_kv_in_norm', 'delta_w_dkv', 'delta_ckv_norm', 'delta_w_uk', 'delta_w_uv', 'delta_q_lora_norm', 'delta_w_dq', 'delta_w_uq', 'delta_w_o', 'delta_final_norm', 'new_m_ffn_pre_norm', 'new_m_ffn_pre_wg', 'new_m_ffn_pre_wu', 'new_m_ffn_pre_wd', 'new_m_mix_norm', 'new_m_ffn_post_norm', 'new_m_ffn_post_wg', 'new_m_ffn_post_wu', 'new_m_ffn_post_wd', 'new_m_pool_w', 'new_m_pool_scale', 'new_m_kv_in_norm', 'new_m_w_dkv', 'new_m_ckv_norm', 'new_m_w_uk', 'new_m_w_uv', 'new_m_q_lora_norm', 'new_m_w_dq', 'new_m_w_uq', 'new_m_w_o', 'new_m_final_norm', 'new_v_ffn_pre_norm', 'new_v_ffn_pre_wg', 'new_v_ffn_pre_wu', 'new_v_ffn_pre_wd', 'new_v_mix_norm', 'new_v_ffn_post_norm', 'new_v_ffn_post_wg', 'new_v_ffn_post_wu', 'new_v_ffn_post_wd', 'new_v_pool_w', 'new_v_pool_scale', 'new_v_kv_in_norm', 'new_v_w_dkv', 'new_v_ckv_norm', 'new_v_w_uk', 'new_v_w_uv', 'new_v_q_lora_norm', 'new_v_w_dq', 'new_v_w_uq', 'new_v_w_o', 'new_v_final_norm']
TWIN_LEAF_KINDS = {'loss': 'loss', 'grad_x': 'grad_x', 'grad_ffn_pre_norm': 'grad_w', 'grad_ffn_pre_wg': 'grad_w', 'grad_ffn_pre_wu': 'grad_w', 'grad_ffn_pre_wd': 'grad_w', 'grad_mix_norm': 'grad_w', 'grad_ffn_post_norm': 'grad_w', 'grad_ffn_post_wg': 'grad_w', 'grad_ffn_post_wu': 'grad_w', 'grad_ffn_post_wd': 'grad_w', 'grad_pool_w': 'grad_w', 'grad_pool_scale': 'grad_w', 'grad_kv_in_norm': 'grad_w', 'grad_w_dkv': 'grad_w', 'grad_ckv_norm': 'grad_w', 'grad_w_uk': 'grad_w', 'grad_w_uv': 'grad_w', 'grad_q_lora_norm': 'grad_w', 'grad_w_dq': 'grad_w', 'grad_w_uq': 'grad_w', 'grad_w_o': 'grad_w', 'grad_final_norm': 'grad_w', 'delta_ffn_pre_norm': 'delta_w', 'delta_ffn_pre_wg': 'delta_w', 'delta_ffn_pre_wu': 'delta_w', 'delta_ffn_pre_wd': 'delta_w', 'delta_mix_norm': 'delta_w', 'delta_ffn_post_norm': 'delta_w', 'delta_ffn_post_wg': 'delta_w', 'delta_ffn_post_wu': 'delta_w', 'delta_ffn_post_wd': 'delta_w', 'delta_pool_w': 'delta_w', 'delta_pool_scale': 'delta_w', 'delta_kv_in_norm': 'delta_w', 'delta_w_dkv': 'delta_w', 'delta_ckv_norm': 'delta_w', 'delta_w_uk': 'delta_w', 'delta_w_uv': 'delta_w', 'delta_q_lora_norm': 'delta_w', 'delta_w_dq': 'delta_w', 'delta_w_uq': 'delta_w', 'delta_w_o': 'delta_w', 'delta_final_norm': 'delta_w', 'new_m_ffn_pre_norm': 'new_m', 'new_m_ffn_pre_wg': 'new_m', 'new_m_ffn_pre_wu': 'new_m', 'new_m_ffn_pre_wd': 'new_m', 'new_m_mix_norm': 'new_m', 'new_m_ffn_post_norm': 'new_m', 'new_m_ffn_post_wg': 'new_m', 'new_m_ffn_post_wu': 'new_m', 'new_m_ffn_post_wd': 'new_m', 'new_m_pool_w': 'new_m', 'new_m_pool_scale': 'new_m', 'new_m_kv_in_norm': 'new_m', 'new_m_w_dkv': 'new_m', 'new_m_ckv_norm': 'new_m', 'new_m_w_uk': 'new_m', 'new_m_w_uv': 'new_m', 'new_m_q_lora_norm': 'new_m', 'new_m_w_dq': 'new_m', 'new_m_w_uq': 'new_m', 'new_m_w_o': 'new_m', 'new_m_final_norm': 'new_m', 'new_v_ffn_pre_norm': 'new_v', 'new_v_ffn_pre_wg': 'new_v', 'new_v_ffn_pre_wu': 'new_v', 'new_v_ffn_pre_wd': 'new_v', 'new_v_mix_norm': 'new_v', 'new_v_ffn_post_norm': 'new_v', 'new_v_ffn_post_wg': 'new_v', 'new_v_ffn_post_wu': 'new_v', 'new_v_ffn_post_wd': 'new_v', 'new_v_pool_w': 'new_v', 'new_v_pool_scale': 'new_v', 'new_v_kv_in_norm': 'new_v', 'new_v_w_dkv': 'new_v', 'new_v_ckv_norm': 'new_v', 'new_v_w_uk': 'new_v', 'new_v_w_uv': 'new_v', 'new_v_q_lora_norm': 'new_v', 'new_v_w_dq': 'new_v', 'new_v_w_uq': 'new_v', 'new_v_w_o': 'new_v', 'new_v_final_norm': 'new_v'}


def _forward(args):
    return _fwd_reference(*[args[k] for k in FWD_PARAMS])


def _output_shape():
    def fwd():
        inp = _fwd_setup_inputs(0)
        return _fwd_reference(*[inp[k] for k in FWD_PARAMS])
    out = _jax.eval_shape(fwd)
    return out.shape, out.dtype

N_MICROBATCH = 1
ADAM_LR = 0.001
ADAM_B1 = 0.9
ADAM_B2 = 0.999
ADAM_EPS = 1e-08
ADAM_WD = 0.01
ADAM_STEP = 10
PER_EXAMPLE_BATCH_AXIS = {'x': 0, 'loss_target': 0}
SHARED_INPUTS = []
_WEIGHT_DTYPES = {'ffn_pre_norm': _jnp.float32, 'ffn_pre_wg': _jnp.float32, 'ffn_pre_wu': _jnp.float32, 'ffn_pre_wd': _jnp.float32, 'mix_norm': _jnp.float32, 'ffn_post_norm': _jnp.float32, 'ffn_post_wg': _jnp.float32, 'ffn_post_wu': _jnp.float32, 'ffn_post_wd': _jnp.float32, 'pool_w': _jnp.float32, 'pool_scale': _jnp.float32, 'kv_in_norm': _jnp.float32, 'w_dkv': _jnp.float32, 'ckv_norm': _jnp.float32, 'w_uk': _jnp.float32, 'w_uv': _jnp.float32, 'q_lora_norm': _jnp.float32, 'w_dq': _jnp.float32, 'w_uq': _jnp.float32, 'w_o': _jnp.float32, 'final_norm': _jnp.float32}
MOMENT_SCALE = {'ffn_pre_norm': 1.204832e-01, 'ffn_pre_wg': 5.182516e-02, 'ffn_pre_wu': 5.013203e-02, 'ffn_pre_wd': 8.311666e-02, 'mix_norm': 1.694335e-01, 'ffn_post_norm': 9.843490e-02, 'ffn_post_wg': 4.253869e-02, 'ffn_post_wu': 4.120031e-02, 'ffn_post_wd': 6.834816e-02, 'pool_w': 2.436295e-01, 'pool_scale': 7.137278e-01, 'kv_in_norm': 7.939053e-02, 'w_dkv': 1.805405e-01, 'ckv_norm': 2.248766e-01, 'w_uk': 3.306919e-02, 'w_uv': 4.358515e-02, 'q_lora_norm': 7.543997e-02, 'w_dq': 7.857981e-02, 'w_uq': 2.262300e-02, 'w_o': 4.362518e-02, 'final_norm': 1.291278e+02}


def _to_microbatches(a, axis):
    t = _jnp.moveaxis(a, axis, 0)
    t = t.reshape((N_MICROBATCH, t.shape[0] // N_MICROBATCH) + t.shape[1:])
    return _jnp.moveaxis(t, 1, axis + 1)


def setup_inputs(seed: int = 0) -> dict:
    inp = _fwd_setup_inputs(seed)
    key = _jax.random.fold_in(_jax.random.key(seed), 7919)
    shape, _ = _output_shape()
    out = dict(inp)
    out["loss_target"] = _jax.random.normal(_jax.random.fold_in(key, 0), shape, _jnp.float32)
    for i, name in enumerate(TWIN_WEIGHTS):
        w = inp[name].astype(_jnp.float32)
        if MOMENT_SCALE is None:
            s = _jnp.sqrt(_jnp.mean(_jnp.square(w)) + 1e-30)
        else:
            s = MOMENT_SCALE[name]
        km, kv = _jax.random.split(_jax.random.fold_in(key, i + 1))
        out[name] = w
        out["m_" + name] = s * _jax.random.normal(km, w.shape, _jnp.float32)
        out["v_" + name] = (s * s) * _jax.random.uniform(kv, w.shape, _jnp.float32, 0.5, 1.5)
    if N_MICROBATCH > 1:
        for name, axis in PER_EXAMPLE_BATCH_AXIS.items():
            out[name] = _to_microbatches(out[name], axis)
    return {'x': out['x'], 'ffn_pre_norm': out['ffn_pre_norm'], 'ffn_pre_wg': out['ffn_pre_wg'], 'ffn_pre_wu': out['ffn_pre_wu'], 'ffn_pre_wd': out['ffn_pre_wd'], 'mix_norm': out['mix_norm'], 'ffn_post_norm': out['ffn_post_norm'], 'ffn_post_wg': out['ffn_post_wg'], 'ffn_post_wu': out['ffn_post_wu'], 'ffn_post_wd': out['ffn_post_wd'], 'pool_w': out['pool_w'], 'pool_scale': out['pool_scale'], 'kv_in_norm': out['kv_in_norm'], 'w_dkv': out['w_dkv'], 'ckv_norm': out['ckv_norm'], 'w_uk': out['w_uk'], 'w_uv': out['w_uv'], 'q_lora_norm': out['q_lora_norm'], 'w_dq': out['w_dq'], 'w_uq': out['w_uq'], 'w_o': out['w_o'], 'final_norm': out['final_norm'], 'loss_target': out['loss_target'], 'm_ffn_pre_norm': out['m_ffn_pre_norm'], 'm_ffn_pre_wg': out['m_ffn_pre_wg'], 'm_ffn_pre_wu': out['m_ffn_pre_wu'], 'm_ffn_pre_wd': out['m_ffn_pre_wd'], 'm_mix_norm': out['m_mix_norm'], 'm_ffn_post_norm': out['m_ffn_post_norm'], 'm_ffn_post_wg': out['m_ffn_post_wg'], 'm_ffn_post_wu': out['m_ffn_post_wu'], 'm_ffn_post_wd': out['m_ffn_post_wd'], 'm_pool_w': out['m_pool_w'], 'm_pool_scale': out['m_pool_scale'], 'm_kv_in_norm': out['m_kv_in_norm'], 'm_w_dkv': out['m_w_dkv'], 'm_ckv_norm': out['m_ckv_norm'], 'm_w_uk': out['m_w_uk'], 'm_w_uv': out['m_w_uv'], 'm_q_lora_norm': out['m_q_lora_norm'], 'm_w_dq': out['m_w_dq'], 'm_w_uq': out['m_w_uq'], 'm_w_o': out['m_w_o'], 'm_final_norm': out['m_final_norm'], 'v_ffn_pre_norm': out['v_ffn_pre_norm'], 'v_ffn_pre_wg': out['v_ffn_pre_wg'], 'v_ffn_pre_wu': out['v_ffn_pre_wu'], 'v_ffn_pre_wd': out['v_ffn_pre_wd'], 'v_mix_norm': out['v_mix_norm'], 'v_ffn_post_norm': out['v_ffn_post_norm'], 'v_ffn_post_wg': out['v_ffn_post_wg'], 'v_ffn_post_wu': out['v_ffn_post_wu'], 'v_ffn_post_wd': out['v_ffn_post_wd'], 'v_pool_w': out['v_pool_w'], 'v_pool_scale': out['v_pool_scale'], 'v_kv_in_norm': out['v_kv_in_norm'], 'v_w_dkv': out['v_w_dkv'], 'v_ckv_norm': out['v_ckv_norm'], 'v_w_uk': out['v_w_uk'], 'v_w_uv': out['v_w_uv'], 'v_q_lora_norm': out['v_q_lora_norm'], 'v_w_dq': out['v_w_dq'], 'v_w_uq': out['v_w_uq'], 'v_w_o': out['v_w_o'], 'v_final_norm': out['v_final_norm']}


def _loss(weights, diff, rest, loss_target):
    with _jax.named_scope("forward"):
        args = {**rest, TWIN_DIFF_INPUT: diff, **{k: w.astype(_WEIGHT_DTYPES[k]) for k, w in weights.items()}}
        y = _forward(args)
    with _jax.named_scope("loss_head"):
        err = _jnp.square(y.astype(_jnp.float32) - loss_target)
        return 0.5 * _jnp.sum(_jnp.mean(err, axis=-1)) if err.ndim else 0.5 * err


def _adamw(w, g, m, v):
    m = ADAM_B1 * m + (1.0 - ADAM_B1) * g
    v = ADAM_B2 * v + (1.0 - ADAM_B2) * _jnp.square(g)
    m_hat = m / (1.0 - ADAM_B1 ** ADAM_STEP)
    v_hat = v / (1.0 - ADAM_B2 ** ADAM_STEP)
    delta = -ADAM_LR * (m_hat / (_jnp.sqrt(v_hat) + ADAM_EPS) + ADAM_WD * w)
    return delta, m, v


def reference(x, ffn_pre_norm, ffn_pre_wg, ffn_pre_wu, ffn_pre_wd, mix_norm, ffn_post_norm, ffn_post_wg, ffn_post_wu, ffn_post_wd, pool_w, pool_scale, kv_in_norm, w_dkv, ckv_norm, w_uk, w_uv, q_lora_norm, w_dq, w_uq, w_o, final_norm, loss_target, m_ffn_pre_norm, m_ffn_pre_wg, m_ffn_pre_wu, m_ffn_pre_wd, m_mix_norm, m_ffn_post_norm, m_ffn_post_wg, m_ffn_post_wu, m_ffn_post_wd, m_pool_w, m_pool_scale, m_kv_in_norm, m_w_dkv, m_ckv_norm, m_w_uk, m_w_uv, m_q_lora_norm, m_w_dq, m_w_uq, m_w_o, m_final_norm, v_ffn_pre_norm, v_ffn_pre_wg, v_ffn_pre_wu, v_ffn_pre_wd, v_mix_norm, v_ffn_post_norm, v_ffn_post_wg, v_ffn_post_wu, v_ffn_post_wd, v_pool_w, v_pool_scale, v_kv_in_norm, v_w_dkv, v_ckv_norm, v_w_uk, v_w_uv, v_q_lora_norm, v_w_dq, v_w_uq, v_w_o, v_final_norm):
    given = dict(x=x, ffn_pre_norm=ffn_pre_norm, ffn_pre_wg=ffn_pre_wg, ffn_pre_wu=ffn_pre_wu, ffn_pre_wd=ffn_pre_wd, mix_norm=mix_norm, ffn_post_norm=ffn_post_norm, ffn_post_wg=ffn_post_wg, ffn_post_wu=ffn_post_wu, ffn_post_wd=ffn_post_wd, pool_w=pool_w, pool_scale=pool_scale, kv_in_norm=kv_in_norm, w_dkv=w_dkv, ckv_norm=ckv_norm, w_uk=w_uk, w_uv=w_uv, q_lora_norm=q_lora_norm, w_dq=w_dq, w_uq=w_uq, w_o=w_o, final_norm=final_norm, loss_target=loss_target, m_ffn_pre_norm=m_ffn_pre_norm, m_ffn_pre_wg=m_ffn_pre_wg, m_ffn_pre_wu=m_ffn_pre_wu, m_ffn_pre_wd=m_ffn_pre_wd, m_mix_norm=m_mix_norm, m_ffn_post_norm=m_ffn_post_norm, m_ffn_post_wg=m_ffn_post_wg, m_ffn_post_wu=m_ffn_post_wu, m_ffn_post_wd=m_ffn_post_wd, m_pool_w=m_pool_w, m_pool_scale=m_pool_scale, m_kv_in_norm=m_kv_in_norm, m_w_dkv=m_w_dkv, m_ckv_norm=m_ckv_norm, m_w_uk=m_w_uk, m_w_uv=m_w_uv, m_q_lora_norm=m_q_lora_norm, m_w_dq=m_w_dq, m_w_uq=m_w_uq, m_w_o=m_w_o, m_final_norm=m_final_norm, v_ffn_pre_norm=v_ffn_pre_norm, v_ffn_pre_wg=v_ffn_pre_wg, v_ffn_pre_wu=v_ffn_pre_wu, v_ffn_pre_wd=v_ffn_pre_wd, v_mix_norm=v_mix_norm, v_ffn_post_norm=v_ffn_post_norm, v_ffn_post_wg=v_ffn_post_wg, v_ffn_post_wu=v_ffn_post_wu, v_ffn_post_wd=v_ffn_post_wd, v_pool_w=v_pool_w, v_pool_scale=v_pool_scale, v_kv_in_norm=v_kv_in_norm, v_w_dkv=v_w_dkv, v_ckv_norm=v_ckv_norm, v_w_uk=v_w_uk, v_w_uv=v_w_uv, v_q_lora_norm=v_q_lora_norm, v_w_dq=v_w_dq, v_w_uq=v_w_uq, v_w_o=v_w_o, v_final_norm=v_final_norm)
    weights = {n: given[n] for n in TWIN_WEIGHTS}
    shared = {n: given[n] for n in SHARED_INPUTS}
    per_example = {n: given[n] for n in ['x']}
    grad_fn = _jax.value_and_grad(_loss, argnums=(0, 1))

    def one_microbatch(ex, loss_target):
        ex = dict(ex)
        diff = ex.pop(TWIN_DIFF_INPUT)
        return grad_fn(weights, diff, {**shared, **ex}, loss_target)

    if N_MICROBATCH == 1:
        loss, (grad_w, grad_x) = one_microbatch(per_example, given["loss_target"])
    else:
        def body(carry, xs):
            loss_sum, grad_sum = carry
            l_k, (gw_k, gx_k) = one_microbatch(xs[0], xs[1])
            with _jax.named_scope("update"):
                return (loss_sum + l_k, _jax.tree.map(_jnp.add, grad_sum, gw_k)), gx_k

        init = (_jnp.zeros((), _jnp.float32), _jax.tree.map(_jnp.zeros_like, weights))
        (loss, grad_w), grad_x = _jax.lax.scan(body, init, (per_example, given["loss_target"]))
    with _jax.named_scope("update"):
        delta_w, new_m, new_v = {}, {}, {}
        for n in TWIN_WEIGHTS:
            delta_w[n], new_m[n], new_v[n] = _adamw(weights[n], grad_w[n], given["m_" + n], given["v_" + n])
    return (loss, grad_x, *[grad_w[n] for n in TWIN_WEIGHTS], *[delta_w[n] for n in TWIN_WEIGHTS],
            *[new_m[n] for n in TWIN_WEIGHTS], *[new_v[n] for n in TWIN_WEIGHTS])
```

```python
import functools

import jax
import jax.numpy as jnp
from jax import lax
from jax.experimental import pallas as pl
from jax.experimental.pallas import tpu as pltpu

F32 = jnp.float32
BF16 = jnp.bfloat16
MESH = pl.DeviceIdType.MESH

DEPTH = 4
N_A = 2
N_HEADS = 16
DN = 128
DR = 64
HD = 256
QK_DIM = DN + DR
POOL_WINDOWS = (2, 4, 8, 16)
PG = 256
HALO = 16
N_CHIPS = 4
RMS_EPS = 1e-6
ROPE_THETA = 10000.0
ATT_SCALE = QK_DIM ** -0.5
NEG = -1e30

ADAM_LR = 0.001
ADAM_B1 = 0.9
ADAM_B2 = 0.999
ADAM_EPS = 1e-08
ADAM_WD = 0.01
ADAM_STEP = 10

VMEM_BYTES_V7X = 64 * 1024 * 1024
VMEM_LIMIT = VMEM_BYTES_V7X * 3 // 4

NN = ((1,), (0,))
NT = ((1,), (1,))
TN = ((0,), (0,))


def _cp(*sem):
    return pltpu.CompilerParams(dimension_semantics=sem, vmem_limit_bytes=VMEM_LIMIT)


def _dot(a, b, dims):
    return lax.dot_general(a, b, (dims, ((), ())), preferred_element_type=F32)


def _tile(n, pref):
    if n <= pref:
        return n
    if n % pref == 0:
        return pref
    t = 1 << (pref.bit_length() - 1)
    while n % t:
        t //= 2
    return t


def _rinv(x):
    return lax.rsqrt(jnp.mean(x * x, axis=-1, keepdims=True) + RMS_EPS)


def _sigmoid(a):
    return 1.0 / (1.0 + jnp.exp(-a))


def _mm(name, a, b, *, grid, a_block, a_map, b_block, b_map, o_block, o_map, out_shape, dims,
        out_dtype, res=None):
    nax = len(grid)
    nk = grid[-1]
    acc_shape = tuple(d for d in o_block if d is not None)
    has_res = res is not None

    def body(*refs):
        a_ref, b_ref = refs[0], refs[1]
        o_ref, acc = refs[-2], refs[-1]
        k = pl.program_id(nax - 1)

        @pl.when(k == 0)
        def _():
            acc[...] = jnp.zeros(acc_shape, F32)

        acc[...] += _dot(a_ref[...].astype(BF16), b_ref[...].astype(BF16), dims)

        @pl.when(k == nk - 1)
        def _():
            r = acc[...]
            if has_res:
                r = r + refs[2][...].astype(F32)
            o_ref[...] = r.astype(out_dtype)

    in_specs = [pl.BlockSpec(a_block, a_map), pl.BlockSpec(b_block, b_map)]
    args = [a, b]
    if has_res:
        in_specs.append(pl.BlockSpec(o_block, o_map))
        args.append(res)
    return pl.pallas_call(
        body, name=name, grid=grid, in_specs=in_specs, out_specs=pl.BlockSpec(o_block, o_map),
        out_shape=jax.ShapeDtypeStruct(out_shape, out_dtype),
        scratch_shapes=[pltpu.VMEM(acc_shape, F32)],
        compiler_params=_cp(*(("parallel",) * (nax - 1) + ("arbitrary",))),
    )(*args)


def mm_nn(name, a, b, out_dtype, tm, tn, tk, res=None):
    (m, kd), n = a.shape, b.shape[1]
    tm, tn, tk = _tile(m, tm), _tile(n, tn), _tile(kd, tk)
    return _mm(name, a, b, grid=(m // tm, n // tn, kd // tk),
               a_block=(tm, tk), a_map=lambda i, j, k: (i, k),
               b_block=(tk, tn), b_map=lambda i, j, k: (k, j),
               o_block=(tm, tn), o_map=lambda i, j, k: (i, j),
               out_shape=(m, n), dims=NN, out_dtype=out_dtype, res=res)


def mm_nt(name, a, b, out_dtype, tm, tn, tk):
    (m, kd), n = a.shape, b.shape[0]
    tm, tn, tk = _tile(m, tm), _tile(n, tn), _tile(kd, tk)
    return _mm(name, a, b, grid=(m // tm, n // tn, kd // tk),
               a_block=(tm, tk), a_map=lambda i, j, k: (i, k),
               b_block=(tn, tk), b_map=lambda i, j, k: (j, k),
               o_block=(tm, tn), o_map=lambda i, j, k: (i, j),
               out_shape=(m, n), dims=NT, out_dtype=out_dtype)


def mm_tn(name, a, b, out_dtype, tm, tn, tk):
    (kd, m), n = a.shape, b.shape[1]
    tm, tn, tk = _tile(m, tm), _tile(n, tn), _tile(kd, tk)
    return _mm(name, a, b, grid=(m // tm, n // tn, kd // tk),
               a_block=(tk, tm), a_map=lambda i, j, k: (k, i),
               b_block=(tk, tn), b_map=lambda i, j, k: (k, j),
               o_block=(tm, tn), o_map=lambda i, j, k: (i, j),
               out_shape=(m, n), dims=TN, out_dtype=out_dtype)


def rms_fwd(name, x, gain):
    s, d = x.shape
    tm = _tile(s, 1024)

    def body(x_ref, g_ref, o_ref):
        xv = x_ref[...]
        o_ref[...] = (xv * _rinv(xv) * g_ref[...]).astype(BF16)

    return pl.pallas_call(
        body, name=name, grid=(s // tm,),
        in_specs=[pl.BlockSpec((tm, d), lambda i: (i, 0)), pl.BlockSpec((1, d), lambda i: (0, 0))],
        out_specs=pl.BlockSpec((tm, d), lambda i: (i, 0)),
        out_shape=jax.ShapeDtypeStruct((s, d), BF16), compiler_params=_cp("parallel"),
    )(x, gain)


def _rms_bwd_math(xv, g, du):
    rinv = _rinv(xv)
    xhat = xv * rinv
    dgain = jnp.sum(du * xhat, axis=0, keepdims=True)
    dxh = du * g
    dx = rinv * (dxh - xhat * jnp.mean(dxh * xhat, axis=-1, keepdims=True))
    return dx, dgain


def rms_bwd(name, x, gain, du, dres=None):
    s, d = x.shape
    tm = _tile(s, 1024)
    has_res = dres is not None

    def body(*refs):
        x_ref, g_ref, du_ref = refs[:3]
        dx_ref, dg_ref = refs[-2:]

        @pl.when(pl.program_id(0) == 0)
        def _():
            dg_ref[...] = jnp.zeros((1, d), F32)

        dx, dgain = _rms_bwd_math(x_ref[...], g_ref[...], du_ref[...].astype(F32))
        if has_res:
            dx = dx + refs[3][...]
        dx_ref[...] = dx
        dg_ref[...] += dgain

    row = pl.BlockSpec((tm, d), lambda i: (i, 0))
    vec = pl.BlockSpec((1, d), lambda i: (0, 0))
    args = [x, gain, du] + ([dres] if has_res else [])
    return pl.pallas_call(
        body, name=name, grid=(s // tm,),
        in_specs=[row, vec, row] + ([row] if has_res else []),
        out_specs=[row, vec],
        out_shape=[jax.ShapeDtypeStruct((s, d), F32), jax.ShapeDtypeStruct((1, d), F32)],
        compiler_params=_cp("arbitrary"),
    )(*args)


def ffn_fwd(name, h, gain, wgT, wuT, wd, layer):
    s, d = h.shape
    f = wd.shape[1]
    tm, tf = _tile(s, 1024), _tile(f, 256)
    nf = f // tf

    def body(h_ref, g_ref, wg_ref, wu_ref, wd_ref, o_ref, u_sc, acc):
        j = pl.program_id(1)

        @pl.when(j == 0)
        def _():
            xv = h_ref[...]
            u_sc[...] = (xv * _rinv(xv) * g_ref[...]).astype(BF16)
            acc[...] = jnp.zeros((tm, d), F32)

        u = u_sc[...]
        a = _dot(u, wg_ref[...], NT)
        b = _dot(u, wu_ref[...], NT)
        act = (a * _sigmoid(a) * b).astype(BF16)
        acc[...] += _dot(act, wd_ref[...], NN)

        @pl.when(j == nf - 1)
        def _():
            o_ref[...] = h_ref[...] + 0.5 * acc[...]

    row = pl.BlockSpec((tm, d), lambda i, j: (i, 0))
    wsp = pl.BlockSpec((None, tf, d), lambda i, j: (layer, j, 0))
    return pl.pallas_call(
        body, name=name, grid=(s // tm, nf),
        in_specs=[row, pl.BlockSpec((1, d), lambda i, j: (0, 0)), wsp, wsp, wsp],
        out_specs=row, out_shape=jax.ShapeDtypeStruct((s, d), F32),
        scratch_shapes=[pltpu.VMEM((tm, d), BF16), pltpu.VMEM((tm, d), F32)],
        compiler_params=_cp("parallel", "arbitrary"),
    )(h, gain, wgT, wuT, wd)


def ffn_bwd(name, h, gain, wgT, wuT, wd, layer, dout):
    s, d = h.shape
    f = wd.shape[1]
    tm, tf = _tile(s, 512), _tile(f, 256)
    nf = f // tf

    def body(h_ref, g_ref, wg_ref, wu_ref, wd_ref, do_ref,
             dh_ref, u_ref, dob_ref, act_ref, da_ref, db_ref, dg_ref, du_acc):
        i, j = pl.program_id(0), pl.program_id(1)

        @pl.when(j == 0)
        def _():
            xv = h_ref[...]
            u_ref[...] = (xv * _rinv(xv) * g_ref[...]).astype(BF16)
            dob_ref[...] = (0.5 * do_ref[...]).astype(BF16)
            du_acc[...] = jnp.zeros((tm, d), F32)

        @pl.when((i == 0) & (j == 0))
        def _():
            dg_ref[...] = jnp.zeros((1, d), F32)

        u = u_ref[...]
        wg, wu = wg_ref[...], wu_ref[...]
        a = _dot(u, wg, NT)
        b = _dot(u, wu, NT)
        sig = _sigmoid(a)
        sa = a * sig
        dact = _dot(dob_ref[...], wd_ref[...], NT)
        db = (dact * sa).astype(BF16)
        da = (dact * b * (sig * (1.0 + a * (1.0 - sig)))).astype(BF16)
        act_ref[...] = (sa * b).astype(BF16)
        da_ref[...] = da
        db_ref[...] = db
        du_acc[...] += _dot(da, wg, NN) + _dot(db, wu, NN)

        @pl.when(j == nf - 1)
        def _():
            dx, dgain = _rms_bwd_math(h_ref[...], g_ref[...], du_acc[...])
            dh_ref[...] = do_ref[...] + dx
            dg_ref[...] += dgain

    row = pl.BlockSpec((tm, d), lambda i, j: (i, 0))
    vec = pl.BlockSpec((1, d), lambda i, j: (0, 0))
    wsp = pl.BlockSpec((None, tf, d), lambda i, j: (layer, j, 0))
    hid = pl.BlockSpec((tm, tf), lambda i, j: (i, j))
    sd = lambda dt: jax.ShapeDtypeStruct((s, d), dt)
    sf = jax.ShapeDtypeStruct((s, f), BF16)
    return pl.pallas_call(
        body, name=name, grid=(s // tm, nf),
        in_specs=[row, vec, wsp, wsp, wsp, row],
        out_specs=[row, row, row, hid, hid, hid, vec],
        out_shape=[sd(F32), sd(BF16), sd(BF16), sf, sf, sf, jax.ShapeDtypeStruct((1, d), F32)],
        scratch_shapes=[pltpu.VMEM((tm, d), F32)],
        compiler_params=_cp("arbitrary", "arbitrary"),
    )(h, gain, wgT, wuT, wd, dout)


def pool_fwd(name, h, gain, pw, scale, layer):
    s, d = h.shape
    tm = _tile(s, 512)

    def body(h_ref, g_ref, pw_ref, sc_ref, o_ref, y_ref, ext):
        i = pl.program_id(0)

        @pl.when(i == 0)
        def _():
            ext[0:HALO, :] = jnp.zeros((HALO, d), F32)

        xv = h_ref[...]
        ext[HALO:HALO + tm, :] = xv * _rinv(xv) * g_ref[...]
        pos = i * tm + lax.broadcasted_iota(jnp.int32, (tm, 1), 0)
        for g, w in enumerate(POOL_WINDOWS):
            lo, hi = g * PG, (g + 1) * PG
            ug = ext[HALO:HALO + tm, lo:hi]
            tot = ug
            for k in range(1, w):
                tot = tot + ext[HALO - k:HALO - k + tm, lo:hi]
            cnt = jnp.minimum(pos + 1, w).astype(F32)
            yb = (tot / cnt - ug).astype(BF16)
            y_ref[:, lo:hi] = yb
            o_ref[:, lo:hi] = xv[:, lo:hi] + _dot(yb, pw_ref[g], NN) * sc_ref[:, lo:hi]
        ext[0:HALO, :] = ext[tm:tm + HALO, :]

    row = pl.BlockSpec((tm, d), lambda i: (i, 0))
    vec = pl.BlockSpec((1, d), lambda i: (0, 0))
    return pl.pallas_call(
        body, name=name, grid=(s // tm,),
        in_specs=[row, vec, pl.BlockSpec((None, 4, PG, PG), lambda i: (layer, 0, 0, 0)), vec],
        out_specs=[row, row],
        out_shape=[jax.ShapeDtypeStruct((s, d), F32), jax.ShapeDtypeStruct((s, d), BF16)],
        scratch_shapes=[pltpu.VMEM((HALO + tm, d), F32)],
        compiler_params=_cp("arbitrary"),
    )(h, gain, pw, scale)


def pool_bwd(name, h, gain, y, pw, scale, layer, dout):
    s, d = h.shape
    tm = _tile(s, 512)
    ns = s // tm

    def body(h_ref, g_ref, y_ref, pw_ref, sc_ref, do_ref, dh_ref, dz_ref, dg_ref, dsc_ref, ext, du_sc):
        i = pl.program_id(0)
        t = ns - 1 - i

        @pl.when(i == 0)
        def _():
            ext[tm:tm + HALO, :] = jnp.zeros((HALO, d), F32)
            dg_ref[...] = jnp.zeros((1, d), F32)
            dsc_ref[...] = jnp.zeros((1, d), F32)

        pos = t * tm + lax.broadcasted_iota(jnp.int32, (tm, 1), 0)
        for g, w in enumerate(POOL_WINDOWS):
            lo, hi = g * PG, (g + 1) * PG
            dog = do_ref[:, lo:hi]
            z = _dot(y_ref[:, lo:hi], pw_ref[g], NN)
            dsc_ref[:, lo:hi] += jnp.sum(dog * z, axis=0, keepdims=True)
            dzb = (dog * sc_ref[:, lo:hi]).astype(BF16)
            dz_ref[:, lo:hi] = dzb
            dy = _dot(dzb, pw_ref[g], NT)
            cnt = jnp.minimum(pos + 1, w).astype(F32)
            ext[0:tm, lo:hi] = dy / cnt
            tot = ext[0:tm, lo:hi]
            for k in range(1, w):
                tot = tot + ext[k:k + tm, lo:hi]
            du_sc[:, lo:hi] = tot - dy
        ext[tm:tm + HALO, :] = ext[0:HALO, :]
        dx, dgain = _rms_bwd_math(h_ref[...], g_ref[...], du_sc[...])
        dh_ref[...] = do_ref[...] + dx
        dg_ref[...] += dgain

    row = pl.BlockSpec((tm, d), lambda i: (ns - 1 - i, 0))
    vec = pl.BlockSpec((1, d), lambda i: (0, 0))
    return pl.pallas_call(
        body, name=name, grid=(ns,),
        in_specs=[row, vec, row, pl.BlockSpec((None, 4, PG, PG), lambda i: (layer, 0, 0, 0)), vec, row],
        out_specs=[row, row, vec, vec],
        out_shape=[jax.ShapeDtypeStruct((s, d), F32), jax.ShapeDtypeStruct((s, d), BF16),
                   jax.ShapeDtypeStruct((1, d), F32), jax.ShapeDtypeStruct((1, d), F32)],
        scratch_shapes=[pltpu.VMEM((tm + HALO, d), F32), pltpu.VMEM((tm, d), F32)],
        compiler_params=_cp("arbitrary"),
    )(h, gain, y, pw, scale, dout)


def _rope128(raw_hi, cs):
    b = raw_hi * cs
    r = b + pltpu.roll(b, DR, 1)
    lane = lax.broadcasted_iota(jnp.int32, r.shape, 1)
    return jnp.where(lane < DR, r, 0.0)


def _rope128_bwd(t, cs):
    return (t + pltpu.roll(t, DR, 1)) * cs


def q_heads(name, cq, w256, cs):
    s = cq.shape[0]
    nh = w256.shape[0]
    tm = _tile(s, 1024)

    def body(cq_ref, w_ref, cs_ref, o_ref):
        r = _dot(cq_ref[...], w_ref[...], NN)
        o_ref[:, 0:DN] = r[:, 0:DN].astype(BF16)
        o_ref[:, DN:HD] = _rope128(r[:, DN:HD], cs_ref[...]).astype(BF16)

    return pl.pallas_call(
        body, name=name, grid=(s // tm, nh),
        in_specs=[pl.BlockSpec((tm, HD), lambda i, hh: (i, 0)),
                  pl.BlockSpec((None, HD, HD), lambda i, hh: (hh, 0, 0)),
                  pl.BlockSpec((tm, DN), lambda i, hh: (i, 0))],
        out_specs=pl.BlockSpec((None, tm, HD), lambda i, hh: (hh, i, 0)),
        out_shape=jax.ShapeDtypeStruct((nh, s, HD), BF16),
        compiler_params=_cp("parallel", "parallel"),
    )(cq, w256, cs)


def kv_post(name, kv_raw, gain, cs):
    s = kv_raw.shape[0]
    tm = _tile(s, 1024)

    def body(x_ref, g_ref, cs_ref, c_ref, kr_ref):
        c = x_ref[:, 0:DN]
        c_ref[...] = (c * _rinv(c) * g_ref[...]).astype(BF16)
        kr_ref[...] = _rope128(x_ref[:, DN:HD], cs_ref[...]).astype(BF16)

    half = pl.BlockSpec((tm, DN), lambda i: (i, 0))
    return pl.pallas_call(
        body, name=name, grid=(s // tm,),
        in_specs=[pl.BlockSpec((tm, HD), lambda i: (i, 0)), pl.BlockSpec((1, DN), lambda i: (0, 0)), half],
        out_specs=[half, half],
        out_shape=[jax.ShapeDtypeStruct((s, DN), BF16), jax.ShapeDtypeStruct((s, DN), BF16)],
        compiler_params=_cp("parallel"),
    )(kv_raw, gain, cs)


def kv_heads(name, c_kv, kr, wuk, wuv):
    s = c_kv.shape[0]
    nh = wuk.shape[0]
    tm = _tile(s, 1024)

    def body(c_ref, kr_ref, wk_ref, wv_ref, k_ref, v_ref):
        c = c_ref[...]
        k_ref[:, 0:DN] = _dot(c, wk_ref[...], NN).astype(BF16)
        k_ref[:, DN:HD] = kr_ref[...]
        v_ref[...] = _dot(c, wv_ref[...], NN).astype(BF16)

    half = pl.BlockSpec((tm, DN), lambda i, hh: (i, 0))
    wsp = pl.BlockSpec((None, DN, DN), lambda i, hh: (hh, 0, 0))
    return pl.pallas_call(
        body, name=name, grid=(s // tm, nh),
        in_specs=[half, half, wsp, wsp],
        out_specs=[pl.BlockSpec((None, tm, HD), lambda i, hh: (hh, i, 0)),
                   pl.BlockSpec((None, tm, DN), lambda i, hh: (hh, i, 0))],
        out_shape=[jax.ShapeDtypeStruct((nh, s, HD), BF16), jax.ShapeDtypeStruct((nh, s, DN), BF16)],
        compiler_params=_cp("parallel", "parallel"),
    )(c_kv, kr, wuk, wuv)


def kv_post_bwd(name, kv_raw, gain, dc, dk, cs):
    s = kv_raw.shape[0]
    nh = dk.shape[0]
    tm = _tile(s, 1024)

    def body(x_ref, g_ref, dc_ref, dk_ref, cs_ref, o_ref, dg_ref, acc):
        i, hh = pl.program_id(0), pl.program_id(1)

        @pl.when(hh == 0)
        def _():
            acc[...] = jnp.zeros((tm, DN), F32)

        @pl.when((i == 0) & (hh == 0))
        def _():
            dg_ref[...] = jnp.zeros((1, DN), F32)

        acc[...] += dk_ref[...]

        @pl.when(hh == nh - 1)
        def _():
            dx, dgain = _rms_bwd_math(x_ref[:, 0:DN], g_ref[...], dc_ref[...])
            o_ref[:, 0:DN] = dx.astype(BF16)
            o_ref[:, DN:HD] = _rope128_bwd(acc[...], cs_ref[...]).astype(BF16)
            dg_ref[...] += dgain

    half = pl.BlockSpec((tm, DN), lambda i, hh: (i, 0))
    vec = pl.BlockSpec((1, DN), lambda i, hh: (0, 0))
    full = pl.BlockSpec((tm, HD), lambda i, hh: (i, 0))
    return pl.pallas_call(
        body, name=name, grid=(s // tm, nh),
        in_specs=[full, vec, half, pl.BlockSpec((None, tm, DN), lambda i, hh: (hh, i, 1)), half],
        out_specs=[full, vec],
        out_shape=[jax.ShapeDtypeStruct((s, HD), BF16), jax.ShapeDtypeStruct((1, DN), F32)],
        scratch_shapes=[pltpu.VMEM((tm, DN), F32)],
        compiler_params=_cp("arbitrary", "arbitrary"),
    )(kv_raw, gain, dc, dk, cs)


TQ = 1024
TC = 512


def flash_fwd(name, q, k, v):
    nh, s, _ = q.shape
    tq, tk = _tile(s, TQ), _tile(s, TC)
    per = tq // tk

    def body(q_ref, k_ref, v_ref, o_ref, lse_ref, m_sc, l_sc, acc):
        i = pl.program_id(1)
        m_sc[...] = jnp.full((tq, 1), NEG, F32)
        l_sc[...] = jnp.zeros((tq, 1), F32)
        acc[...] = jnp.zeros((tq, DN), F32)
        qv = q_ref[...]

        def step(j, masked):
            start = pl.multiple_of(j * tk, tk)
            sc = _dot(qv, k_ref[pl.ds(start, tk), :], NT) * ATT_SCALE
            if masked:
                row = i * tq + lax.broadcasted_iota(jnp.int32, (tq, tk), 0)
                col = j * tk + lax.broadcasted_iota(jnp.int32, (tq, tk), 1)
                sc = jnp.where(col <= row, sc, NEG)
            m_prev = m_sc[...]
            m_new = jnp.maximum(m_prev, jnp.max(sc, axis=-1, keepdims=True))
            alpha = jnp.exp(m_prev - m_new)
            p = jnp.exp(sc - m_new)
            l_sc[...] = alpha * l_sc[...] + jnp.sum(p, axis=-1, keepdims=True)
            acc[...] = alpha * acc[...] + _dot(p.astype(BF16), v_ref[pl.ds(start, tk), :], NN)
            m_sc[...] = m_new

        def full_step(j, carry):
            step(j, False)
            return carry

        lax.fori_loop(0, i * per, full_step, 0)
        for dd in range(per):
            step(i * per + dd, True)
        l = l_sc[...]
        o_ref[...] = (acc[...] / l).astype(BF16)
        lse_ref[...] = jnp.broadcast_to(m_sc[...] + jnp.log(l), (tq, DN))

    return pl.pallas_call(
        body, name=name, grid=(nh, s // tq),
        in_specs=[pl.BlockSpec((None, tq, HD), lambda hh, i: (hh, i, 0)),
                  pl.BlockSpec((None, s, HD), lambda hh, i: (hh, 0, 0)),
                  pl.BlockSpec((None, s, DN), lambda hh, i: (hh, 0, 0))],
        out_specs=[pl.BlockSpec((tq, DN), lambda hh, i: (i, hh)),
                   pl.BlockSpec((None, tq, DN), lambda hh, i: (hh, i, 0))],
        out_shape=[jax.ShapeDtypeStruct((s, nh * DN), BF16), jax.ShapeDtypeStruct((nh, s, DN), F32)],
        scratch_shapes=[pltpu.VMEM((tq, 1), F32), pltpu.VMEM((tq, 1), F32), pltpu.VMEM((tq, DN), F32)],
        compiler_params=_cp("parallel", "arbitrary"),
    )(q, k, v)


def attn_delta(name, o, do, nh):
    s = o.shape[0]
    tm = _tile(s, 1024)

    def body(o_ref, do_ref, d_ref):
        dl = jnp.sum(o_ref[...].astype(F32) * do_ref[...].astype(F32), axis=-1, keepdims=True)
        d_ref[...] = jnp.broadcast_to(dl, (tm, DN))

    blk = pl.BlockSpec((tm, DN), lambda hh, i: (i, hh))
    return pl.pallas_call(
        body, name=name, grid=(nh, s // tm), in_specs=[blk, blk],
        out_specs=pl.BlockSpec((None, tm, DN), lambda hh, i: (hh, i, 0)),
        out_shape=jax.ShapeDtypeStruct((nh, s, DN), F32), compiler_params=_cp("parallel", "parallel"),
    )(o, do)


def flash_dq(name, q, k, v, do, lse, delta, cs):
    nh, s, _ = q.shape
    tq, tk = _tile(s, TQ), _tile(s, TC)
    per = tq // tk

    def body(q_ref, k_ref, v_ref, do_ref, lse_ref, dl_ref, cs_ref, dq_ref, acc):
        i = pl.program_id(1)
        acc[...] = jnp.zeros((tq, HD), F32)
        qv = q_ref[...]
        dov = do_ref[...]
        lse_c = lse_ref[:, 0:1]
        dl_c = dl_ref[:, 0:1]

        def step(j, masked):
            start = pl.multiple_of(j * tk, tk)
            kj = k_ref[pl.ds(start, tk), :]
            sc = _dot(qv, kj, NT) * ATT_SCALE
            if masked:
                row = i * tq + lax.broadcasted_iota(jnp.int32, (tq, tk), 0)
                col = j * tk + lax.broadcasted_iota(jnp.int32, (tq, tk), 1)
                sc = jnp.where(col <= row, sc, NEG)
            p = jnp.exp(sc - lse_c)
            dp = _dot(dov, v_ref[pl.ds(start, tk), :], NT)
            ds = (p * (dp - dl_c) * ATT_SCALE).astype(BF16)
            acc[...] += _dot(ds, kj, NN)

        def full_step(j, carry):
            step(j, False)
            return carry

        lax.fori_loop(0, i * per, full_step, 0)
        for dd in range(per):
            step(i * per + dd, True)
        dq_ref[:, 0:DN] = acc[:, 0:DN].astype(BF16)
        dq_ref[:, DN:HD] = _rope128_bwd(acc[:, DN:HD], cs_ref[...]).astype(BF16)

    rep = pl.BlockSpec((None, tq, DN), lambda hh, i: (hh, i, 0))
    return pl.pallas_call(
        body, name=name, grid=(nh, s // tq),
        in_specs=[pl.BlockSpec((None, tq, HD), lambda hh, i: (hh, i, 0)),
                  pl.BlockSpec((None, s, HD), lambda hh, i: (hh, 0, 0)),
                  pl.BlockSpec((None, s, DN), lambda hh, i: (hh, 0, 0)),
                  pl.BlockSpec((tq, DN), lambda hh, i: (i, hh)),
                  rep, rep, pl.BlockSpec((tq, DN), lambda hh, i: (i, 0))],
        out_specs=pl.BlockSpec((None, tq, HD), lambda hh, i: (hh, i, 0)),
        out_shape=jax.ShapeDtypeStruct((nh, s, HD), BF16),
        scratch_shapes=[pltpu.VMEM((tq, HD), F32)],
        compiler_params=_cp("parallel", "arbitrary"),
    )(q, k, v, do, lse, delta, cs)


def flash_dkv(name, q, k, v, do, lse_row, delta_row, prev=None):
    nh, s, _ = q.shape
    tk, tq = _tile(s, TQ), _tile(s, TC)
    per = tk // tq
    nqc = s // tq
    has_prev = prev is not None

    def body(*refs):
        k_ref, v_ref, q_ref, do_ref, lse_ref, dl_ref = refs[:6]
        dk_ref, dv_ref, dk_acc, dv_acc = refs[-4:]
        i = pl.program_id(1)
        dk_acc[...] = jnp.zeros((tk, HD), F32)
        dv_acc[...] = jnp.zeros((tk, DN), F32)
        kv = k_ref[...]
        vv = v_ref[...]

        def step(j, masked):
            start = pl.multiple_of(j * tq, tq)
            qj = q_ref[pl.ds(start, tq), :]
            doj = do_ref[pl.ds(start, tq), :]
            sc = _dot(kv, qj, NT) * ATT_SCALE
            if masked:
                key = i * tk + lax.broadcasted_iota(jnp.int32, (tk, tq), 0)
                qry = j * tq + lax.broadcasted_iota(jnp.int32, (tk, tq), 1)
                sc = jnp.where(key <= qry, sc, NEG)
            p = jnp.exp(sc - lse_ref[j][0:1, :])
            dv_acc[...] += _dot(p.astype(BF16), doj, NN)
            dp = _dot(vv, doj, NT)
            ds = (p * (dp - dl_ref[j][0:1, :]) * ATT_SCALE).astype(BF16)
            dk_acc[...] += _dot(ds, qj, NN)

        def full_step(j, carry):
            step(j, False)
            return carry

        for dd in range(per):
            step(i * per + dd, True)
        lax.fori_loop((i + 1) * per, nqc, full_step, 0)
        if has_prev:
            dk_ref[...] = dk_acc[...] + refs[6][...]
            dv_ref[...] = dv_acc[...] + refs[7][...]
        else:
            dk_ref[...] = dk_acc[...]
            dv_ref[...] = dv_acc[...]

    kblk = pl.BlockSpec((None, tk, HD), lambda hh, i: (hh, i, 0))
    vblk = pl.BlockSpec((None, tk, DN), lambda hh, i: (hh, i, 0))
    stat = pl.BlockSpec((None, nqc, 8, tq), lambda hh, i: (hh, 0, 0, 0))
    in_specs = [kblk, vblk, pl.BlockSpec((None, s, HD), lambda hh, i: (hh, 0, 0)),
                pl.BlockSpec((s, DN), lambda hh, i: (0, hh)), stat, stat]
    args = [k, v, q, do, lse_row, delta_row]
    if has_prev:
        in_specs += [kblk, vblk]
        args += list(prev)
    return pl.pallas_call(
        body, name=name, grid=(nh, s // tk), in_specs=in_specs, out_specs=[kblk, vblk],
        out_shape=[jax.ShapeDtypeStruct((nh, s, HD), F32), jax.ShapeDtypeStruct((nh, s, DN), F32)],
        scratch_shapes=[pltpu.VMEM((tk, HD), F32), pltpu.VMEM((tk, DN), F32)],
        compiler_params=_cp("parallel", "arbitrary"),
    )(*args)


def loss_head(name, h, gain, target):
    s, d = h.shape
    tm = _tile(s, 1024)

    def body(h_ref, g_ref, t_ref, sq_ref, dh_ref, dg_ref):
        @pl.when(pl.program_id(0) == 0)
        def _():
            sq_ref[...] = jnp.zeros((1, d), F32)
            dg_ref[...] = jnp.zeros((1, d), F32)

        xv = h_ref[...]
        g = g_ref[...]
        rinv = _rinv(xv)
        xhat = xv * rinv
        err = xhat * g - t_ref[...]
        sq_ref[...] += jnp.sum(err * err, axis=0, keepdims=True)
        dy = err / d
        dg_ref[...] += jnp.sum(dy * xhat, axis=0, keepdims=True)
        dxh = dy * g
        dh_ref[...] = rinv * (dxh - xhat * jnp.mean(dxh * xhat, axis=-1, keepdims=True))

    row = pl.BlockSpec((tm, d), lambda i: (i, 0))
    vec = pl.BlockSpec((1, d), lambda i: (0, 0))
    return pl.pallas_call(
        body, name=name, grid=(s // tm,), in_specs=[row, vec, row], out_specs=[vec, row, vec],
        out_shape=[jax.ShapeDtypeStruct((1, d), F32), jax.ShapeDtypeStruct((s, d), F32),
                   jax.ShapeDtypeStruct((1, d), F32)],
        compiler_params=_cp("arbitrary"),
    )(h, gain, target)


def _adamw_math(w, g, m, v):
    m2 = ADAM_B1 * m + (1.0 - ADAM_B1) * g
    v2 = ADAM_B2 * v + (1.0 - ADAM_B2) * (g * g)
    mh = m2 / (1.0 - ADAM_B1 ** ADAM_STEP)
    vh = v2 / (1.0 - ADAM_B2 ** ADAM_STEP)
    delta = -ADAM_LR * (mh / (jnp.sqrt(vh) + ADAM_EPS) + ADAM_WD * w)
    return delta, m2, v2


def adamw(name, parts, w, m, v):
    rows, cols = w.shape
    tr = _tile(rows, max(8, (1 << 18) // cols))
    stacked = parts[0].ndim == 3
    npart = len(parts)

    def body(*refs):
        p_refs = refs[:npart]
        w_ref, m_ref, v_ref, g_ref, d_ref, m2_ref, v2_ref = refs[npart:]
        if stacked:
            g = p_refs[0][0]
            for n in range(1, parts[0].shape[0]):
                g = g + p_refs[0][n]
        else:
            g = p_refs[0][...]
            for r in p_refs[1:]:
                g = g + r[...]
        delta, m2, v2 = _adamw_math(w_ref[...], g, m_ref[...], v_ref[...])
        g_ref[...] = g
        d_ref[...] = delta
        m2_ref[...] = m2
        v2_ref[...] = v2

    blk = pl.BlockSpec((tr, cols), lambda i: (i, 0))
    pblk = pl.BlockSpec((parts[0].shape[0], tr, cols), lambda i: (0, i, 0)) if stacked else blk
    sds = jax.ShapeDtypeStruct((rows, cols), F32)
    return pl.pallas_call(
        body, name=name, grid=(rows // tr,), in_specs=[pblk] * npart + [blk] * 3,
        out_specs=[blk] * 4, out_shape=[sds] * 4, compiler_params=_cp("parallel"),
    )(*parts, w, m, v)


def sum4(name, x):
    _, a, r, c = x.shape
    tr = _tile(r, max(8, (1 << 18) // c))

    def body(x_ref, o_ref):
        o_ref[...] = ((x_ref[3].astype(F32) + x_ref[0].astype(F32)) + x_ref[1].astype(F32)) + x_ref[2].astype(F32)

    return pl.pallas_call(
        body, name=name, grid=(a, r // tr),
        in_specs=[pl.BlockSpec((4, None, tr, c), lambda i, j: (0, i, j, 0))],
        out_specs=pl.BlockSpec((None, tr, c), lambda i, j: (i, j, 0)),
        out_shape=jax.ShapeDtypeStruct((a, r, c), F32), compiler_params=_cp("parallel", "parallel"),
    )(x)


ANY = pl.BlockSpec(memory_space=pl.ANY)


def _mesh_pos():
    return lax.axis_index("x"), lax.axis_index("y"), lax.axis_index("c")


def _chip_peers(x, y):
    return [(1 - x, y), (x, 1 - y), (1 - x, 1 - y)]


def chip_all_gather(name, shards):
    n = len(shards)

    def body(*refs):
        ins, outs = refs[:n], refs[n:2 * n]
        send_sems, recv_sems, loc_sems = refs[2 * n:]
        x, y, c = _mesh_pos()
        me = 2 * x + y
        peers = _chip_peers(x, y)

        def remote(a, j, block):
            px, py = peers[j]
            return pltpu.make_async_remote_copy(
                src_ref=ins[a], dst_ref=outs[a].at[:, block], send_sem=send_sems.at[a * 3 + j],
                recv_sem=recv_sems.at[a * 3 + j], device_id=(px, py, c), device_id_type=MESH)

        local = [pltpu.make_async_copy(ins[a], outs[a].at[:, me], loc_sems.at[a]) for a in range(n)]
        for cp in local:
            cp.start()
        sends = [remote(a, j, me) for a in range(n) for j in range(3)]
        for cp in sends:
            cp.start()
        for a in range(n):
            for j, (px, py) in enumerate(peers):
                remote(a, j, 2 * px + py).wait_recv()
        for cp in sends:
            cp.wait_send()
        for cp in local:
            cp.wait()

    return pl.pallas_call(
        body, name=name, in_specs=[ANY] * n, out_specs=[ANY] * n,
        out_shape=[jax.ShapeDtypeStruct((s.shape[0], N_CHIPS) + s.shape[1:], s.dtype) for s in shards],
        scratch_shapes=[pltpu.SemaphoreType.DMA((3 * n,)), pltpu.SemaphoreType.DMA((3 * n,)),
                        pltpu.SemaphoreType.DMA((n,))],
    )(*shards)


def chip_all_to_all(name, fulls):
    n = len(fulls)

    def body(*refs):
        ins, outs = refs[:n], refs[n:2 * n]
        send_sems, recv_sems, loc_sems = refs[2 * n:]
        x, y, c = _mesh_pos()
        me = 2 * x + y
        peers = _chip_peers(x, y)

        def remote(a, j):
            px, py = peers[j]
            return pltpu.make_async_remote_copy(
                src_ref=ins[a].at[:, 2 * px + py], dst_ref=outs[a].at[j], send_sem=send_sems.at[a * 3 + j],
                recv_sem=recv_sems.at[a * 3 + j], device_id=(px, py, c), device_id_type=MESH)

        local = [pltpu.make_async_copy(ins[a].at[:, me], outs[a].at[3], loc_sems.at[a]) for a in range(n)]
        for cp in local:
            cp.start()
        sends = [remote(a, j) for a in range(n) for j in range(3)]
        for cp in sends:
            cp.start()
        for cp in sends:
            cp.wait_recv()
        for cp in sends:
            cp.wait_send()
        for cp in local:
            cp.wait()

    return pl.pallas_call(
        body, name=name, in_specs=[ANY] * n, out_specs=[ANY] * n,
        out_shape=[jax.ShapeDtypeStruct((N_CHIPS, f.shape[0]) + f.shape[2:], f.dtype) for f in fulls],
        scratch_shapes=[pltpu.SemaphoreType.DMA((3 * n,)), pltpu.SemaphoreType.DMA((3 * n,)),
                        pltpu.SemaphoreType.DMA((n,))],
    )(*fulls)


def sibling_exchange(name, arrs):
    n = len(arrs)

    def body(*refs):
        ins, outs = refs[:n], refs[n:2 * n]
        send_sems, recv_sems = refs[2 * n:]
        x, y, c = _mesh_pos()
        copies = [pltpu.make_async_remote_copy(
            src_ref=ins[a], dst_ref=outs[a], send_sem=send_sems.at[a], recv_sem=recv_sems.at[a],
            device_id=(x, y, 1 - c), device_id_type=MESH) for a in range(n)]
        for cp in copies:
            cp.start()
        for cp in copies:
            cp.wait_recv()
        for cp in copies:
            cp.wait_send()

    return pl.pallas_call(
        body, name=name, in_specs=[ANY] * n, out_specs=[ANY] * n,
        out_shape=[jax.ShapeDtypeStruct(a.shape, a.dtype) for a in arrs],
        scratch_shapes=[pltpu.SemaphoreType.DMA((n,)), pltpu.SemaphoreType.DMA((n,))],
    )(*arrs)


def all_gather8(name, buf):
    r, cdim = buf.shape

    def body(in_ref, out_ref, send_sems, recv_sems, loc_sem):
        x, y, c = _mesh_pos()
        flips = [(fx, fy, fc) for fx in (0, 1) for fy in (0, 1) for fc in (0, 1)][1:]

        def peer(f):
            return tuple((1 - p) if b else p for p, b in zip((x, y, c), f))

        def remote(j, slot):
            return pltpu.make_async_remote_copy(
                src_ref=in_ref, dst_ref=out_ref.at[slot], send_sem=send_sems.at[j], recv_sem=recv_sems.at[j],
                device_id=peer(flips[j]), device_id_type=MESH)

        me = 4 * x + 2 * y + c
        local = pltpu.make_async_copy(in_ref, out_ref.at[me], loc_sem)
        local.start()
        sends = [remote(j, me) for j in range(7)]
        for cp in sends:
            cp.start()
        for j in range(7):
            px, py, pc = peer(flips[j])
            remote(j, 4 * px + 2 * py + pc).wait_recv()
        for cp in sends:
            cp.wait_send()
        local.wait()

    return pl.pallas_call(
        body, name=name, in_specs=[ANY], out_specs=ANY,
        out_shape=jax.ShapeDtypeStruct((8, r, cdim), buf.dtype),
        scratch_shapes=[pltpu.SemaphoreType.DMA((7,)), pltpu.SemaphoreType.DMA((7,)), pltpu.SemaphoreType.DMA],
    )(buf)


def _rot_cols(w):
    half = w.shape[-1] // 2
    return jnp.concatenate([-w[..., half:], w[..., :half]], axis=-1)


def _fold_rot(g):
    rot = g[..., DN + DR:]
    half = DR // 2
    return jnp.concatenate([g[..., :DN], g[..., DN:DN + DR] + jnp.concatenate([rot[..., half:], -rot[..., :half]], -1)], -1)


def _with_rot(w):
    return jnp.concatenate([w, _rot_cols(w[..., DN:])], axis=-1)


def _rope_table(s):
    pos = jnp.arange(s, dtype=F32)
    inv_freq = ROPE_THETA ** (-jnp.arange(0, DR, 2, dtype=F32) / DR)
    ang = pos[:, None] * inv_freq[None, :]
    cos, sin = jnp.cos(ang), jnp.sin(ang)
    return jnp.concatenate([cos, cos, sin, sin], axis=-1)


def _row_stats(x, tq):
    nh, s, _ = x.shape
    return jnp.broadcast_to(x[:, :, 0].reshape(nh, s // tq, 1, tq), (nh, s // tq, 8, tq))


REPL_NAMES = ["ffn_pre_norm", "mix_norm", "ffn_post_norm", "kv_in_norm", "final_norm", "ckv_norm",
              "q_lora_norm", "pool_scale_full", "w_uk", "w_uv"]
PACK_COLS = 1024


def _pack_rows(arrs):
    rows, counts = [], []
    for a in arrs:
        flat = a.reshape(-1).astype(F32)
        n = -(-flat.shape[0] // PACK_COLS)
        flat = jnp.pad(flat, (0, n * PACK_COLS - flat.shape[0]))
        rows.append(flat.reshape(n, PACK_COLS))
        counts.append(n)
    total = sum(counts)
    pad = -total % 8
    if pad:
        rows.append(jnp.zeros((pad, PACK_COLS), F32))
    return jnp.concatenate(rows, axis=0), counts


def _unpack_rows(buf, counts, shapes):
    out, r = [], 0
    for n, shp in zip(counts, shapes):
        size = 1
        for dd in shp:
            size *= dd
        out.append(buf[r:r + n].reshape(-1)[:size].reshape(shp))
        r += n
    return out


def kernel(x, ffn_pre_norm, ffn_pre_wg, ffn_pre_wu, ffn_pre_wd, mix_norm, ffn_post_norm, ffn_post_wg, ffn_post_wu, ffn_post_wd, pool_w, pool_scale, kv_in_norm, w_dkv, ckv_norm, w_uk, w_uv, q_lora_norm, w_dq, w_uq, w_o, final_norm, loss_target, m_ffn_pre_norm, m_ffn_pre_wg, m_ffn_pre_wu, m_ffn_pre_wd, m_mix_norm, m_ffn_post_norm, m_ffn_post_wg, m_ffn_post_wu, m_ffn_post_wd, m_pool_w, m_pool_scale, m_kv_in_norm, m_w_dkv, m_ckv_norm, m_w_uk, m_w_uv, m_q_lora_norm, m_w_dq, m_w_uq, m_w_o, m_final_norm, v_ffn_pre_norm, v_ffn_pre_wg, v_ffn_pre_wu, v_ffn_pre_wd, v_mix_norm, v_ffn_post_norm, v_ffn_post_wg, v_ffn_post_wu, v_ffn_post_wd, v_pool_w, v_pool_scale, v_kv_in_norm, v_w_dkv, v_ckv_norm, v_w_uk, v_w_uv, v_q_lora_norm, v_w_dq, v_w_uq, v_w_o, v_final_norm):
    weights = dict(ffn_pre_norm=ffn_pre_norm, ffn_pre_wg=ffn_pre_wg, ffn_pre_wu=ffn_pre_wu, ffn_pre_wd=ffn_pre_wd,
                   mix_norm=mix_norm, ffn_post_norm=ffn_post_norm, ffn_post_wg=ffn_post_wg, ffn_post_wu=ffn_post_wu,
                   ffn_post_wd=ffn_post_wd, pool_w=pool_w, pool_scale=pool_scale, kv_in_norm=kv_in_norm, w_dkv=w_dkv,
                   ckv_norm=ckv_norm, w_uk=w_uk, w_uv=w_uv, q_lora_norm=q_lora_norm, w_dq=w_dq, w_uq=w_uq, w_o=w_o,
                   final_norm=final_norm)
    mom1 = dict(ffn_pre_norm=m_ffn_pre_norm, ffn_pre_wg=m_ffn_pre_wg, ffn_pre_wu=m_ffn_pre_wu, ffn_pre_wd=m_ffn_pre_wd,
                mix_norm=m_mix_norm, ffn_post_norm=m_ffn_post_norm, ffn_post_wg=m_ffn_post_wg, ffn_post_wu=m_ffn_post_wu,
                ffn_post_wd=m_ffn_post_wd, pool_w=m_pool_w, pool_scale=m_pool_scale, kv_in_norm=m_kv_in_norm,
                w_dkv=m_w_dkv, ckv_norm=m_ckv_norm, w_uk=m_w_uk, w_uv=m_w_uv, q_lora_norm=m_q_lora_norm, w_dq=m_w_dq,
                w_uq=m_w_uq, w_o=m_w_o, final_norm=m_final_norm)
    mom2 = dict(ffn_pre_norm=v_ffn_pre_norm, ffn_pre_wg=v_ffn_pre_wg, ffn_pre_wu=v_ffn_pre_wu, ffn_pre_wd=v_ffn_pre_wd,
                mix_norm=v_mix_norm, ffn_post_norm=v_ffn_post_norm, ffn_post_wg=v_ffn_post_wg, ffn_post_wu=v_ffn_post_wu,
                ffn_post_wd=v_ffn_post_wd, pool_w=v_pool_w, pool_scale=v_pool_scale, kv_in_norm=v_kv_in_norm,
                w_dkv=v_w_dkv, ckv_norm=v_ckv_norm, w_uk=v_w_uk, w_uv=v_w_uv, q_lora_norm=v_q_lora_norm, w_dq=v_w_dq,
                w_uq=v_w_uq, w_o=v_w_o, final_norm=v_final_norm)
    names = list(weights)

    h0 = x[0]
    target = loss_target[0]
    s, d = h0.shape
    nh = N_HEADS
    nb = DEPTH - N_A
    f_loc = ffn_pre_wd.shape[1]
    ffn_dim = N_CHIPS * f_loc
    my_chip = 2 * lax.axis_index("x") + lax.axis_index("y")

    hidden_major = lambda w: jnp.transpose(w, (0, 2, 1)).astype(BF16)
    uq_loc = jnp.transpose(w_uq, (0, 2, 1, 3))
    shards = [
        hidden_major(ffn_pre_wg), hidden_major(ffn_pre_wu), ffn_pre_wd.astype(BF16),
        hidden_major(ffn_post_wg), hidden_major(ffn_post_wu), ffn_post_wd.astype(BF16),
        pool_w.reshape(N_A * 4, PG // N_CHIPS, PG).astype(BF16),
        _with_rot(w_dkv)[None].astype(BF16),
        w_dq.astype(BF16),
        _with_rot(uq_loc).reshape(nb * nh, uq_loc.shape[2], HD).astype(BF16),
        w_o.astype(BF16),
        pool_scale.reshape(N_A, 1, PG),
    ]
    gathered = chip_all_gather("gather_weights", shards)
    merge = lambda g: g.reshape(g.shape[0], N_CHIPS * g.shape[2], g.shape[3])
    pre_wgT, pre_wuT, pre_wd, post_wgT, post_wuT, post_wd = [merge(g) for g in gathered[:6]]
    pool_w_full = merge(gathered[6]).reshape(N_A, 4, PG, PG)
    wdkv256 = merge(gathered[7])[0]
    wdq_full = merge(gathered[8])
    wq256 = merge(gathered[9]).reshape(nb, nh, HD, HD)
    wo_full = merge(gathered[10])
    pool_scale_full = gathered[11].reshape(N_A, d)
    wuk_h = jnp.transpose(w_uk, (1, 0, 2)).astype(BF16)
    wuv_h = jnp.transpose(w_uv, (1, 0, 2)).astype(BF16)
    cs = _rope_table(s)

    ffn_w = {"pre": (ffn_pre_norm, pre_wgT, pre_wuT, pre_wd), "post": (ffn_post_norm, post_wgT, post_wuT, post_wd)}

    saved = {}
    h = h0
    kv = None
    for l in range(DEPTH):
        saved["a", l] = h
        g_, wg_, wu_, wd_ = ffn_w["pre"]
        h = ffn_fwd(f"ffn_pre_fwd{l}", h, g_[l:l + 1], wg_, wu_, wd_, l)
        saved["b", l] = h
        if l < N_A:
            h, y = pool_fwd(f"pool_fwd{l}", h, mix_norm[l:l + 1], pool_w_full, pool_scale_full[l:l + 1], l)
            saved["y", l] = y
        else:
            j = l - N_A
            u = rms_fwd(f"mix_norm_fwd{l}", h, mix_norm[l:l + 1])
            cq_raw = mm_nn(f"q_down{l}", u, wdq_full[j], F32, 2048, 256, 1024)
            cq = rms_fwd(f"q_norm_fwd{l}", cq_raw, q_lora_norm[j:j + 1])
            q = q_heads(f"q_heads{l}", cq, wq256[j], cs)
            o, lse = flash_fwd(f"flash_fwd{l}", q, kv[0], kv[1])
            saved["att", l] = (u, cq_raw, cq, q, o, lse)
            h = mm_nn(f"attn_out{l}", o, wo_full[j], F32, 1024, 1024, 1024, res=h)
        saved["c", l] = h
        g_, wg_, wu_, wd_ = ffn_w["post"]
        h = ffn_fwd(f"ffn_post_fwd{l}", h, g_[l:l + 1], wg_, wu_, wd_, l)
        if l == N_A - 1:
            u_kv = rms_fwd("kv_in_norm_fwd", h, kv_in_norm[None])
            kv_raw = mm_nn("kv_down", u_kv, wdkv256, F32, 2048, 256, 1024)
            c_kv, kr = kv_post("kv_post", kv_raw, ckv_norm[None], cs)
            kv = kv_heads("kv_heads", c_kv, kr, wuk_h, wuv_h)
            saved["kv"] = (u_kv, kv_raw, c_kv)

    sq, dh, g_final = loss_head("loss_head", h, final_norm[None], target)
    loss = lax.psum(0.5 * jnp.sum(sq) / d, ("x", "y", "c"))

    grads = {}
    gvec = {n: [None] * DEPTH for n in ("ffn_pre_norm", "mix_norm", "ffn_post_norm")}
    gffn = {(p, m): [None] * DEPTH for p in ("pre", "post") for m in ("wg", "wu", "wd")}
    g_pool_w, g_pool_scale = [None] * N_A, [None] * N_A
    g_qnorm, g_wdq, g_wq256, g_wo = [None] * nb, [None] * nb, [None] * nb, [None] * nb
    dkv = None

    def ffn_backward(part, l, hin, dh):
        g_, wg_, wu_, wd_ = ffn_w[part]
        dh, u, dob, act, da, db, dgain = ffn_bwd(f"ffn_{part}_bwd{l}", hin, g_[l:l + 1], wg_, wu_, wd_, l, dh)
        gvec[f"ffn_{part}_norm"][l] = dgain[0]
        gffn[part, "wg"][l] = mm_tn(f"ffn_{part}_dwg{l}", da, u, BF16, ffn_dim // 2, d, 1024)
        gffn[part, "wu"][l] = mm_tn(f"ffn_{part}_dwu{l}", db, u, BF16, ffn_dim // 2, d, 1024)
        gffn[part, "wd"][l] = mm_tn(f"ffn_{part}_dwd{l}", act, dob, BF16, ffn_dim // 2, d, 1024)
        return dh

    for l in reversed(range(DEPTH)):
        if l == N_A - 1:
            u_kv, kv_raw, c_kv = saved["kv"]
            dk, dv = dkv
            tm = _tile(s, 2048)
            head_red = dict(grid=(s // tm, 1, nh), b_block=(None, DN, DN), b_map=lambda i, jj, kk: (kk, 0, 0),
                            o_block=(tm, DN), o_map=lambda i, jj, kk: (i, 0), out_shape=(s, DN), dims=NT, out_dtype=F32)
            dc = _mm("kv_dc_k", dk, wuk_h, a_block=(None, tm, DN), a_map=lambda i, jj, kk: (kk, i, 0), **head_red)
            dc = _mm("kv_dc_v", dv, wuv_h, a_block=(None, tm, DN), a_map=lambda i, jj, kk: (kk, i, 0), res=dc, **head_red)
            tk = _tile(s, 2048)
            head_tn = dict(grid=(nh, 1, s // tk), a_block=(tk, DN), a_map=lambda hh, jj, kk: (kk, 0),
                           b_block=(None, tk, DN), b_map=lambda hh, jj, kk: (hh, kk, 0),
                           o_block=(None, DN, DN), o_map=lambda hh, jj, kk: (hh, 0, 0),
                           out_shape=(nh, DN, DN), dims=TN, out_dtype=F32)
            grads["w_uk"] = jnp.transpose(_mm("kv_dwuk", c_kv, dk, **head_tn), (1, 0, 2))
            grads["w_uv"] = jnp.transpose(_mm("kv_dwuv", c_kv, dv, **head_tn), (1, 0, 2))
            dkv_raw, dg_ckv = kv_post_bwd("kv_post_bwd", kv_raw, ckv_norm[None], dc, dk, cs)
            grads["ckv_norm"] = dg_ckv[0]
            du_kv = mm_nt("kv_du", dkv_raw, wdkv256, F32, 2048, 1024, 256)
            g_wdkv256 = mm_tn("kv_dwdkv", u_kv, dkv_raw, BF16, 1024, 256, 2048)
            dh, dg_kvin = rms_bwd("kv_in_norm_bwd", saved["a", l + 1], kv_in_norm[None], du_kv, dres=dh)
            grads["kv_in_norm"] = dg_kvin[0]

        dh = ffn_backward("post", l, saved["c", l], dh)

        if l < N_A:
            dh, dz, dgain, dscale = pool_bwd(f"pool_bwd{l}", saved["b", l], mix_norm[l:l + 1], saved["y", l],
                                             pool_w_full, pool_scale_full[l:l + 1], l, dh)
            gvec["mix_norm"][l] = dgain[0]
            g_pool_scale[l] = dscale[0]
            tk = _tile(s, 2048)
            g_pool_w[l] = _mm(f"pool_dw{l}", saved["y", l], dz, grid=(4, 1, s // tk),
                              a_block=(tk, PG), a_map=lambda g, jj, kk: (kk, g),
                              b_block=(tk, PG), b_map=lambda g, jj, kk: (kk, g),
                              o_block=(None, PG, PG), o_map=lambda g, jj, kk: (g, 0, 0),
                              out_shape=(4, PG, PG), dims=TN, out_dtype=BF16)
        else:
            j = l - N_A
            u, cq_raw, cq, q, o, lse = saved["att", l]
            k_, v_ = kv
            do = mm_nt(f"attn_do{l}", dh, wo_full[j], BF16, 1024, 1024, 1024)
            g_wo[j] = mm_tn(f"attn_dwo{l}", o, dh, BF16, 1024, 1024, 1024)
            delta = attn_delta(f"attn_delta{l}", o, do, nh)
            dq_raw = flash_dq(f"flash_dq{l}", q, k_, v_, do, lse, delta, cs)
            tc = _tile(s, TC)
            dkv = flash_dkv(f"flash_dkv{l}", q, k_, v_, do, _row_stats(lse, tc), _row_stats(delta, tc), prev=dkv)
            tm = _tile(s, 2048)
            dcq = _mm(f"q_dcq{l}", dq_raw, wq256[j], grid=(s // tm, 1, nh),
                      a_block=(None, tm, HD), a_map=lambda i, jj, kk: (kk, i, 0),
                      b_block=(None, HD, HD), b_map=lambda i, jj, kk: (kk, 0, 0),
                      o_block=(tm, HD), o_map=lambda i, jj, kk: (i, 0),
                      out_shape=(s, HD), dims=NT, out_dtype=F32)
            tk = _tile(s, 2048)
            g_wq256[j] = _mm(f"q_dwq{l}", cq, dq_raw, grid=(nh, 1, s // tk),
                             a_block=(tk, HD), a_map=lambda hh, jj, kk: (kk, 0),
                             b_block=(None, tk, HD), b_map=lambda hh, jj, kk: (hh, kk, 0),
                             o_block=(None, HD, HD), o_map=lambda hh, jj, kk: (hh, 0, 0),
                             out_shape=(nh, HD, HD), dims=TN, out_dtype=BF16)
            dcq_raw, dg_q = rms_bwd(f"q_norm_bwd{l}", cq_raw, q_lora_norm[j:j + 1], dcq)
            g_qnorm[j] = dg_q[0]
            du = mm_nt(f"q_du{l}", dcq_raw, wdq_full[j], F32, 2048, 1024, 256)
            g_wdq[j] = mm_tn(f"q_dwdq{l}", u, dcq_raw, BF16, 1024, 256, 2048)
            dh, dgain = rms_bwd(f"mix_norm_bwd{l}", saved["b", l], mix_norm[l:l + 1], du, dres=dh)
            gvec["mix_norm"][l] = dgain[0]

        dh = ffn_backward("pre", l, saved["a", l], dh)

    grad_x = dh[None]

    split = lambda g: g.reshape(g.shape[0], N_CHIPS, g.shape[1] // N_CHIPS, g.shape[2])
    fulls = [split(jnp.stack(gffn[p, m])) for p in ("pre", "post") for m in ("wg", "wu", "wd")]
    fulls += [
        split(jnp.stack(g_pool_w).reshape(N_A * 4, PG, PG)),
        split(g_wdkv256[None]),
        split(jnp.stack(g_wdq)),
        split(jnp.stack(g_wq256).reshape(nb * nh, HD, HD)),
        split(jnp.stack(g_wo)),
    ]
    sharded_names = ["ffn_pre_wg", "ffn_pre_wu", "ffn_pre_wd", "ffn_post_wg", "ffn_post_wu", "ffn_post_wd",
                     "pool_w", "w_dkv", "w_dq", "w_uq", "w_o"]
    received = chip_all_to_all("exchange_grads", fulls)
    partial = [sum4(f"chip_sum_{n}", r) for n, r in zip(sharded_names, received)]
    sibling = sibling_exchange("exchange_sibling", partial)

    def natural(name, p):
        if name in ("ffn_pre_wg", "ffn_pre_wu", "ffn_post_wg", "ffn_post_wu"):
            return jnp.transpose(p, (0, 2, 1))
        if name == "w_dkv":
            return _fold_rot(p[0])
        if name == "w_uq":
            return jnp.transpose(_fold_rot(p).reshape(nb, nh, p.shape[1], QK_DIM), (0, 2, 1, 3))
        return p.reshape(weights[name].shape)

    results = {}
    for n, p, q_ in zip(sharded_names, partial, sibling):
        shp = weights[n].shape
        as2d = lambda a: a.reshape(-1, shp[-1])
        outs = adamw(f"adamw_{n}", [as2d(natural(n, p)), as2d(natural(n, q_))],
                     as2d(weights[n]), as2d(mom1[n]), as2d(mom2[n]))
        results[n] = [o_.reshape(shp) for o_ in outs]

    grads.update(ffn_pre_norm=jnp.stack(gvec["ffn_pre_norm"]), mix_norm=jnp.stack(gvec["mix_norm"]),
                 ffn_post_norm=jnp.stack(gvec["ffn_post_norm"]), final_norm=g_final[0],
                 q_lora_norm=jnp.stack(g_qnorm), pool_scale_full=jnp.stack(g_pool_scale))
    repl_shapes = [grads[n].shape for n in REPL_NAMES]
    packed_g, counts = _pack_rows([grads[n] for n in REPL_NAMES])
    all_g = all_gather8("gather_small_grads", packed_g)
    zeros_ps = jnp.zeros((N_A, d), F32)
    pick = lambda src: [zeros_ps if n == "pool_scale_full" else src[n] for n in REPL_NAMES]
    packed = [_pack_rows(pick(src))[0] for src in (weights, mom1, mom2)]
    outs = adamw("adamw_replicated", [all_g], *packed)
    unpacked = [_unpack_rows(o_, counts, repl_shapes) for o_ in outs]
    for idx, n in enumerate(REPL_NAMES):
        results[n] = [u_[idx] for u_ in unpacked]
    g_ps = lax.dynamic_index_in_dim(results["pool_scale_full"][0].reshape(N_A, N_CHIPS, PG), my_chip, axis=1, keepdims=False)
    results["pool_scale"] = adamw("adamw_pool_scale", [g_ps], pool_scale, m_pool_scale, v_pool_scale)

    out = [loss, grad_x]
    for part in range(4):
        out += [results[n][part] for n in names]
    return tuple(out)
```

```python
import functools

import jax
import jax.numpy as jnp
from jax import lax
from jax.experimental import pallas as pl
from jax.experimental.pallas import tpu as pltpu

F32 = jnp.float32
BF16 = jnp.bfloat16
MESH = pl.DeviceIdType.MESH

DEPTH = 4
N_A = 2
N_HEADS = 16
DN = 128
DR = 64
HD = 256
QK_DIM = DN + DR
POOL_WINDOWS = (2, 4, 8, 16)
PG = 256
HALO = 16
N_CHIPS = 4
RMS_EPS = 1e-6
ROPE_THETA = 10000.0
ATT_SCALE = QK_DIM ** -0.5
LOG2E = 1.4426950408889634
LN2 = 0.6931471805599453
Q_PRESCALE = ATT_SCALE * LOG2E
NEG = -1e30
LANES = 128

ADAM_LR = 0.001
ADAM_B1 = 0.9
ADAM_B2 = 0.999
ADAM_EPS = 1e-08
ADAM_WD = 0.01
ADAM_STEP = 10

VMEM_BYTES_V7X = 64 * 1024 * 1024
VMEM_LIMIT = VMEM_BYTES_V7X * 3 // 4
BIG_VMEM = VMEM_BYTES_V7X * 7 // 8
FFN_RB = 64

NN = ((1,), (0,))
NT = ((1,), (1,))
TN = ((0,), (0,))


def _cp(*sem):
    return pltpu.CompilerParams(dimension_semantics=sem, vmem_limit_bytes=VMEM_LIMIT)


def _dot(a, b, dims):
    return lax.dot_general(a, b, (dims, ((), ())), preferred_element_type=F32)


def _tile(n, pref):
    if n <= pref:
        return n
    if n % pref == 0:
        return pref
    t = 1 << (pref.bit_length() - 1)
    while n % t:
        t //= 2
    return t


def _rinv(x):
    return lax.rsqrt(jnp.mean(x * x, axis=-1, keepdims=True) + RMS_EPS)


def _sigmoid(a):
    return 1.0 / (1.0 + jnp.exp(-a))


def _mm(name, a, b, *, grid, a_block, a_map, b_block, b_map, o_block, o_map, out_shape, dims,
        out_dtype, res=None):
    nax = len(grid)
    nk = grid[-1]
    acc_shape = tuple(d for d in o_block if d is not None)
    has_res = res is not None

    def body(*refs):
        a_ref, b_ref = refs[0], refs[1]
        o_ref, acc = refs[-2], refs[-1]
        k = pl.program_id(nax - 1)

        @pl.when(k == 0)
        def _():
            acc[...] = jnp.zeros(acc_shape, F32)

        acc[...] += _dot(a_ref[...].astype(BF16), b_ref[...].astype(BF16), dims)

        @pl.when(k == nk - 1)
        def _():
            r = acc[...]
            if has_res:
                r = r + refs[2][...].astype(F32)
            o_ref[...] = r.astype(out_dtype)

    in_specs = [pl.BlockSpec(a_block, a_map), pl.BlockSpec(b_block, b_map)]
    args = [a, b]
    if has_res:
        in_specs.append(pl.BlockSpec(o_block, o_map))
        args.append(res)
    return pl.pallas_call(
        body, name=name, grid=grid, in_specs=in_specs, out_specs=pl.BlockSpec(o_block, o_map),
        out_shape=jax.ShapeDtypeStruct(out_shape, out_dtype),
        scratch_shapes=[pltpu.VMEM(acc_shape, F32)],
        compiler_params=_cp(*(("parallel",) * (nax - 1) + ("arbitrary",))),
    )(*args)


def mm_nn(name, a, b, out_dtype, tm, tn, tk, res=None):
    (m, kd), n = a.shape, b.shape[1]
    tm, tn, tk = _tile(m, tm), _tile(n, tn), _tile(kd, tk)
    return _mm(name, a, b, grid=(m // tm, n // tn, kd // tk),
               a_block=(tm, tk), a_map=lambda i, j, k: (i, k),
               b_block=(tk, tn), b_map=lambda i, j, k: (k, j),
               o_block=(tm, tn), o_map=lambda i, j, k: (i, j),
               out_shape=(m, n), dims=NN, out_dtype=out_dtype, res=res)


def mm_nt(name, a, b, out_dtype, tm, tn, tk):
    (m, kd), n = a.shape, b.shape[0]
    tm, tn, tk = _tile(m, tm), _tile(n, tn), _tile(kd, tk)
    return _mm(name, a, b, grid=(m // tm, n // tn, kd // tk),
               a_block=(tm, tk), a_map=lambda i, j, k: (i, k),
               b_block=(tn, tk), b_map=lambda i, j, k: (j, k),
               o_block=(tm, tn), o_map=lambda i, j, k: (i, j),
               out_shape=(m, n), dims=NT, out_dtype=out_dtype)


def mm_tn(name, a, b, out_dtype, tm, tn, tk):
    (kd, m), n = a.shape, b.shape[1]
    tm, tn, tk = _tile(m, tm), _tile(n, tn), _tile(kd, tk)
    return _mm(name, a, b, grid=(m // tm, n // tn, kd // tk),
               a_block=(tk, tm), a_map=lambda i, j, k: (k, i),
               b_block=(tk, tn), b_map=lambda i, j, k: (k, j),
               o_block=(tm, tn), o_map=lambda i, j, k: (i, j),
               out_shape=(m, n), dims=TN, out_dtype=out_dtype)


def rms_fwd(name, x, gain):
    s, d = x.shape
    tm = _tile(s, 1024)

    def body(x_ref, g_ref, o_ref):
        xv = x_ref[...]
        o_ref[...] = (xv * _rinv(xv) * g_ref[...]).astype(BF16)

    return pl.pallas_call(
        body, name=name, grid=(s // tm,),
        in_specs=[pl.BlockSpec((tm, d), lambda i: (i, 0)), pl.BlockSpec((1, d), lambda i: (0, 0))],
        out_specs=pl.BlockSpec((tm, d), lambda i: (i, 0)),
        out_shape=jax.ShapeDtypeStruct((s, d), BF16), compiler_params=_cp("parallel"),
    )(x, gain)


def _rms_bwd_math(xv, g, du):
    rinv = _rinv(xv)
    xhat = xv * rinv
    dgain = jnp.sum(du * xhat, axis=0, keepdims=True)
    dxh = du * g
    dx = rinv * (dxh - xhat * jnp.mean(dxh * xhat, axis=-1, keepdims=True))
    return dx, dgain


def rms_bwd(name, x, gain, du, dres=None):
    s, d = x.shape
    tm = _tile(s, 1024)
    has_res = dres is not None

    def body(*refs):
        x_ref, g_ref, du_ref = refs[:3]
        dx_ref, dg_ref = refs[-2:]

        @pl.when(pl.program_id(0) == 0)
        def _():
            dg_ref[...] = jnp.zeros((1, d), F32)

        dx, dgain = _rms_bwd_math(x_ref[...], g_ref[...], du_ref[...].astype(F32))
        if has_res:
            dx = dx + refs[3][...]
        dx_ref[...] = dx
        dg_ref[...] += dgain

    row = pl.BlockSpec((tm, d), lambda i: (i, 0))
    vec = pl.BlockSpec((1, d), lambda i: (0, 0))
    args = [x, gain, du] + ([dres] if has_res else [])
    return pl.pallas_call(
        body, name=name, grid=(s // tm,),
        in_specs=[row, vec, row] + ([row] if has_res else []),
        out_specs=[row, vec],
        out_shape=[jax.ShapeDtypeStruct((s, d), F32), jax.ShapeDtypeStruct((1, d), F32)],
        compiler_params=_cp("arbitrary"),
    )(*args)


def ffn_fwd(name, h, gain, wgT, wuT, wd, layer):
    s, d = h.shape
    f = wd.shape[1]
    tm, tf = _tile(s, 1024), _tile(f, 256)
    nf = f // tf

    def body(h_ref, g_ref, wg_ref, wu_ref, wd_ref, o_ref, u_sc, acc):
        j = pl.program_id(1)

        @pl.when(j == 0)
        def _():
            xv = h_ref[...]
            u_sc[...] = (xv * _rinv(xv) * g_ref[...]).astype(BF16)
            acc[...] = jnp.zeros((tm, d), F32)

        u = u_sc[...]
        a = _dot(u, wg_ref[...], NT)
        b = _dot(u, wu_ref[...], NT)
        act = (a * _sigmoid(a) * b).astype(BF16)
        acc[...] += _dot(act, wd_ref[...], NN)

        @pl.when(j == nf - 1)
        def _():
            o_ref[...] = h_ref[...] + 0.5 * acc[...]

    row = pl.BlockSpec((tm, d), lambda i, j: (i, 0))
    wsp = pl.BlockSpec((None, tf, d), lambda i, j: (layer, j, 0))
    return pl.pallas_call(
        body, name=name, grid=(s // tm, nf),
        in_specs=[row, pl.BlockSpec((1, d), lambda i, j: (0, 0)), wsp, wsp, wsp],
        out_specs=row, out_shape=jax.ShapeDtypeStruct((s, d), F32),
        scratch_shapes=[pltpu.VMEM((tm, d), BF16), pltpu.VMEM((tm, d), F32)],
        compiler_params=_cp("parallel", "arbitrary"),
    )(h, gain, wgT, wuT, wd)


def ffn_bwd(name, h, gain, wgT, wuT, wd, layer, dout):
    s, d = h.shape
    f = wd.shape[1]
    tm, tf = _tile(s, 1024), _tile(f, 256)
    nf = f // tf
    halves = 2 if tm % (2 * FFN_RB) == 0 else 1
    hm = tm // halves

    def body(h_ref, g_ref, wg_ref, wu_ref, wd_ref, do_ref,
             dh_ref, u_ref, dob_ref, act_ref, da_ref, db_ref, dg_ref, du_acc, a_sc, b_sc, c_sc):
        i, j = pl.program_id(0), pl.program_id(1)

        @pl.when(j == 0)
        def _():
            xv = h_ref[...]
            u_ref[...] = (xv * _rinv(xv) * g_ref[...]).astype(BF16)
            dob_ref[...] = (0.5 * do_ref[...]).astype(BF16)
            du_acc[...] = jnp.zeros((tm, d), F32)

        @pl.when((i == 0) & (j == 0))
        def _():
            dg_ref[...] = jnp.zeros((1, d), F32)

        wg, wu = wg_ref[...], wu_ref[...]
        for part in range(halves):
            prow = pl.ds(part * hm, hm)
            u = u_ref[prow, :]
            a_sc[prow, :] = _dot(u, wg, NT)
            b_sc[prow, :] = _dot(u, wu, NT)
            c_sc[prow, :] = _dot(dob_ref[prow, :], wd_ref[...], NT)
            for r in range(hm // FFN_RB):
                rows = pl.ds(part * hm + r * FFN_RB, FFN_RB)
                for t in range(tf // LANES):
                    cols = slice(t * LANES, (t + 1) * LANES)
                    a, b, dact = a_sc[rows, cols], b_sc[rows, cols], c_sc[rows, cols]
                    sig = _sigmoid(a)
                    sa = a * sig
                    act_ref[rows, cols] = (sa * b).astype(BF16)
                    db_ref[rows, cols] = (dact * sa).astype(BF16)
                    da_ref[rows, cols] = (dact * b * (sig * (1.0 + a * (1.0 - sig)))).astype(BF16)
            du_acc[prow, :] += _dot(da_ref[prow, :], wg, NN) + _dot(db_ref[prow, :], wu, NN)

        @pl.when(j == nf - 1)
        def _():
            dx, dgain = _rms_bwd_math(h_ref[...], g_ref[...], du_acc[...])
            dh_ref[...] = do_ref[...] + dx
            dg_ref[...] += dgain

    row = pl.BlockSpec((tm, d), lambda i, j: (i, 0))
    vec = pl.BlockSpec((1, d), lambda i, j: (0, 0))
    wsp = pl.BlockSpec((None, tf, d), lambda i, j: (layer, j, 0))
    hid = pl.BlockSpec((tm, tf), lambda i, j: (i, j))
    sd = lambda dt: jax.ShapeDtypeStruct((s, d), dt)
    sf = jax.ShapeDtypeStruct((s, f), BF16)
    return pl.pallas_call(
        body, name=name, grid=(s // tm, nf),
        in_specs=[row, vec, wsp, wsp, wsp, row],
        out_specs=[row, row, row, hid, hid, hid, vec],
        out_shape=[sd(F32), sd(BF16), sd(BF16), sf, sf, sf, jax.ShapeDtypeStruct((1, d), F32)],
        scratch_shapes=[pltpu.VMEM((tm, d), F32)] + [pltpu.VMEM((tm, tf), F32)] * 3,
        compiler_params=pltpu.CompilerParams(dimension_semantics=("arbitrary", "arbitrary"),
                                             vmem_limit_bytes=BIG_VMEM),
    )(h, gain, wgT, wuT, wd, dout)


def pool_fwd(name, h, gain, pw, scale, layer):
    s, d = h.shape
    tm = _tile(s, 512)

    def body(h_ref, g_ref, pw_ref, sc_ref, o_ref, y_ref, ext):
        i = pl.program_id(0)

        @pl.when(i == 0)
        def _():
            ext[0:HALO, :] = jnp.zeros((HALO, d), F32)

        xv = h_ref[...]
        ext[HALO:HALO + tm, :] = xv * _rinv(xv) * g_ref[...]
        pos = i * tm + lax.broadcasted_iota(jnp.int32, (tm, 1), 0)
        for g, w in enumerate(POOL_WINDOWS):
            lo, hi = g * PG, (g + 1) * PG
            ug = ext[HALO:HALO + tm, lo:hi]
            tot = ug
            for k in range(1, w):
                tot = tot + ext[HALO - k:HALO - k + tm, lo:hi]
            cnt = jnp.minimum(pos + 1, w).astype(F32)
            yb = (tot / cnt - ug).astype(BF16)
            y_ref[:, lo:hi] = yb
            o_ref[:, lo:hi] = xv[:, lo:hi] + _dot(yb, pw_ref[g], NN) * sc_ref[:, lo:hi]
        ext[0:HALO, :] = ext[tm:tm + HALO, :]

    row = pl.BlockSpec((tm, d), lambda i: (i, 0))
    vec = pl.BlockSpec((1, d), lambda i: (0, 0))
    return pl.pallas_call(
        body, name=name, grid=(s // tm,),
        in_specs=[row, vec, pl.BlockSpec((None, 4, PG, PG), lambda i: (layer, 0, 0, 0)), vec],
        out_specs=[row, row],
        out_shape=[jax.ShapeDtypeStruct((s, d), F32), jax.ShapeDtypeStruct((s, d), BF16)],
        scratch_shapes=[pltpu.VMEM((HALO + tm, d), F32)],
        compiler_params=_cp("arbitrary"),
    )(h, gain, pw, scale)


def pool_bwd(name, h, gain, y, pw, scale, layer, dout):
    s, d = h.shape
    tm = _tile(s, 512)
    ns = s // tm

    def body(h_ref, g_ref, y_ref, pw_ref, sc_ref, do_ref, dh_ref, dz_ref, dg_ref, dsc_ref, ext, du_sc):
        i = pl.program_id(0)
        t = ns - 1 - i

        @pl.when(i == 0)
        def _():
            ext[tm:tm + HALO, :] = jnp.zeros((HALO, d), F32)
            dg_ref[...] = jnp.zeros((1, d), F32)
            dsc_ref[...] = jnp.zeros((1, d), F32)

        pos = t * tm + lax.broadcasted_iota(jnp.int32, (tm, 1), 0)
        for g, w in enumerate(POOL_WINDOWS):
            lo, hi = g * PG, (g + 1) * PG
            dog = do_ref[:, lo:hi]
            z = _dot(y_ref[:, lo:hi], pw_ref[g], NN)
            dsc_ref[:, lo:hi] += jnp.sum(dog * z, axis=0, keepdims=True)
            dzb = (dog * sc_ref[:, lo:hi]).astype(BF16)
            dz_ref[:, lo:hi] = dzb
            dy = _dot(dzb, pw_ref[g], NT)
            cnt = jnp.minimum(pos + 1, w).astype(F32)
            ext[0:tm, lo:hi] = dy / cnt
            tot = ext[0:tm, lo:hi]
            for k in range(1, w):
                tot = tot + ext[k:k + tm, lo:hi]
            du_sc[:, lo:hi] = tot - dy
        ext[tm:tm + HALO, :] = ext[0:HALO, :]
        dx, dgain = _rms_bwd_math(h_ref[...], g_ref[...], du_sc[...])
        dh_ref[...] = do_ref[...] + dx
        dg_ref[...] += dgain

    row = pl.BlockSpec((tm, d), lambda i: (ns - 1 - i, 0))
    vec = pl.BlockSpec((1, d), lambda i: (0, 0))
    return pl.pallas_call(
        body, name=name, grid=(ns,),
        in_specs=[row, vec, row, pl.BlockSpec((None, 4, PG, PG), lambda i: (layer, 0, 0, 0)), vec, row],
        out_specs=[row, row, vec, vec],
        out_shape=[jax.ShapeDtypeStruct((s, d), F32), jax.ShapeDtypeStruct((s, d), BF16),
                   jax.ShapeDtypeStruct((1, d), F32), jax.ShapeDtypeStruct((1, d), F32)],
        scratch_shapes=[pltpu.VMEM((tm + HALO, d), F32), pltpu.VMEM((tm, d), F32)],
        compiler_params=_cp("arbitrary"),
    )(h, gain, y, pw, scale, dout)


def _rope128(raw_hi, cs):
    b = raw_hi * cs
    r = b + pltpu.roll(b, DR, 1)
    lane = lax.broadcasted_iota(jnp.int32, r.shape, 1)
    return jnp.where(lane < DR, r, 0.0)


def _rope128_bwd(t, cs):
    return (t + pltpu.roll(t, DR, 1)) * cs


def q_heads(name, cq, w256, cs):
    s = cq.shape[0]
    nh = w256.shape[0]
    tm = _tile(s, 1024)

    def body(cq_ref, w_ref, cs_ref, o_ref):
        r = _dot(cq_ref[...], w_ref[...], NN) * Q_PRESCALE
        o_ref[:, 0:DN] = r[:, 0:DN].astype(BF16)
        o_ref[:, DN:HD] = _rope128(r[:, DN:HD], cs_ref[...]).astype(BF16)

    return pl.pallas_call(
        body, name=name, grid=(s // tm, nh),
        in_specs=[pl.BlockSpec((tm, HD), lambda i, hh: (i, 0)),
                  pl.BlockSpec((None, HD, HD), lambda i, hh: (hh, 0, 0)),
                  pl.BlockSpec((tm, DN), lambda i, hh: (i, 0))],
        out_specs=pl.BlockSpec((None, tm, HD), lambda i, hh: (hh, i, 0)),
        out_shape=jax.ShapeDtypeStruct((nh, s, HD), BF16),
        compiler_params=_cp("parallel", "parallel"),
    )(cq, w256, cs)


def kv_post(name, kv_raw, gain, cs):
    s = kv_raw.shape[0]
    tm = _tile(s, 1024)

    def body(x_ref, g_ref, cs_ref, c_ref, kr_ref):
        c = x_ref[:, 0:DN]
        c_ref[...] = (c * _rinv(c) * g_ref[...]).astype(BF16)
        kr_ref[...] = _rope128(x_ref[:, DN:HD], cs_ref[...]).astype(BF16)

    half = pl.BlockSpec((tm, DN), lambda i: (i, 0))
    return pl.pallas_call(
        body, name=name, grid=(s // tm,),
        in_specs=[pl.BlockSpec((tm, HD), lambda i: (i, 0)), pl.BlockSpec((1, DN), lambda i: (0, 0)), half],
        out_specs=[half, half],
        out_shape=[jax.ShapeDtypeStruct((s, DN), BF16), jax.ShapeDtypeStruct((s, DN), BF16)],
        compiler_params=_cp("parallel"),
    )(kv_raw, gain, cs)


def kv_heads(name, c_kv, kr, wuk, wuv):
    s = c_kv.shape[0]
    nh = wuk.shape[0]
    tm = _tile(s, 1024)

    def body(c_ref, kr_ref, wk_ref, wv_ref, k_ref, v_ref):
        c = c_ref[...]
        k_ref[:, 0:DN] = _dot(c, wk_ref[...], NN).astype(BF16)
        k_ref[:, DN:HD] = kr_ref[...]
        v_ref[:, 0:DN] = _dot(c, wv_ref[...], NN).astype(BF16)
        v_ref[:, DN:HD] = jnp.ones((tm, DN), BF16)

    half = pl.BlockSpec((tm, DN), lambda i, hh: (i, 0))
    wsp = pl.BlockSpec((None, DN, DN), lambda i, hh: (hh, 0, 0))
    head = pl.BlockSpec((None, tm, HD), lambda i, hh: (hh, i, 0))
    return pl.pallas_call(
        body, name=name, grid=(s // tm, nh),
        in_specs=[half, half, wsp, wsp], out_specs=[head, head],
        out_shape=[jax.ShapeDtypeStruct((nh, s, HD), BF16), jax.ShapeDtypeStruct((nh, s, HD), BF16)],
        compiler_params=_cp("parallel", "parallel"),
    )(c_kv, kr, wuk, wuv)


def kv_post_bwd(name, kv_raw, gain, dc, dk, cs):
    s = kv_raw.shape[0]
    nh = dk.shape[0]
    tm = _tile(s, 1024)

    def body(x_ref, g_ref, dc_ref, dk_ref, cs_ref, o_ref, dg_ref, acc):
        i, hh = pl.program_id(0), pl.program_id(1)

        @pl.when(hh == 0)
        def _():
            acc[...] = jnp.zeros((tm, DN), F32)

        @pl.when((i == 0) & (hh == 0))
        def _():
            dg_ref[...] = jnp.zeros((1, DN), F32)

        acc[...] += dk_ref[...]

        @pl.when(hh == nh - 1)
        def _():
            dx, dgain = _rms_bwd_math(x_ref[:, 0:DN], g_ref[...], dc_ref[...])
            o_ref[:, 0:DN] = dx.astype(BF16)
            o_ref[:, DN:HD] = _rope128_bwd(acc[...], cs_ref[...]).astype(BF16)
            dg_ref[...] += dgain

    half = pl.BlockSpec((tm, DN), lambda i, hh: (i, 0))
    vec = pl.BlockSpec((1, DN), lambda i, hh: (0, 0))
    full = pl.BlockSpec((tm, HD), lambda i, hh: (i, 0))
    return pl.pallas_call(
        body, name=name, grid=(s // tm, nh),
        in_specs=[full, vec, half, pl.BlockSpec((None, tm, DN), lambda i, hh: (hh, i, 1)), half],
        out_specs=[full, vec],
        out_shape=[jax.ShapeDtypeStruct((s, HD), BF16), jax.ShapeDtypeStruct((1, DN), F32)],
        scratch_shapes=[pltpu.VMEM((tm, DN), F32)],
        compiler_params=_cp("arbitrary", "arbitrary"),
    )(kv_raw, gain, dc, dk, cs)


TQ = 1024
TC = 512
RB = 64


def _flash_cp():
    return pltpu.CompilerParams(dimension_semantics=("parallel", "arbitrary"), vmem_limit_bytes=BIG_VMEM)


def _causal(x, row0, col0, keys_on_rows=False):
    r = row0 + lax.broadcasted_iota(jnp.int32, x.shape, 0)
    c = col0 + lax.broadcasted_iota(jnp.int32, x.shape, 1)
    return jnp.where((r <= c) if keys_on_rows else (c <= r), x, NEG)


def _chunk_pipeline(n_pairs, chunk_of, scores, finish, bufs):
    b0, b1 = bufs
    scores(chunk_of(0), b0)

    def pair(t, carry):
        m = 2 * t
        scores(chunk_of(m + 1), b1)
        finish(chunk_of(m), b0, False)
        scores(chunk_of(m + 2), b0)
        finish(chunk_of(m + 1), b1, False)
        return carry

    lax.fori_loop(0, n_pairs, pair, 0)
    m = 2 * n_pairs
    scores(chunk_of(m + 1), b1)
    finish(chunk_of(m), b0, True)
    finish(chunk_of(m + 1), b1, True)


def flash_fwd(name, q, k, v2):
    nh, s, _ = q.shape
    tq, tk = _tile(s, TQ), _tile(s, TC)
    assert tq == 2 * tk
    nt = tk // LANES

    def body(q_ref, k_ref, v_ref, o_ref, lse_ref, s0_sc, s1_sc, p_sc, m_sc, acc):
        i = pl.program_id(1)
        m_sc[...] = jnp.full((tq, LANES), NEG, F32)
        acc[...] = jnp.zeros((tq, HD), F32)

        def scores(j, dst):
            dst[...] = _dot(q_ref[...], k_ref[pl.ds(pl.multiple_of(j * tk, tk), tk), :], NT)

        def softmax_rows(src, r0, j, masked):
            rows = pl.ds(r0, RB)

            def tile(t):
                x = src[rows, t * LANES:(t + 1) * LANES]
                return _causal(x, i * tq + r0, j * tk + t * LANES) if masked else x

            mx = tile(0)
            for t in range(1, nt):
                mx = jnp.maximum(mx, tile(t))
            m_prev = m_sc[rows, :]
            m_new = jnp.maximum(m_prev, jnp.max(mx, axis=-1, keepdims=True))
            alpha = jnp.exp2(m_prev - m_new)
            for t in range(nt):
                p_sc[rows, t * LANES:(t + 1) * LANES] = jnp.exp2(tile(t) - m_new).astype(BF16)
            m_sc[rows, :] = m_new
            acc[rows, 0:DN] = alpha * acc[rows, 0:DN]
            acc[rows, DN:HD] = alpha * acc[rows, DN:HD]

        def finish(j, src, masked):
            for r in range(tq // RB):
                softmax_rows(src, r * RB, j, masked)
            acc[...] += _dot(p_sc[...], v_ref[pl.ds(pl.multiple_of(j * tk, tk), tk), :], NN)

        _chunk_pipeline(i, lambda m: m, scores, finish, (s0_sc, s1_sc))
        l = acc[:, DN:HD]
        o_ref[...] = (acc[:, 0:DN] / l).astype(BF16)
        lse_ref[...] = m_sc[...] + jnp.log2(l)

    whole = pl.BlockSpec((None, s, HD), lambda hh, i: (hh, 0, 0))
    return pl.pallas_call(
        body, name=name, grid=(nh, s // tq),
        in_specs=[pl.BlockSpec((None, tq, HD), lambda hh, i: (hh, i, 0)), whole, whole],
        out_specs=[pl.BlockSpec((tq, DN), lambda hh, i: (i, hh)),
                   pl.BlockSpec((None, tq, DN), lambda hh, i: (hh, i, 0))],
        out_shape=[jax.ShapeDtypeStruct((s, nh * DN), BF16), jax.ShapeDtypeStruct((nh, s, DN), F32)],
        scratch_shapes=[pltpu.VMEM((tq, tk), F32), pltpu.VMEM((tq, tk), F32), pltpu.VMEM((tq, tk), BF16),
                        pltpu.VMEM((tq, LANES), F32), pltpu.VMEM((tq, HD), F32)],
        compiler_params=_flash_cp(),
    )(q, k, v2)


def attn_do2(name, o, do, nh):
    s = o.shape[0]
    tm = _tile(s, 1024)

    def body(o_ref, do_ref, d_ref):
        dov = do_ref[...]
        delta = jnp.sum(o_ref[...].astype(F32) * dov.astype(F32), axis=-1, keepdims=True)
        hi = delta.astype(BF16).astype(F32)
        lo = delta - hi
        lane = lax.broadcasted_iota(jnp.int32, (tm, DN), 1)
        d_ref[:, 0:DN] = dov
        d_ref[:, DN:HD] = jnp.where(lane == 0, -hi, jnp.where(lane == 1, -lo, 0.0)).astype(BF16)

    blk = pl.BlockSpec((tm, DN), lambda hh, i: (i, hh))
    return pl.pallas_call(
        body, name=name, grid=(nh, s // tm), in_specs=[blk, blk],
        out_specs=pl.BlockSpec((None, tm, HD), lambda hh, i: (hh, i, 0)),
        out_shape=jax.ShapeDtypeStruct((nh, s, HD), BF16), compiler_params=_cp("parallel", "parallel"),
    )(o, do)


def flash_dq(name, q, k, v2, do2, lse, cs):
    nh, s, _ = q.shape
    tq, tk = _tile(s, TQ), _tile(s, TC)
    assert tq == 2 * tk
    nt = tk // LANES

    def body(q_ref, k_ref, v_ref, do_ref, lse_ref, cs_ref, dq_ref, s0_sc, s1_sc, d0_sc, d1_sc, ds_sc, acc):
        i = pl.program_id(1)
        acc[...] = jnp.zeros((tq, HD), F32)

        def scores(j, dst):
            start = pl.multiple_of(j * tk, tk)
            dst[0][...] = _dot(q_ref[...], k_ref[pl.ds(start, tk), :], NT)
            dst[1][...] = _dot(do_ref[...], v_ref[pl.ds(start, tk), :], NT)

        def ds_rows(src, r0, j, masked):
            rows = pl.ds(r0, RB)
            lse_r = lse_ref[rows, :]
            for t in range(nt):
                cols = slice(t * LANES, (t + 1) * LANES)
                x = src[0][rows, cols]
                if masked:
                    x = _causal(x, i * tq + r0, j * tk + t * LANES)
                ds_sc[rows, cols] = (jnp.exp2(x - lse_r) * src[1][rows, cols]).astype(BF16)

        def finish(j, src, masked):
            for r in range(tq // RB):
                ds_rows(src, r * RB, j, masked)
            acc[...] += _dot(ds_sc[...], k_ref[pl.ds(pl.multiple_of(j * tk, tk), tk), :], NN)

        _chunk_pipeline(i, lambda m: m, scores, finish, ((s0_sc, d0_sc), (s1_sc, d1_sc)))
        dq_ref[:, 0:DN] = (acc[:, 0:DN] * ATT_SCALE).astype(BF16)
        dq_ref[:, DN:HD] = _rope128_bwd(acc[:, DN:HD] * ATT_SCALE, cs_ref[...]).astype(BF16)

    tile = pl.BlockSpec((None, tq, HD), lambda hh, i: (hh, i, 0))
    whole = pl.BlockSpec((None, s, HD), lambda hh, i: (hh, 0, 0))
    return pl.pallas_call(
        body, name=name, grid=(nh, s // tq),
        in_specs=[tile, whole, whole, tile, pl.BlockSpec((None, tq, DN), lambda hh, i: (hh, i, 0)),
                  pl.BlockSpec((tq, DN), lambda hh, i: (i, 0))],
        out_specs=tile, out_shape=jax.ShapeDtypeStruct((nh, s, HD), BF16),
        scratch_shapes=[pltpu.VMEM((tq, tk), F32)] * 4 + [pltpu.VMEM((tq, tk), BF16), pltpu.VMEM((tq, HD), F32)],
        compiler_params=_flash_cp(),
    )(q, k, v2, do2, lse, cs)


def flash_dkv(name, q, k, v2, do2, lse_row, prev=None):
    nh, s, _ = q.shape
    tk, tq = _tile(s, TQ), _tile(s, TC)
    assert tk == 2 * tq
    nqc = s // tq
    nt = tq // LANES
    has_prev = prev is not None

    def body(*refs):
        k_ref, v_ref, q_ref, do_ref, lse_ref = refs[:5]
        dk_ref, dv_ref, s0_sc, s1_sc, d0_sc, d1_sc, p_sc, ds_sc, dk_acc, dv_acc = refs[-10:]
        i = pl.program_id(1)
        dk_acc[...] = jnp.zeros((tk, HD), F32)
        dv_acc[...] = jnp.zeros((tk, DN), F32)

        def scores(j, dst):
            start = pl.multiple_of(j * tq, tq)
            dst[0][...] = _dot(k_ref[...], q_ref[pl.ds(start, tq), :], NT)
            dst[1][...] = _dot(v_ref[...], do_ref[pl.ds(start, tq), :], NT)

        def p_rows(src, r0, j, masked):
            rows = pl.ds(r0, RB)
            for t in range(nt):
                cols = slice(t * LANES, (t + 1) * LANES)
                x = src[0][rows, cols]
                if masked:
                    x = _causal(x, i * tk + r0, j * tq + t * LANES, keys_on_rows=True)
                p = jnp.exp2(x - jnp.tile(lse_ref[j][:, cols], (RB // 8, 1)))
                p_sc[rows, cols] = p.astype(BF16)
                ds_sc[rows, cols] = (p * src[1][rows, cols]).astype(BF16)

        def finish(j, src, masked):
            for r in range(tk // RB):
                p_rows(src, r * RB, j, masked)
            start = pl.multiple_of(j * tq, tq)
            dv_acc[...] += _dot(p_sc[...], do_ref[pl.ds(start, tq), 0:DN], NN)
            dk_acc[...] += _dot(ds_sc[...], q_ref[pl.ds(start, tq), :], NN)

        _chunk_pipeline(nqc // 2 - 1 - i, lambda m: nqc - 1 - m, scores, finish, ((s0_sc, d0_sc), (s1_sc, d1_sc)))
        if has_prev:
            dk_ref[...] = dk_acc[...] * LN2 + refs[5][...]
            dv_ref[...] = dv_acc[...] + refs[6][...]
        else:
            dk_ref[...] = dk_acc[...] * LN2
            dv_ref[...] = dv_acc[...]

    kblk = pl.BlockSpec((None, tk, HD), lambda hh, i: (hh, i, 0))
    vblk = pl.BlockSpec((None, tk, DN), lambda hh, i: (hh, i, 0))
    whole = pl.BlockSpec((None, s, HD), lambda hh, i: (hh, 0, 0), pipeline_mode=pl.Buffered(1))
    in_specs = [kblk, kblk, whole, whole, pl.BlockSpec((None, nqc, 8, tq), lambda hh, i: (hh, 0, 0, 0))]
    args = [k, v2, q, do2, lse_row]
    if has_prev:
        in_specs += [kblk, vblk]
        args += list(prev)
    return pl.pallas_call(
        body, name=name, grid=(nh, s // tk), in_specs=in_specs, out_specs=[kblk, vblk],
        out_shape=[jax.ShapeDtypeStruct((nh, s, HD), F32), jax.ShapeDtypeStruct((nh, s, DN), F32)],
        scratch_shapes=[pltpu.VMEM((tk, tq), F32)] * 4 + [pltpu.VMEM((tk, tq), BF16)] * 2
        + [pltpu.VMEM((tk, HD), F32), pltpu.VMEM((tk, DN), F32)],
        compiler_params=_flash_cp(),
    )(*args)


def loss_head(name, h, gain, target):
    s, d = h.shape
    tm = _tile(s, 1024)

    def body(h_ref, g_ref, t_ref, sq_ref, dh_ref, dg_ref):
        @pl.when(pl.program_id(0) == 0)
        def _():
            sq_ref[...] = jnp.zeros((1, d), F32)
            dg_ref[...] = jnp.zeros((1, d), F32)

        xv = h_ref[...]
        g = g_ref[...]
        rinv = _rinv(xv)
        xhat = xv * rinv
        err = xhat * g - t_ref[...]
        sq_ref[...] += jnp.sum(err * err, axis=0, keepdims=True)
        dy = err / d
        dg_ref[...] += jnp.sum(dy * xhat, axis=0, keepdims=True)
        dxh = dy * g
        dh_ref[...] = rinv * (dxh - xhat * jnp.mean(dxh * xhat, axis=-1, keepdims=True))

    row = pl.BlockSpec((tm, d), lambda i: (i, 0))
    vec = pl.BlockSpec((1, d), lambda i: (0, 0))
    return pl.pallas_call(
        body, name=name, grid=(s // tm,), in_specs=[row, vec, row], out_specs=[vec, row, vec],
        out_shape=[jax.ShapeDtypeStruct((1, d), F32), jax.ShapeDtypeStruct((s, d), F32),
                   jax.ShapeDtypeStruct((1, d), F32)],
        compiler_params=_cp("arbitrary"),
    )(h, gain, target)


def _adamw_math(w, g, m, v):
    m2 = ADAM_B1 * m + (1.0 - ADAM_B1) * g
    v2 = ADAM_B2 * v + (1.0 - ADAM_B2) * (g * g)
    mh = m2 / (1.0 - ADAM_B1 ** ADAM_STEP)
    vh = v2 / (1.0 - ADAM_B2 ** ADAM_STEP)
    delta = -ADAM_LR * (mh / (jnp.sqrt(vh) + ADAM_EPS) + ADAM_WD * w)
    return delta, m2, v2


def adamw(name, parts, w, m, v):
    rows, cols = w.shape
    tr = _tile(rows, max(8, (1 << 18) // cols))
    stacked = parts[0].ndim == 3
    npart = len(parts)

    def body(*refs):
        p_refs = refs[:npart]
        w_ref, m_ref, v_ref, g_ref, d_ref, m2_ref, v2_ref = refs[npart:]
        if stacked:
            g = p_refs[0][0]
            for n in range(1, parts[0].shape[0]):
                g = g + p_refs[0][n]
        else:
            g = p_refs[0][...]
            for r in p_refs[1:]:
                g = g + r[...]
        delta, m2, v2 = _adamw_math(w_ref[...], g, m_ref[...], v_ref[...])
        g_ref[...] = g
        d_ref[...] = delta
        m2_ref[...] = m2
        v2_ref[...] = v2

    blk = pl.BlockSpec((tr, cols), lambda i: (i, 0))
    pblk = pl.BlockSpec((parts[0].shape[0], tr, cols), lambda i: (0, i, 0)) if stacked else blk
    sds = jax.ShapeDtypeStruct((rows, cols), F32)
    return pl.pallas_call(
        body, name=name, grid=(rows // tr,), in_specs=[pblk] * npart + [blk] * 3,
        out_specs=[blk] * 4, out_shape=[sds] * 4, compiler_params=_cp("parallel"),
    )(*parts, w, m, v)


def sum4(name, x):
    _, a, r, c = x.shape
    tr = _tile(r, max(8, (1 << 18) // c))

    def body(x_ref, o_ref):
        o_ref[...] = ((x_ref[3].astype(F32) + x_ref[0].astype(F32)) + x_ref[1].astype(F32)) + x_ref[2].astype(F32)

    return pl.pallas_call(
        body, name=name, grid=(a, r // tr),
        in_specs=[pl.BlockSpec((4, None, tr, c), lambda i, j: (0, i, j, 0))],
        out_specs=pl.BlockSpec((None, tr, c), lambda i, j: (i, j, 0)),
        out_shape=jax.ShapeDtypeStruct((a, r, c), F32), compiler_params=_cp("parallel", "parallel"),
    )(x)


ANY = pl.BlockSpec(memory_space=pl.ANY)


def _mesh_pos():
    return lax.axis_index("x"), lax.axis_index("y"), lax.axis_index("c")


def _chip_peers(x, y):
    return [(1 - x, y), (x, 1 - y), (1 - x, 1 - y)]


def chip_all_gather(name, shards):
    n = len(shards)

    def body(*refs):
        ins, outs = refs[:n], refs[n:2 * n]
        send_sems, recv_sems, loc_sems = refs[2 * n:]
        x, y, c = _mesh_pos()
        me = 2 * x + y
        peers = _chip_peers(x, y)

        def remote(a, j, block):
            px, py = peers[j]
            return pltpu.make_async_remote_copy(
                src_ref=ins[a], dst_ref=outs[a].at[:, block], send_sem=send_sems.at[a * 3 + j],
                recv_sem=recv_sems.at[a * 3 + j], device_id=(px, py, c), device_id_type=MESH)

        local = [pltpu.make_async_copy(ins[a], outs[a].at[:, me], loc_sems.at[a]) for a in range(n)]
        for cp in local:
            cp.start()
        sends = [remote(a, j, me) for a in range(n) for j in range(3)]
        for cp in sends:
            cp.start()
        for a in range(n):
            for j, (px, py) in enumerate(peers):
                remote(a, j, 2 * px + py).wait_recv()
        for cp in sends:
            cp.wait_send()
        for cp in local:
            cp.wait()

    return pl.pallas_call(
        body, name=name, in_specs=[ANY] * n, out_specs=[ANY] * n,
        out_shape=[jax.ShapeDtypeStruct((s.shape[0], N_CHIPS) + s.shape[1:], s.dtype) for s in shards],
        scratch_shapes=[pltpu.SemaphoreType.DMA((3 * n,)), pltpu.SemaphoreType.DMA((3 * n,)),
                        pltpu.SemaphoreType.DMA((n,))],
    )(*shards)


def chip_all_to_all(name, fulls):
    n = len(fulls)

    def body(*refs):
        ins, outs = refs[:n], refs[n:2 * n]
        send_sems, recv_sems, loc_sems = refs[2 * n:]
        x, y, c = _mesh_pos()
        me = 2 * x + y
        peers = _chip_peers(x, y)

        def remote(a, j):
            px, py = peers[j]
            return pltpu.make_async_remote_copy(
                src_ref=ins[a].at[:, 2 * px + py], dst_ref=outs[a].at[j], send_sem=send_sems.at[a * 3 + j],
                recv_sem=recv_sems.at[a * 3 + j], device_id=(px, py, c), device_id_type=MESH)

        local = [pltpu.make_async_copy(ins[a].at[:, me], outs[a].at[3], loc_sems.at[a]) for a in range(n)]
        for cp in local:
            cp.start()
        sends = [remote(a, j) for a in range(n) for j in range(3)]
        for cp in sends:
            cp.start()
        for cp in sends:
            cp.wait_recv()
        for cp in sends:
            cp.wait_send()
        for cp in local:
            cp.wait()

    return pl.pallas_call(
        body, name=name, in_specs=[ANY] * n, out_specs=[ANY] * n,
        out_shape=[jax.ShapeDtypeStruct((N_CHIPS, f.shape[0]) + f.shape[2:], f.dtype) for f in fulls],
        scratch_shapes=[pltpu.SemaphoreType.DMA((3 * n,)), pltpu.SemaphoreType.DMA((3 * n,)),
                        pltpu.SemaphoreType.DMA((n,))],
    )(*fulls)


def sibling_exchange(name, arrs):
    n = len(arrs)

    def body(*refs):
        ins, outs = refs[:n], refs[n:2 * n]
        send_sems, recv_sems = refs[2 * n:]
        x, y, c = _mesh_pos()
        copies = [pltpu.make_async_remote_copy(
            src_ref=ins[a], dst_ref=outs[a], send_sem=send_sems.at[a], recv_sem=recv_sems.at[a],
            device_id=(x, y, 1 - c), device_id_type=MESH) for a in range(n)]
        for cp in copies:
            cp.start()
        for cp in copies:
            cp.wait_recv()
        for cp in copies:
            cp.wait_send()

    return pl.pallas_call(
        body, name=name, in_specs=[ANY] * n, out_specs=[ANY] * n,
        out_shape=[jax.ShapeDtypeStruct(a.shape, a.dtype) for a in arrs],
        scratch_shapes=[pltpu.SemaphoreType.DMA((n,)), pltpu.SemaphoreType.DMA((n,))],
    )(*arrs)


def all_gather8(name, buf):
    r, cdim = buf.shape

    def body(in_ref, out_ref, send_sems, recv_sems, loc_sem):
        x, y, c = _mesh_pos()
        flips = [(fx, fy, fc) for fx in (0, 1) for fy in (0, 1) for fc in (0, 1)][1:]

        def peer(f):
            return tuple((1 - p) if b else p for p, b in zip((x, y, c), f))

        def remote(j, slot):
            return pltpu.make_async_remote_copy(
                src_ref=in_ref, dst_ref=out_ref.at[slot], send_sem=send_sems.at[j], recv_sem=recv_sems.at[j],
                device_id=peer(flips[j]), device_id_type=MESH)

        me = 4 * x + 2 * y + c
        local = pltpu.make_async_copy(in_ref, out_ref.at[me], loc_sem)
        local.start()
        sends = [remote(j, me) for j in range(7)]
        for cp in sends:
            cp.start()
        for j in range(7):
            px, py, pc = peer(flips[j])
            remote(j, 4 * px + 2 * py + pc).wait_recv()
        for cp in sends:
            cp.wait_send()
        local.wait()

    return pl.pallas_call(
        body, name=name, in_specs=[ANY], out_specs=ANY,
        out_shape=jax.ShapeDtypeStruct((8, r, cdim), buf.dtype),
        scratch_shapes=[pltpu.SemaphoreType.DMA((7,)), pltpu.SemaphoreType.DMA((7,)), pltpu.SemaphoreType.DMA],
    )(buf)


def _rot_cols(w):
    half = w.shape[-1] // 2
    return jnp.concatenate([-w[..., half:], w[..., :half]], axis=-1)


def _fold_rot(g):
    rot = g[..., DN + DR:]
    half = DR // 2
    return jnp.concatenate([g[..., :DN], g[..., DN:DN + DR] + jnp.concatenate([rot[..., half:], -rot[..., :half]], -1)], -1)


def _with_rot(w):
    return jnp.concatenate([w, _rot_cols(w[..., DN:])], axis=-1)


def _rope_table(s):
    pos = jnp.arange(s, dtype=F32)
    inv_freq = ROPE_THETA ** (-jnp.arange(0, DR, 2, dtype=F32) / DR)
    ang = pos[:, None] * inv_freq[None, :]
    cos, sin = jnp.cos(ang), jnp.sin(ang)
    return jnp.concatenate([cos, cos, sin, sin], axis=-1)


def _row_stats(x, tq):
    nh, s, _ = x.shape
    return jnp.broadcast_to(x[:, :, 0].reshape(nh, s // tq, 1, tq), (nh, s // tq, 8, tq))


REPL_NAMES = ["ffn_pre_norm", "mix_norm", "ffn_post_norm", "kv_in_norm", "final_norm", "ckv_norm",
              "q_lora_norm", "pool_scale_full", "w_uk", "w_uv"]
PACK_COLS = 1024


def _pack_rows(arrs):
    rows, counts = [], []
    for a in arrs:
        flat = a.reshape(-1).astype(F32)
        n = -(-flat.shape[0] // PACK_COLS)
        flat = jnp.pad(flat, (0, n * PACK_COLS - flat.shape[0]))
        rows.append(flat.reshape(n, PACK_COLS))
        counts.append(n)
    total = sum(counts)
    pad = -total % 8
    if pad:
        rows.append(jnp.zeros((pad, PACK_COLS), F32))
    return jnp.concatenate(rows, axis=0), counts


def _unpack_rows(buf, counts, shapes):
    out, r = [], 0
    for n, shp in zip(counts, shapes):
        size = 1
        for dd in shp:
            size *= dd
        out.append(buf[r:r + n].reshape(-1)[:size].reshape(shp))
        r += n
    return out


def kernel(x, ffn_pre_norm, ffn_pre_wg, ffn_pre_wu, ffn_pre_wd, mix_norm, ffn_post_norm, ffn_post_wg, ffn_post_wu, ffn_post_wd, pool_w, pool_scale, kv_in_norm, w_dkv, ckv_norm, w_uk, w_uv, q_lora_norm, w_dq, w_uq, w_o, final_norm, loss_target, m_ffn_pre_norm, m_ffn_pre_wg, m_ffn_pre_wu, m_ffn_pre_wd, m_mix_norm, m_ffn_post_norm, m_ffn_post_wg, m_ffn_post_wu, m_ffn_post_wd, m_pool_w, m_pool_scale, m_kv_in_norm, m_w_dkv, m_ckv_norm, m_w_uk, m_w_uv, m_q_lora_norm, m_w_dq, m_w_uq, m_w_o, m_final_norm, v_ffn_pre_norm, v_ffn_pre_wg, v_ffn_pre_wu, v_ffn_pre_wd, v_mix_norm, v_ffn_post_norm, v_ffn_post_wg, v_ffn_post_wu, v_ffn_post_wd, v_pool_w, v_pool_scale, v_kv_in_norm, v_w_dkv, v_ckv_norm, v_w_uk, v_w_uv, v_q_lora_norm, v_w_dq, v_w_uq, v_w_o, v_final_norm):
    weights = dict(ffn_pre_norm=ffn_pre_norm, ffn_pre_wg=ffn_pre_wg, ffn_pre_wu=ffn_pre_wu, ffn_pre_wd=ffn_pre_wd,
                   mix_norm=mix_norm, ffn_post_norm=ffn_post_norm, ffn_post_wg=ffn_post_wg, ffn_post_wu=ffn_post_wu,
                   ffn_post_wd=ffn_post_wd, pool_w=pool_w, pool_scale=pool_scale, kv_in_norm=kv_in_norm, w_dkv=w_dkv,
                   ckv_norm=ckv_norm, w_uk=w_uk, w_uv=w_uv, q_lora_norm=q_lora_norm, w_dq=w_dq, w_uq=w_uq, w_o=w_o,
                   final_norm=final_norm)
    mom1 = dict(ffn_pre_norm=m_ffn_pre_norm, ffn_pre_wg=m_ffn_pre_wg, ffn_pre_wu=m_ffn_pre_wu, ffn_pre_wd=m_ffn_pre_wd,
                mix_norm=m_mix_norm, ffn_post_norm=m_ffn_post_norm, ffn_post_wg=m_ffn_post_wg, ffn_post_wu=m_ffn_post_wu,
                ffn_post_wd=m_ffn_post_wd, pool_w=m_pool_w, pool_scale=m_pool_scale, kv_in_norm=m_kv_in_norm,
                w_dkv=m_w_dkv, ckv_norm=m_ckv_norm, w_uk=m_w_uk, w_uv=m_w_uv, q_lora_norm=m_q_lora_norm, w_dq=m_w_dq,
                w_uq=m_w_uq, w_o=m_w_o, final_norm=m_final_norm)
    mom2 = dict(ffn_pre_norm=v_ffn_pre_norm, ffn_pre_wg=v_ffn_pre_wg, ffn_pre_wu=v_ffn_pre_wu, ffn_pre_wd=v_ffn_pre_wd,
                mix_norm=v_mix_norm, ffn_post_norm=v_ffn_post_norm, ffn_post_wg=v_ffn_post_wg, ffn_post_wu=v_ffn_post_wu,
                ffn_post_wd=v_ffn_post_wd, pool_w=v_pool_w, pool_scale=v_pool_scale, kv_in_norm=v_kv_in_norm,
                w_dkv=v_w_dkv, ckv_norm=v_ckv_norm, w_uk=v_w_uk, w_uv=v_w_uv, q_lora_norm=v_q_lora_norm, w_dq=v_w_dq,
                w_uq=v_w_uq, w_o=v_w_o, final_norm=v_final_norm)
    names = list(weights)

    h0 = x[0]
    target = loss_target[0]
    s, d = h0.shape
    nh = N_HEADS
    nb = DEPTH - N_A
    f_loc = ffn_pre_wd.shape[1]
    ffn_dim = N_CHIPS * f_loc
    my_chip = 2 * lax.axis_index("x") + lax.axis_index("y")

    hidden_major = lambda w: jnp.transpose(w, (0, 2, 1)).astype(BF16)
    uq_loc = jnp.transpose(w_uq, (0, 2, 1, 3))
    shards = [
        hidden_major(ffn_pre_wg), hidden_major(ffn_pre_wu), ffn_pre_wd.astype(BF16),
        hidden_major(ffn_post_wg), hidden_major(ffn_post_wu), ffn_post_wd.astype(BF16),
        pool_w.reshape(N_A * 4, PG // N_CHIPS, PG).astype(BF16),
        _with_rot(w_dkv)[None].astype(BF16),
        w_dq.astype(BF16),
        _with_rot(uq_loc).reshape(nb * nh, uq_loc.shape[2], HD).astype(BF16),
        w_o.astype(BF16),
        pool_scale.reshape(N_A, 1, PG),
    ]
    gathered = chip_all_gather("gather_weights", shards)
    merge = lambda g: g.reshape(g.shape[0], N_CHIPS * g.shape[2], g.shape[3])
    pre_wgT, pre_wuT, pre_wd, post_wgT, post_wuT, post_wd = [merge(g) for g in gathered[:6]]
    pool_w_full = merge(gathered[6]).reshape(N_A, 4, PG, PG)
    wdkv256 = merge(gathered[7])[0]
    wdq_full = merge(gathered[8])
    wq256 = merge(gathered[9]).reshape(nb, nh, HD, HD)
    wo_full = merge(gathered[10])
    pool_scale_full = gathered[11].reshape(N_A, d)
    wuk_h = jnp.transpose(w_uk, (1, 0, 2)).astype(BF16)
    wuv_h = jnp.transpose(w_uv, (1, 0, 2)).astype(BF16)
    cs = _rope_table(s)

    ffn_w = {"pre": (ffn_pre_norm, pre_wgT, pre_wuT, pre_wd), "post": (ffn_post_norm, post_wgT, post_wuT, post_wd)}

    saved = {}
    h = h0
    kv = None
    for l in range(DEPTH):
        saved["a", l] = h
        g_, wg_, wu_, wd_ = ffn_w["pre"]
        h = ffn_fwd(f"ffn_pre_fwd{l}", h, g_[l:l + 1], wg_, wu_, wd_, l)
        saved["b", l] = h
        if l < N_A:
            h, y = pool_fwd(f"pool_fwd{l}", h, mix_norm[l:l + 1], pool_w_full, pool_scale_full[l:l + 1], l)
            saved["y", l] = y
        else:
            j = l - N_A
            u = rms_fwd(f"mix_norm_fwd{l}", h, mix_norm[l:l + 1])
            cq_raw = mm_nn(f"q_down{l}", u, wdq_full[j], F32, 2048, 256, 1024)
            cq = rms_fwd(f"q_norm_fwd{l}", cq_raw, q_lora_norm[j:j + 1])
            q = q_heads(f"q_heads{l}", cq, wq256[j], cs)
            o, lse = flash_fwd(f"flash_fwd{l}", q, kv[0], kv[1])
            saved["att", l] = (u, cq_raw, cq, q, o, lse)
            h = mm_nn(f"attn_out{l}", o, wo_full[j], F32, 1024, 1024, 1024, res=h)
        saved["c", l] = h
        g_, wg_, wu_, wd_ = ffn_w["post"]
        h = ffn_fwd(f"ffn_post_fwd{l}", h, g_[l:l + 1], wg_, wu_, wd_, l)
        if l == N_A - 1:
            u_kv = rms_fwd("kv_in_norm_fwd", h, kv_in_norm[None])
            kv_raw = mm_nn("kv_down", u_kv, wdkv256, F32, 2048, 256, 1024)
            c_kv, kr = kv_post("kv_post", kv_raw, ckv_norm[None], cs)
            kv = kv_heads("kv_heads", c_kv, kr, wuk_h, wuv_h)
            saved["kv"] = (u_kv, kv_raw, c_kv)

    sq, dh, g_final = loss_head("loss_head", h, final_norm[None], target)
    loss = lax.psum(0.5 * jnp.sum(sq) / d, ("x", "y", "c"))

    grads = {}
    gvec = {n: [None] * DEPTH for n in ("ffn_pre_norm", "mix_norm", "ffn_post_norm")}
    gffn = {(p, m): [None] * DEPTH for p in ("pre", "post") for m in ("wg", "wu", "wd")}
    g_pool_w, g_pool_scale = [None] * N_A, [None] * N_A
    g_qnorm, g_wdq, g_wq256, g_wo = [None] * nb, [None] * nb, [None] * nb, [None] * nb
    dkv = None

    def ffn_backward(part, l, hin, dh):
        g_, wg_, wu_, wd_ = ffn_w[part]
        dh, u, dob, act, da, db, dgain = ffn_bwd(f"ffn_{part}_bwd{l}", hin, g_[l:l + 1], wg_, wu_, wd_, l, dh)
        gvec[f"ffn_{part}_norm"][l] = dgain[0]
        gffn[part, "wg"][l] = mm_tn(f"ffn_{part}_dwg{l}", da, u, BF16, ffn_dim // 2, d, 1024)
        gffn[part, "wu"][l] = mm_tn(f"ffn_{part}_dwu{l}", db, u, BF16, ffn_dim // 2, d, 1024)
        gffn[part, "wd"][l] = mm_tn(f"ffn_{part}_dwd{l}", act, dob, BF16, ffn_dim // 2, d, 1024)
        return dh

    for l in reversed(range(DEPTH)):
        if l == N_A - 1:
            u_kv, kv_raw, c_kv = saved["kv"]
            dk, dv = dkv
            tm = _tile(s, 2048)
            head_red = dict(grid=(s // tm, 1, nh), b_block=(None, DN, DN), b_map=lambda i, jj, kk: (kk, 0, 0),
                            o_block=(tm, DN), o_map=lambda i, jj, kk: (i, 0), out_shape=(s, DN), dims=NT, out_dtype=F32)
            dc = _mm("kv_dc_k", dk, wuk_h, a_block=(None, tm, DN), a_map=lambda i, jj, kk: (kk, i, 0), **head_red)
            dc = _mm("kv_dc_v", dv, wuv_h, a_block=(None, tm, DN), a_map=lambda i, jj, kk: (kk, i, 0), res=dc, **head_red)
            tk = _tile(s, 2048)
            head_tn = dict(grid=(nh, 1, s // tk), a_block=(tk, DN), a_map=lambda hh, jj, kk: (kk, 0),
                           b_block=(None, tk, DN), b_map=lambda hh, jj, kk: (hh, kk, 0),
                           o_block=(None, DN, DN), o_map=lambda hh, jj, kk: (hh, 0, 0),
                           out_shape=(nh, DN, DN), dims=TN, out_dtype=F32)
            grads["w_uk"] = jnp.transpose(_mm("kv_dwuk", c_kv, dk, **head_tn), (1, 0, 2))
            grads["w_uv"] = jnp.transpose(_mm("kv_dwuv", c_kv, dv, **head_tn), (1, 0, 2))
            dkv_raw, dg_ckv = kv_post_bwd("kv_post_bwd", kv_raw, ckv_norm[None], dc, dk, cs)
            grads["ckv_norm"] = dg_ckv[0]
            du_kv = mm_nt("kv_du", dkv_raw, wdkv256, F32, 2048, 1024, 256)
            g_wdkv256 = mm_tn("kv_dwdkv", u_kv, dkv_raw, BF16, 1024, 256, 2048)
            dh, dg_kvin = rms_bwd("kv_in_norm_bwd", saved["a", l + 1], kv_in_norm[None], du_kv, dres=dh)
            grads["kv_in_norm"] = dg_kvin[0]

        dh = ffn_backward("post", l, saved["c", l], dh)

        if l < N_A:
            dh, dz, dgain, dscale = pool_bwd(f"pool_bwd{l}", saved["b", l], mix_norm[l:l + 1], saved["y", l],
                                             pool_w_full, pool_scale_full[l:l + 1], l, dh)
            gvec["mix_norm"][l] = dgain[0]
            g_pool_scale[l] = dscale[0]
            tk = _tile(s, 2048)
            g_pool_w[l] = _mm(f"pool_dw{l}", saved["y", l], dz, grid=(4, 1, s // tk),
                              a_block=(tk, PG), a_map=lambda g, jj, kk: (kk, g),
                              b_block=(tk, PG), b_map=lambda g, jj, kk: (kk, g),
                              o_block=(None, PG, PG), o_map=lambda g, jj, kk: (g, 0, 0),
                              out_shape=(4, PG, PG), dims=TN, out_dtype=BF16)
        else:
            j = l - N_A
            u, cq_raw, cq, q, o, lse = saved["att", l]
            k_, v_ = kv
            do = mm_nt(f"attn_do{l}", dh, wo_full[j], BF16, 1024, 1024, 1024)
            g_wo[j] = mm_tn(f"attn_dwo{l}", o, dh, BF16, 1024, 1024, 1024)
            do2 = attn_do2(f"attn_do2_{l}", o, do, nh)
            dq_raw = flash_dq(f"flash_dq{l}", q, k_, v_, do2, lse, cs)
            dkv = flash_dkv(f"flash_dkv{l}", q, k_, v_, do2, _row_stats(lse, _tile(s, TC)), prev=dkv)
            tm = _tile(s, 2048)
            dcq = _mm(f"q_dcq{l}", dq_raw, wq256[j], grid=(s // tm, 1, nh),
                      a_block=(None, tm, HD), a_map=lambda i, jj, kk: (kk, i, 0),
                      b_block=(None, HD, HD), b_map=lambda i, jj, kk: (kk, 0, 0),
                      o_block=(tm, HD), o_map=lambda i, jj, kk: (i, 0),
                      out_shape=(s, HD), dims=NT, out_dtype=F32)
            tk = _tile(s, 2048)
            g_wq256[j] = _mm(f"q_dwq{l}", cq, dq_raw, grid=(nh, 1, s // tk),
                             a_block=(tk, HD), a_map=lambda hh, jj, kk: (kk, 0),
                             b_block=(None, tk, HD), b_map=lambda hh, jj, kk: (hh, kk, 0),
                             o_block=(None, HD, HD), o_map=lambda hh, jj, kk: (hh, 0, 0),
                             out_shape=(nh, HD, HD), dims=TN, out_dtype=BF16)
            dcq_raw, dg_q = rms_bwd(f"q_norm_bwd{l}", cq_raw, q_lora_norm[j:j + 1], dcq)
            g_qnorm[j] = dg_q[0]
            du = mm_nt(f"q_du{l}", dcq_raw, wdq_full[j], F32, 2048, 1024, 256)
            g_wdq[j] = mm_tn(f"q_dwdq{l}", u, dcq_raw, BF16, 1024, 256, 2048)
            dh, dgain = rms_bwd(f"mix_norm_bwd{l}", saved["b", l], mix_norm[l:l + 1], du, dres=dh)
            gvec["mix_norm"][l] = dgain[0]

        dh = ffn_backward("pre", l, saved["a", l], dh)

    grad_x = dh[None]

    split = lambda g: g.reshape(g.shape[0], N_CHIPS, g.shape[1] // N_CHIPS, g.shape[2])
    fulls = [split(jnp.stack(gffn[p, m])) for p in ("pre", "post") for m in ("wg", "wu", "wd")]
    fulls += [
        split(jnp.stack(g_pool_w).reshape(N_A * 4, PG, PG)),
        split(g_wdkv256[None]),
        split(jnp.stack(g_wdq)),
        split(jnp.stack(g_wq256).reshape(nb * nh, HD, HD)),
        split(jnp.stack(g_wo)),
    ]
    sharded_names = ["ffn_pre_wg", "ffn_pre_wu", "ffn_pre_wd", "ffn_post_wg", "ffn_post_wu", "ffn_post_wd",
                     "pool_w", "w_dkv", "w_dq", "w_uq", "w_o"]
    received = chip_all_to_all("exchange_grads", fulls)
    partial = [sum4(f"chip_sum_{n}", r) for n, r in zip(sharded_names, received)]
    sibling = sibling_exchange("exchange_sibling", partial)

    def natural(name, p):
        if name in ("ffn_pre_wg", "ffn_pre_wu", "ffn_post_wg", "ffn_post_wu"):
            return jnp.transpose(p, (0, 2, 1))
        if name == "w_dkv":
            return _fold_rot(p[0])
        if name == "w_uq":
            return jnp.transpose(_fold_rot(p).reshape(nb, nh, p.shape[1], QK_DIM), (0, 2, 1, 3))
        return p.reshape(weights[name].shape)

    results = {}
    for n, p, q_ in zip(sharded_names, partial, sibling):
        shp = weights[n].shape
        as2d = lambda a: a.reshape(-1, shp[-1])
        outs = adamw(f"adamw_{n}", [as2d(natural(n, p)), as2d(natural(n, q_))],
                     as2d(weights[n]), as2d(mom1[n]), as2d(mom2[n]))
        results[n] = [o_.reshape(shp) for o_ in outs]

    grads.update(ffn_pre_norm=jnp.stack(gvec["ffn_pre_norm"]), mix_norm=jnp.stack(gvec["mix_norm"]),
                 ffn_post_norm=jnp.stack(gvec["ffn_post_norm"]), final_norm=g_final[0],
                 q_lora_norm=jnp.stack(g_qnorm), pool_scale_full=jnp.stack(g_pool_scale))
    repl_shapes = [grads[n].shape for n in REPL_NAMES]
    packed_g, counts = _pack_rows([grads[n] for n in REPL_NAMES])
    all_g = all_gather8("gather_small_grads", packed_g)
    zeros_ps = jnp.zeros((N_A, d), F32)
    pick = lambda src: [zeros_ps if n == "pool_scale_full" else src[n] for n in REPL_NAMES]
    packed = [_pack_rows(pick(src))[0] for src in (weights, mom1, mom2)]
    outs = adamw("adamw_replicated", [all_g], *packed)
    unpacked = [_unpack_rows(o_, counts, repl_shapes) for o_ in outs]
    for idx, n in enumerate(REPL_NAMES):
        results[n] = [u_[idx] for u_ in unpacked]
    g_ps = lax.dynamic_index_in_dim(results["pool_scale_full"][0].reshape(N_A, N_CHIPS, PG), my_chip, axis=1, keepdims=False)
    results["pool_scale"] = adamw("adamw_pool_scale", [g_ps], pool_scale, m_pool_scale, v_pool_scale)

    out = [loss, grad_x]
    for part in range(4):
        out += [results[n][part] for n in names]
    return tuple(out)
```

```python
import functools

import jax
import jax.numpy as jnp
from jax import lax
from jax.experimental import pallas as pl
from jax.experimental.pallas import tpu as pltpu

F32 = jnp.float32
BF16 = jnp.bfloat16
MESH = pl.DeviceIdType.MESH

DEPTH = 4
N_A = 2
N_HEADS = 16
DN = 128
DR = 64
HD = 256
QK_DIM = DN + DR
POOL_WINDOWS = (2, 4, 8, 16)
PG = 256
HALO = 16
N_CHIPS = 4
RMS_EPS = 1e-6
ROPE_THETA = 10000.0
ATT_SCALE = QK_DIM ** -0.5
LOG2E = 1.4426950408889634
LN2 = 0.6931471805599453
Q_PRESCALE = ATT_SCALE * LOG2E
NEG = -1e30
LANES = 128

ADAM_LR = 0.001
ADAM_B1 = 0.9
ADAM_B2 = 0.999
ADAM_EPS = 1e-08
ADAM_WD = 0.01
ADAM_STEP = 10

VMEM_BYTES_V7X = 64 * 1024 * 1024
VMEM_LIMIT = VMEM_BYTES_V7X * 3 // 4
BIG_VMEM = VMEM_BYTES_V7X * 7 // 8
FFN_RB = 64

NN = ((1,), (0,))
NT = ((1,), (1,))
TN = ((0,), (0,))


def _cp(*sem):
    return pltpu.CompilerParams(dimension_semantics=sem, vmem_limit_bytes=VMEM_LIMIT)


def _dot(a, b, dims):
    return lax.dot_general(a, b, (dims, ((), ())), preferred_element_type=F32)


def _tile(n, pref):
    if n <= pref:
        return n
    if n % pref == 0:
        return pref
    t = 1 << (pref.bit_length() - 1)
    while n % t:
        t //= 2
    return t


def _rinv(x):
    return lax.rsqrt(jnp.mean(x * x, axis=-1, keepdims=True) + RMS_EPS)


def _sigmoid(a):
    return 1.0 / (1.0 + jnp.exp(-a))


def _mm(name, a, b, *, grid, a_block, a_map, b_block, b_map, o_block, o_map, out_shape, dims,
        out_dtype, res=None):
    nax = len(grid)
    nk = grid[-1]
    acc_shape = tuple(d for d in o_block if d is not None)
    has_res = res is not None

    def body(*refs):
        a_ref, b_ref = refs[0], refs[1]
        o_ref, acc = refs[-2], refs[-1]
        k = pl.program_id(nax - 1)

        @pl.when(k == 0)
        def _():
            acc[...] = jnp.zeros(acc_shape, F32)

        acc[...] += _dot(a_ref[...].astype(BF16), b_ref[...].astype(BF16), dims)

        @pl.when(k == nk - 1)
        def _():
            r = acc[...]
            if has_res:
                r = r + refs[2][...].astype(F32)
            o_ref[...] = r.astype(out_dtype)

    in_specs = [pl.BlockSpec(a_block, a_map), pl.BlockSpec(b_block, b_map)]
    args = [a, b]
    if has_res:
        in_specs.append(pl.BlockSpec(o_block, o_map))
        args.append(res)
    return pl.pallas_call(
        body, name=name, grid=grid, in_specs=in_specs, out_specs=pl.BlockSpec(o_block, o_map),
        out_shape=jax.ShapeDtypeStruct(out_shape, out_dtype),
        scratch_shapes=[pltpu.VMEM(acc_shape, F32)],
        compiler_params=_cp(*(("parallel",) * (nax - 1) + ("arbitrary",))),
    )(*args)


def mm_nn(name, a, b, out_dtype, tm, tn, tk, res=None):
    (m, kd), n = a.shape, b.shape[1]
    tm, tn, tk = _tile(m, tm), _tile(n, tn), _tile(kd, tk)
    return _mm(name, a, b, grid=(m // tm, n // tn, kd // tk),
               a_block=(tm, tk), a_map=lambda i, j, k: (i, k),
               b_block=(tk, tn), b_map=lambda i, j, k: (k, j),
               o_block=(tm, tn), o_map=lambda i, j, k: (i, j),
               out_shape=(m, n), dims=NN, out_dtype=out_dtype, res=res)


def mm_nt(name, a, b, out_dtype, tm, tn, tk):
    (m, kd), n = a.shape, b.shape[0]
    tm, tn, tk = _tile(m, tm), _tile(n, tn), _tile(kd, tk)
    return _mm(name, a, b, grid=(m // tm, n // tn, kd // tk),
               a_block=(tm, tk), a_map=lambda i, j, k: (i, k),
               b_block=(tn, tk), b_map=lambda i, j, k: (j, k),
               o_block=(tm, tn), o_map=lambda i, j, k: (i, j),
               out_shape=(m, n), dims=NT, out_dtype=out_dtype)


def mm_tn(name, a, b, out_dtype, tm, tn, tk):
    (kd, m), n = a.shape, b.shape[1]
    tm, tn, tk = _tile(m, tm), _tile(n, tn), _tile(kd, tk)
    return _mm(name, a, b, grid=(m // tm, n // tn, kd // tk),
               a_block=(tk, tm), a_map=lambda i, j, k: (k, i),
               b_block=(tk, tn), b_map=lambda i, j, k: (k, j),
               o_block=(tm, tn), o_map=lambda i, j, k: (i, j),
               out_shape=(m, n), dims=TN, out_dtype=out_dtype)


def rms_fwd(name, x, gain):
    s, d = x.shape
    tm = _tile(s, 1024)

    def body(x_ref, g_ref, o_ref):
        xv = x_ref[...]
        o_ref[...] = (xv * _rinv(xv) * g_ref[...]).astype(BF16)

    return pl.pallas_call(
        body, name=name, grid=(s // tm,),
        in_specs=[pl.BlockSpec((tm, d), lambda i: (i, 0)), pl.BlockSpec((1, d), lambda i: (0, 0))],
        out_specs=pl.BlockSpec((tm, d), lambda i: (i, 0)),
        out_shape=jax.ShapeDtypeStruct((s, d), BF16), compiler_params=_cp("parallel"),
    )(x, gain)


def _rms_bwd_math(xv, g, du):
    rinv = _rinv(xv)
    xhat = xv * rinv
    dgain = jnp.sum(du * xhat, axis=0, keepdims=True)
    dxh = du * g
    dx = rinv * (dxh - xhat * jnp.mean(dxh * xhat, axis=-1, keepdims=True))
    return dx, dgain


def rms_bwd(name, x, gain, du, dres=None):
    s, d = x.shape
    tm = _tile(s, 1024)
    has_res = dres is not None

    def body(*refs):
        x_ref, g_ref, du_ref = refs[:3]
        dx_ref, dg_ref = refs[-2:]

        @pl.when(pl.program_id(0) == 0)
        def _():
            dg_ref[...] = jnp.zeros((1, d), F32)

        dx, dgain = _rms_bwd_math(x_ref[...], g_ref[...], du_ref[...].astype(F32))
        if has_res:
            dx = dx + refs[3][...]
        dx_ref[...] = dx
        dg_ref[...] += dgain

    row = pl.BlockSpec((tm, d), lambda i: (i, 0))
    vec = pl.BlockSpec((1, d), lambda i: (0, 0))
    args = [x, gain, du] + ([dres] if has_res else [])
    return pl.pallas_call(
        body, name=name, grid=(s // tm,),
        in_specs=[row, vec, row] + ([row] if has_res else []),
        out_specs=[row, vec],
        out_shape=[jax.ShapeDtypeStruct((s, d), F32), jax.ShapeDtypeStruct((1, d), F32)],
        compiler_params=_cp("arbitrary"),
    )(*args)


def ffn_fwd(name, h, gain, wgT, wuT, wd, layer):
    s, d = h.shape
    f = wd.shape[1]
    tm, tf = _tile(s, 1024), _tile(f, 256)
    nf = f // tf

    def body(h_ref, g_ref, wg_ref, wu_ref, wd_ref, o_ref, u_sc, acc):
        j = pl.program_id(1)

        @pl.when(j == 0)
        def _():
            xv = h_ref[...]
            u_sc[...] = (xv * _rinv(xv) * g_ref[...]).astype(BF16)
            acc[...] = jnp.zeros((tm, d), F32)

        u = u_sc[...]
        a = _dot(u, wg_ref[...], NT)
        b = _dot(u, wu_ref[...], NT)
        act = (a * _sigmoid(a) * b).astype(BF16)
        acc[...] += _dot(act, wd_ref[...], NN)

        @pl.when(j == nf - 1)
        def _():
            o_ref[...] = h_ref[...] + 0.5 * acc[...]

    row = pl.BlockSpec((tm, d), lambda i, j: (i, 0))
    wsp = pl.BlockSpec((None, tf, d), lambda i, j: (layer, j, 0))
    return pl.pallas_call(
        body, name=name, grid=(s // tm, nf),
        in_specs=[row, pl.BlockSpec((1, d), lambda i, j: (0, 0)), wsp, wsp, wsp],
        out_specs=row, out_shape=jax.ShapeDtypeStruct((s, d), F32),
        scratch_shapes=[pltpu.VMEM((tm, d), BF16), pltpu.VMEM((tm, d), F32)],
        compiler_params=_cp("parallel", "arbitrary"),
    )(h, gain, wgT, wuT, wd)


def ffn_bwd(name, h, gain, wgT, wuT, wd, layer, dout):
    s, d = h.shape
    f = wd.shape[1]
    tm, tf = _tile(s, 1024), _tile(f, 256)
    nf = f // tf
    halves = 2 if tm % (2 * FFN_RB) == 0 else 1
    hm = tm // halves

    def body(h_ref, g_ref, wg_ref, wu_ref, wd_ref, do_ref,
             dh_ref, u_ref, dob_ref, act_ref, da_ref, db_ref, dg_ref, du_acc, a_sc, b_sc, c_sc):
        i, j = pl.program_id(0), pl.program_id(1)

        @pl.when(j == 0)
        def _():
            xv = h_ref[...]
            u_ref[...] = (xv * _rinv(xv) * g_ref[...]).astype(BF16)
            dob_ref[...] = (0.5 * do_ref[...]).astype(BF16)
            du_acc[...] = jnp.zeros((tm, d), F32)

        @pl.when((i == 0) & (j == 0))
        def _():
            dg_ref[...] = jnp.zeros((1, d), F32)

        wg, wu = wg_ref[...], wu_ref[...]
        for part in range(halves):
            prow = pl.ds(part * hm, hm)
            u = u_ref[prow, :]
            a_sc[prow, :] = _dot(u, wg, NT)
            b_sc[prow, :] = _dot(u, wu, NT)
            c_sc[prow, :] = _dot(dob_ref[prow, :], wd_ref[...], NT)
            for r in range(hm // FFN_RB):
                rows = pl.ds(part * hm + r * FFN_RB, FFN_RB)
                for t in range(tf // LANES):
                    cols = slice(t * LANES, (t + 1) * LANES)
                    a, b, dact = a_sc[rows, cols], b_sc[rows, cols], c_sc[rows, cols]
                    sig = _sigmoid(a)
                    sa = a * sig
                    act_ref[rows, cols] = (sa * b).astype(BF16)
                    db_ref[rows, cols] = (dact * sa).astype(BF16)
                    da_ref[rows, cols] = (dact * b * (sig * (1.0 + a * (1.0 - sig)))).astype(BF16)
            du_acc[prow, :] += _dot(da_ref[prow, :], wg, NN) + _dot(db_ref[prow, :], wu, NN)

        @pl.when(j == nf - 1)
        def _():
            dx, dgain = _rms_bwd_math(h_ref[...], g_ref[...], du_acc[...])
            dh_ref[...] = do_ref[...] + dx
            dg_ref[...] += dgain

    row = pl.BlockSpec((tm, d), lambda i, j: (i, 0))
    vec = pl.BlockSpec((1, d), lambda i, j: (0, 0))
    wsp = pl.BlockSpec((None, tf, d), lambda i, j: (layer, j, 0))
    hid = pl.BlockSpec((tm, tf), lambda i, j: (i, j))
    sd = lambda dt: jax.ShapeDtypeStruct((s, d), dt)
    sf = jax.ShapeDtypeStruct((s, f), BF16)
    return pl.pallas_call(
        body, name=name, grid=(s // tm, nf),
        in_specs=[row, vec, wsp, wsp, wsp, row],
        out_specs=[row, row, row, hid, hid, hid, vec],
        out_shape=[sd(F32), sd(BF16), sd(BF16), sf, sf, sf, jax.ShapeDtypeStruct((1, d), F32)],
        scratch_shapes=[pltpu.VMEM((tm, d), F32)] + [pltpu.VMEM((tm, tf), F32)] * 3,
        compiler_params=pltpu.CompilerParams(dimension_semantics=("arbitrary", "arbitrary"),
                                             vmem_limit_bytes=BIG_VMEM),
    )(h, gain, wgT, wuT, wd, dout)


def pool_fwd(name, h, gain, pw, scale, layer):
    s, d = h.shape
    tm = _tile(s, 512)

    def body(h_ref, g_ref, pw_ref, sc_ref, o_ref, y_ref, ext):
        i = pl.program_id(0)

        @pl.when(i == 0)
        def _():
            ext[0:HALO, :] = jnp.zeros((HALO, d), F32)

        xv = h_ref[...]
        ext[HALO:HALO + tm, :] = xv * _rinv(xv) * g_ref[...]
        pos = i * tm + lax.broadcasted_iota(jnp.int32, (tm, 1), 0)
        for g, w in enumerate(POOL_WINDOWS):
            lo, hi = g * PG, (g + 1) * PG
            ug = ext[HALO:HALO + tm, lo:hi]
            tot = ug
            for k in range(1, w):
                tot = tot + ext[HALO - k:HALO - k + tm, lo:hi]
            cnt = jnp.minimum(pos + 1, w).astype(F32)
            yb = (tot / cnt - ug).astype(BF16)
            y_ref[:, lo:hi] = yb
            o_ref[:, lo:hi] = xv[:, lo:hi] + _dot(yb, pw_ref[g], NN) * sc_ref[:, lo:hi]
        ext[0:HALO, :] = ext[tm:tm + HALO, :]

    row = pl.BlockSpec((tm, d), lambda i: (i, 0))
    vec = pl.BlockSpec((1, d), lambda i: (0, 0))
    return pl.pallas_call(
        body, name=name, grid=(s // tm,),
        in_specs=[row, vec, pl.BlockSpec((None, 4, PG, PG), lambda i: (layer, 0, 0, 0)), vec],
        out_specs=[row, row],
        out_shape=[jax.ShapeDtypeStruct((s, d), F32), jax.ShapeDtypeStruct((s, d), BF16)],
        scratch_shapes=[pltpu.VMEM((HALO + tm, d), F32)],
        compiler_params=_cp("arbitrary"),
    )(h, gain, pw, scale)


def pool_bwd(name, h, gain, y, pw, scale, layer, dout):
    s, d = h.shape
    tm = _tile(s, 512)
    ns = s // tm

    def body(h_ref, g_ref, y_ref, pw_ref, sc_ref, do_ref, dh_ref, dz_ref, dg_ref, dsc_ref, ext, du_sc):
        i = pl.program_id(0)
        t = ns - 1 - i

        @pl.when(i == 0)
        def _():
            ext[tm:tm + HALO, :] = jnp.zeros((HALO, d), F32)
            dg_ref[...] = jnp.zeros((1, d), F32)
            dsc_ref[...] = jnp.zeros((1, d), F32)

        pos = t * tm + lax.broadcasted_iota(jnp.int32, (tm, 1), 0)
        for g, w in enumerate(POOL_WINDOWS):
            lo, hi = g * PG, (g + 1) * PG
            dog = do_ref[:, lo:hi]
            z = _dot(y_ref[:, lo:hi], pw_ref[g], NN)
            dsc_ref[:, lo:hi] += jnp.sum(dog * z, axis=0, keepdims=True)
            dzb = (dog * sc_ref[:, lo:hi]).astype(BF16)
            dz_ref[:, lo:hi] = dzb
            dy = _dot(dzb, pw_ref[g], NT)
            cnt = jnp.minimum(pos + 1, w).astype(F32)
            ext[0:tm, lo:hi] = dy / cnt
            tot = ext[0:tm, lo:hi]
            for k in range(1, w):
                tot = tot + ext[k:k + tm, lo:hi]
            du_sc[:, lo:hi] = tot - dy
        ext[tm:tm + HALO, :] = ext[0:HALO, :]
        dx, dgain = _rms_bwd_math(h_ref[...], g_ref[...], du_sc[...])
        dh_ref[...] = do_ref[...] + dx
        dg_ref[...] += dgain

    row = pl.BlockSpec((tm, d), lambda i: (ns - 1 - i, 0))
    vec = pl.BlockSpec((1, d), lambda i: (0, 0))
    return pl.pallas_call(
        body, name=name, grid=(ns,),
        in_specs=[row, vec, row, pl.BlockSpec((None, 4, PG, PG), lambda i: (layer, 0, 0, 0)), vec, row],
        out_specs=[row, row, vec, vec],
        out_shape=[jax.ShapeDtypeStruct((s, d), F32), jax.ShapeDtypeStruct((s, d), BF16),
                   jax.ShapeDtypeStruct((1, d), F32), jax.ShapeDtypeStruct((1, d), F32)],
        scratch_shapes=[pltpu.VMEM((tm + HALO, d), F32), pltpu.VMEM((tm, d), F32)],
        compiler_params=_cp("arbitrary"),
    )(h, gain, y, pw, scale, dout)


def _rope128(raw_hi, cs):
    b = raw_hi * cs
    r = b + pltpu.roll(b, DR, 1)
    lane = lax.broadcasted_iota(jnp.int32, r.shape, 1)
    return jnp.where(lane < DR, r, 0.0)


def _rope128_bwd(t, cs):
    return (t + pltpu.roll(t, DR, 1)) * cs


def q_heads(name, cq, w256, cs):
    s = cq.shape[0]
    nh = w256.shape[0]
    tm = _tile(s, 1024)

    def body(cq_ref, w_ref, cs_ref, o_ref):
        r = _dot(cq_ref[...], w_ref[...], NN) * Q_PRESCALE
        o_ref[:, 0:DN] = r[:, 0:DN].astype(BF16)
        o_ref[:, DN:HD] = _rope128(r[:, DN:HD], cs_ref[...]).astype(BF16)

    return pl.pallas_call(
        body, name=name, grid=(s // tm, nh),
        in_specs=[pl.BlockSpec((tm, HD), lambda i, hh: (i, 0)),
                  pl.BlockSpec((None, HD, HD), lambda i, hh: (hh, 0, 0)),
                  pl.BlockSpec((tm, DN), lambda i, hh: (i, 0))],
        out_specs=pl.BlockSpec((None, tm, HD), lambda i, hh: (hh, i, 0)),
        out_shape=jax.ShapeDtypeStruct((nh, s, HD), BF16),
        compiler_params=_cp("parallel", "parallel"),
    )(cq, w256, cs)


def kv_post(name, kv_raw, gain, cs):
    s = kv_raw.shape[0]
    tm = _tile(s, 1024)

    def body(x_ref, g_ref, cs_ref, c_ref, kr_ref):
        c = x_ref[:, 0:DN]
        c_ref[...] = (c * _rinv(c) * g_ref[...]).astype(BF16)
        kr_ref[...] = _rope128(x_ref[:, DN:HD], cs_ref[...]).astype(BF16)

    half = pl.BlockSpec((tm, DN), lambda i: (i, 0))
    return pl.pallas_call(
        body, name=name, grid=(s // tm,),
        in_specs=[pl.BlockSpec((tm, HD), lambda i: (i, 0)), pl.BlockSpec((1, DN), lambda i: (0, 0)), half],
        out_specs=[half, half],
        out_shape=[jax.ShapeDtypeStruct((s, DN), BF16), jax.ShapeDtypeStruct((s, DN), BF16)],
        compiler_params=_cp("parallel"),
    )(kv_raw, gain, cs)


def kv_heads(name, c_kv, kr, wuk, wuv):
    s = c_kv.shape[0]
    nh = wuk.shape[0]
    tm = _tile(s, 1024)

    def body(c_ref, kr_ref, wk_ref, wv_ref, k_ref, v_ref):
        c = c_ref[...]
        k_ref[:, 0:DN] = _dot(c, wk_ref[...], NN).astype(BF16)
        k_ref[:, DN:HD] = kr_ref[...]
        v_ref[:, 0:DN] = _dot(c, wv_ref[...], NN).astype(BF16)
        v_ref[:, DN:HD] = jnp.ones((tm, DN), BF16)

    half = pl.BlockSpec((tm, DN), lambda i, hh: (i, 0))
    wsp = pl.BlockSpec((None, DN, DN), lambda i, hh: (hh, 0, 0))
    head = pl.BlockSpec((None, tm, HD), lambda i, hh: (hh, i, 0))
    return pl.pallas_call(
        body, name=name, grid=(s // tm, nh),
        in_specs=[half, half, wsp, wsp], out_specs=[head, head],
        out_shape=[jax.ShapeDtypeStruct((nh, s, HD), BF16), jax.ShapeDtypeStruct((nh, s, HD), BF16)],
        compiler_params=_cp("parallel", "parallel"),
    )(c_kv, kr, wuk, wuv)


def kv_post_bwd(name, kv_raw, gain, dc, dk, cs):
    s = kv_raw.shape[0]
    nh = dk.shape[0]
    tm = _tile(s, 1024)

    def body(x_ref, g_ref, dc_ref, dk_ref, cs_ref, o_ref, dg_ref, acc):
        i, hh = pl.program_id(0), pl.program_id(1)

        @pl.when(hh == 0)
        def _():
            acc[...] = jnp.zeros((tm, DN), F32)

        @pl.when((i == 0) & (hh == 0))
        def _():
            dg_ref[...] = jnp.zeros((1, DN), F32)

        acc[...] += dk_ref[...]

        @pl.when(hh == nh - 1)
        def _():
            dx, dgain = _rms_bwd_math(x_ref[:, 0:DN], g_ref[...], dc_ref[...])
            o_ref[:, 0:DN] = dx.astype(BF16)
            o_ref[:, DN:HD] = _rope128_bwd(acc[...], cs_ref[...]).astype(BF16)
            dg_ref[...] += dgain

    half = pl.BlockSpec((tm, DN), lambda i, hh: (i, 0))
    vec = pl.BlockSpec((1, DN), lambda i, hh: (0, 0))
    full = pl.BlockSpec((tm, HD), lambda i, hh: (i, 0))
    return pl.pallas_call(
        body, name=name, grid=(s // tm, nh),
        in_specs=[full, vec, half, pl.BlockSpec((None, tm, DN), lambda i, hh: (hh, i, 1)), half],
        out_specs=[full, vec],
        out_shape=[jax.ShapeDtypeStruct((s, HD), BF16), jax.ShapeDtypeStruct((1, DN), F32)],
        scratch_shapes=[pltpu.VMEM((tm, DN), F32)],
        compiler_params=_cp("arbitrary", "arbitrary"),
    )(kv_raw, gain, dc, dk, cs)


TQ = 1024
TC = 512
RB = 64


def _flash_cp():
    return pltpu.CompilerParams(dimension_semantics=("parallel", "arbitrary"), vmem_limit_bytes=BIG_VMEM)


def _causal(x, row0, col0, keys_on_rows=False):
    r = row0 + lax.broadcasted_iota(jnp.int32, x.shape, 0)
    c = col0 + lax.broadcasted_iota(jnp.int32, x.shape, 1)
    return jnp.where((r <= c) if keys_on_rows else (c <= r), x, NEG)


def _chunk_pipeline(n_pairs, chunk_of, scores, finish, bufs):
    b0, b1 = bufs
    scores(chunk_of(0), b0)

    def pair(t, carry):
        m = 2 * t
        scores(chunk_of(m + 1), b1)
        finish(chunk_of(m), b0, False)
        scores(chunk_of(m + 2), b0)
        finish(chunk_of(m + 1), b1, False)
        return carry

    lax.fori_loop(0, n_pairs, pair, 0)
    m = 2 * n_pairs
    scores(chunk_of(m + 1), b1)
    finish(chunk_of(m), b0, True)
    finish(chunk_of(m + 1), b1, True)


def flash_fwd(name, q, k, v2):
    nh, s, _ = q.shape
    tq, tk = _tile(s, TQ), _tile(s, TC)
    assert tq == 2 * tk
    nt = tk // LANES

    def body(q_ref, k_ref, v_ref, o_ref, lse_ref, s0_sc, s1_sc, p_sc, m_sc, acc):
        i = pl.program_id(1)
        m_sc[...] = jnp.full((tq, LANES), NEG, F32)
        acc[...] = jnp.zeros((tq, HD), F32)

        def scores(j, dst):
            dst[...] = _dot(q_ref[...], k_ref[pl.ds(pl.multiple_of(j * tk, tk), tk), :], NT)

        def softmax_rows(src, r0, j, masked):
            rows = pl.ds(r0, RB)

            def tile(t):
                x = src[rows, t * LANES:(t + 1) * LANES]
                return _causal(x, i * tq + r0, j * tk + t * LANES) if masked else x

            mx = tile(0)
            for t in range(1, nt):
                mx = jnp.maximum(mx, tile(t))
            m_prev = m_sc[rows, :]
            m_new = jnp.maximum(m_prev, jnp.max(mx, axis=-1, keepdims=True))
            alpha = jnp.exp2(m_prev - m_new)
            for t in range(nt):
                p_sc[rows, t * LANES:(t + 1) * LANES] = jnp.exp2(tile(t) - m_new).astype(BF16)
            m_sc[rows, :] = m_new
            acc[rows, 0:DN] = alpha * acc[rows, 0:DN]
            acc[rows, DN:HD] = alpha * acc[rows, DN:HD]

        def finish(j, src, masked):
            for r in range(tq // RB):
                softmax_rows(src, r * RB, j, masked)
            acc[...] += _dot(p_sc[...], v_ref[pl.ds(pl.multiple_of(j * tk, tk), tk), :], NN)

        _chunk_pipeline(i, lambda m: m, scores, finish, (s0_sc, s1_sc))
        l = acc[:, DN:HD]
        o_ref[...] = (acc[:, 0:DN] / l).astype(BF16)
        lse_ref[...] = m_sc[...] + jnp.log2(l)

    whole = pl.BlockSpec((None, s, HD), lambda hh, i: (hh, 0, 0))
    return pl.pallas_call(
        body, name=name, grid=(nh, s // tq),
        in_specs=[pl.BlockSpec((None, tq, HD), lambda hh, i: (hh, i, 0)), whole, whole],
        out_specs=[pl.BlockSpec((tq, DN), lambda hh, i: (i, hh)),
                   pl.BlockSpec((None, tq, DN), lambda hh, i: (hh, i, 0))],
        out_shape=[jax.ShapeDtypeStruct((s, nh * DN), BF16), jax.ShapeDtypeStruct((nh, s, DN), F32)],
        scratch_shapes=[pltpu.VMEM((tq, tk), F32), pltpu.VMEM((tq, tk), F32), pltpu.VMEM((tq, tk), BF16),
                        pltpu.VMEM((tq, LANES), F32), pltpu.VMEM((tq, HD), F32)],
        compiler_params=_flash_cp(),
    )(q, k, v2)


def attn_do2(name, o, do, nh):
    s = o.shape[0]
    tm = _tile(s, 1024)

    def body(o_ref, do_ref, d_ref):
        dov = do_ref[...]
        delta = jnp.sum(o_ref[...].astype(F32) * dov.astype(F32), axis=-1, keepdims=True)
        hi = delta.astype(BF16).astype(F32)
        lo = delta - hi
        lane = lax.broadcasted_iota(jnp.int32, (tm, DN), 1)
        d_ref[:, 0:DN] = dov
        d_ref[:, DN:HD] = jnp.where(lane == 0, -hi, jnp.where(lane == 1, -lo, 0.0)).astype(BF16)

    blk = pl.BlockSpec((tm, DN), lambda hh, i: (i, hh))
    return pl.pallas_call(
        body, name=name, grid=(nh, s // tm), in_specs=[blk, blk],
        out_specs=pl.BlockSpec((None, tm, HD), lambda hh, i: (hh, i, 0)),
        out_shape=jax.ShapeDtypeStruct((nh, s, HD), BF16), compiler_params=_cp("parallel", "parallel"),
    )(o, do)


def dq_post(name, dq_acc, cs):
    nh, s, _ = dq_acc.shape
    tm = _tile(s, 1024)

    def body(a_ref, cs_ref, o_ref):
        o_ref[:, 0:DN] = (a_ref[:, 0:DN] * ATT_SCALE).astype(BF16)
        o_ref[:, DN:HD] = _rope128_bwd(a_ref[:, DN:HD] * ATT_SCALE, cs_ref[...]).astype(BF16)

    tile = pl.BlockSpec((None, tm, HD), lambda i, hh: (hh, i, 0))
    return pl.pallas_call(
        body, name=name, grid=(s // tm, nh),
        in_specs=[tile, pl.BlockSpec((tm, DN), lambda i, hh: (i, 0))], out_specs=tile,
        out_shape=jax.ShapeDtypeStruct((nh, s, HD), BF16), compiler_params=_cp("parallel", "parallel"),
    )(dq_acc, cs)


def flash_bwd(name, q, k, v2, do2, lse_row, prev=None):
    nh, s, _ = q.shape
    tk, tq = _tile(s, TQ), _tile(s, TC)
    nqc = s // tq
    per = tk // tq
    nt = tq // LANES
    nkt = s // tk
    has_prev = prev is not None

    def body(*refs):
        k_ref, v_ref, q_ref, do_ref, lse_ref = refs[:5]
        dk_ref, dv_ref, dq_hbm, s_sc, d_sc, p_sc, ds_sc, dk_acc, dv_acc, dq_sc, sem = refs[-11:]
        hh, i = pl.program_id(0), pl.program_id(1)
        dk_acc[...] = jnp.zeros((tk, HD), F32)
        dv_acc[...] = jnp.zeros((tk, DN), F32)

        @pl.when(i == 0)
        def _():
            dq_sc[...] = jnp.zeros((s, HD), F32)

        def p_rows(r0, j, masked):
            rows = pl.ds(r0, RB)
            for t in range(nt):
                cols = slice(t * LANES, (t + 1) * LANES)
                x = s_sc[rows, cols]
                if masked:
                    x = _causal(x, i * tk + r0, j * tq + t * LANES, keys_on_rows=True)
                p = jnp.exp2(x - jnp.tile(lse_ref[j][:, cols], (RB // 8, 1)))
                p_sc[rows, cols] = p.astype(BF16)
                ds_sc[rows, cols] = (p * d_sc[rows, cols]).astype(BF16)

        def step(j, masked):
            start = pl.multiple_of(j * tq, tq)
            qj = q_ref[pl.ds(start, tq), :]
            s_sc[...] = _dot(k_ref[...], qj, NT)
            d_sc[...] = _dot(v_ref[...], do_ref[pl.ds(start, tq), :], NT)
            for r in range(tk // RB):
                p_rows(r * RB, j, masked)
            dv_acc[...] += _dot(p_sc[...], do_ref[pl.ds(start, tq), 0:DN], NN)
            ds = ds_sc[...]
            dk_acc[...] += _dot(ds, qj, NN)
            dq_sc[pl.ds(start, tq), :] += _dot(ds, k_ref[...], TN)

        def full_step(m, carry):
            step(nqc - 1 - m, False)
            return carry

        lax.fori_loop(0, nqc - (i + 1) * per, full_step, 0)
        for dd in reversed(range(per)):
            step(i * per + dd, True)
        if has_prev:
            dk_ref[...] = dk_acc[...] * LN2 + refs[5][...]
            dv_ref[...] = dv_acc[...] + refs[6][...]
        else:
            dk_ref[...] = dk_acc[...] * LN2
            dv_ref[...] = dv_acc[...]

        @pl.when(i == nkt - 1)
        def _():
            cp = pltpu.make_async_copy(dq_sc, dq_hbm.at[hh], sem)
            cp.start()
            cp.wait()

    kblk = pl.BlockSpec((None, tk, HD), lambda hh, i: (hh, i, 0))
    vblk = pl.BlockSpec((None, tk, DN), lambda hh, i: (hh, i, 0))
    whole = pl.BlockSpec((None, s, HD), lambda hh, i: (hh, 0, 0), pipeline_mode=pl.Buffered(1))
    in_specs = [kblk, kblk, whole, whole, pl.BlockSpec((None, nqc, 8, tq), lambda hh, i: (hh, 0, 0, 0))]
    args = [k, v2, q, do2, lse_row]
    if has_prev:
        in_specs += [kblk, vblk]
        args += list(prev)
    big = jax.ShapeDtypeStruct((nh, s, HD), F32)
    return pl.pallas_call(
        body, name=name, grid=(nh, nkt), in_specs=in_specs, out_specs=[kblk, vblk, ANY],
        out_shape=[big, jax.ShapeDtypeStruct((nh, s, DN), F32), big],
        scratch_shapes=[pltpu.VMEM((tk, tq), F32)] * 2 + [pltpu.VMEM((tk, tq), BF16)] * 2
        + [pltpu.VMEM((tk, HD), F32), pltpu.VMEM((tk, DN), F32), pltpu.VMEM((s, HD), F32), pltpu.SemaphoreType.DMA],
        compiler_params=pltpu.CompilerParams(dimension_semantics=("arbitrary", "arbitrary"), vmem_limit_bytes=BIG_VMEM),
    )(*args)


def loss_head(name, h, gain, target):
    s, d = h.shape
    tm = _tile(s, 1024)

    def body(h_ref, g_ref, t_ref, sq_ref, dh_ref, dg_ref):
        @pl.when(pl.program_id(0) == 0)
        def _():
            sq_ref[...] = jnp.zeros((1, d), F32)
            dg_ref[...] = jnp.zeros((1, d), F32)

        xv = h_ref[...]
        g = g_ref[...]
        rinv = _rinv(xv)
        xhat = xv * rinv
        err = xhat * g - t_ref[...]
        sq_ref[...] += jnp.sum(err * err, axis=0, keepdims=True)
        dy = err / d
        dg_ref[...] += jnp.sum(dy * xhat, axis=0, keepdims=True)
        dxh = dy * g
        dh_ref[...] = rinv * (dxh - xhat * jnp.mean(dxh * xhat, axis=-1, keepdims=True))

    row = pl.BlockSpec((tm, d), lambda i: (i, 0))
    vec = pl.BlockSpec((1, d), lambda i: (0, 0))
    return pl.pallas_call(
        body, name=name, grid=(s // tm,), in_specs=[row, vec, row], out_specs=[vec, row, vec],
        out_shape=[jax.ShapeDtypeStruct((1, d), F32), jax.ShapeDtypeStruct((s, d), F32),
                   jax.ShapeDtypeStruct((1, d), F32)],
        compiler_params=_cp("arbitrary"),
    )(h, gain, target)


def _adamw_math(w, g, m, v):
    m2 = ADAM_B1 * m + (1.0 - ADAM_B1) * g
    v2 = ADAM_B2 * v + (1.0 - ADAM_B2) * (g * g)
    mh = m2 / (1.0 - ADAM_B1 ** ADAM_STEP)
    vh = v2 / (1.0 - ADAM_B2 ** ADAM_STEP)
    delta = -ADAM_LR * (mh / (jnp.sqrt(vh) + ADAM_EPS) + ADAM_WD * w)
    return delta, m2, v2


def adamw(name, parts, w, m, v):
    rows, cols = w.shape
    tr = _tile(rows, max(8, (1 << 18) // cols))
    stacked = parts[0].ndim == 3
    npart = len(parts)

    def body(*refs):
        p_refs = refs[:npart]
        w_ref, m_ref, v_ref, g_ref, d_ref, m2_ref, v2_ref = refs[npart:]
        if stacked:
            g = p_refs[0][0]
            for n in range(1, parts[0].shape[0]):
                g = g + p_refs[0][n]
        else:
            g = p_refs[0][...]
            for r in p_refs[1:]:
                g = g + r[...]
        delta, m2, v2 = _adamw_math(w_ref[...], g, m_ref[...], v_ref[...])
        g_ref[...] = g
        d_ref[...] = delta
        m2_ref[...] = m2
        v2_ref[...] = v2

    blk = pl.BlockSpec((tr, cols), lambda i: (i, 0))
    pblk = pl.BlockSpec((parts[0].shape[0], tr, cols), lambda i: (0, i, 0)) if stacked else blk
    sds = jax.ShapeDtypeStruct((rows, cols), F32)
    return pl.pallas_call(
        body, name=name, grid=(rows // tr,), in_specs=[pblk] * npart + [blk] * 3,
        out_specs=[blk] * 4, out_shape=[sds] * 4, compiler_params=_cp("parallel"),
    )(*parts, w, m, v)


def sum4(name, x):
    _, a, r, c = x.shape
    tr = _tile(r, max(8, (1 << 18) // c))

    def body(x_ref, o_ref):
        o_ref[...] = ((x_ref[3].astype(F32) + x_ref[0].astype(F32)) + x_ref[1].astype(F32)) + x_ref[2].astype(F32)

    return pl.pallas_call(
        body, name=name, grid=(a, r // tr),
        in_specs=[pl.BlockSpec((4, None, tr, c), lambda i, j: (0, i, j, 0))],
        out_specs=pl.BlockSpec((None, tr, c), lambda i, j: (i, j, 0)),
        out_shape=jax.ShapeDtypeStruct((a, r, c), F32), compiler_params=_cp("parallel", "parallel"),
    )(x)


ANY = pl.BlockSpec(memory_space=pl.ANY)


def _mesh_pos():
    return lax.axis_index("x"), lax.axis_index("y"), lax.axis_index("c")


def _chip_peers(x, y):
    return [(1 - x, y), (x, 1 - y), (1 - x, 1 - y)]


def chip_all_gather(name, shards):
    n = len(shards)

    def body(*refs):
        ins, outs = refs[:n], refs[n:2 * n]
        send_sems, recv_sems, loc_sems = refs[2 * n:]
        x, y, c = _mesh_pos()
        me = 2 * x + y
        peers = _chip_peers(x, y)

        def remote(a, j, block):
            px, py = peers[j]
            return pltpu.make_async_remote_copy(
                src_ref=ins[a], dst_ref=outs[a].at[:, block], send_sem=send_sems.at[a * 3 + j],
                recv_sem=recv_sems.at[a * 3 + j], device_id=(px, py, c), device_id_type=MESH)

        local = [pltpu.make_async_copy(ins[a], outs[a].at[:, me], loc_sems.at[a]) for a in range(n)]
        for cp in local:
            cp.start()
        sends = [remote(a, j, me) for a in range(n) for j in range(3)]
        for cp in sends:
            cp.start()
        for a in range(n):
            for j, (px, py) in enumerate(peers):
                remote(a, j, 2 * px + py).wait_recv()
        for cp in sends:
            cp.wait_send()
        for cp in local:
            cp.wait()

    return pl.pallas_call(
        body, name=name, in_specs=[ANY] * n, out_specs=[ANY] * n,
        out_shape=[jax.ShapeDtypeStruct((s.shape[0], N_CHIPS) + s.shape[1:], s.dtype) for s in shards],
        scratch_shapes=[pltpu.SemaphoreType.DMA((3 * n,)), pltpu.SemaphoreType.DMA((3 * n,)),
                        pltpu.SemaphoreType.DMA((n,))],
    )(*shards)


def chip_all_to_all(name, fulls):
    n = len(fulls)

    def body(*refs):
        ins, outs = refs[:n], refs[n:2 * n]
        send_sems, recv_sems, loc_sems = refs[2 * n:]
        x, y, c = _mesh_pos()
        me = 2 * x + y
        peers = _chip_peers(x, y)

        def remote(a, j):
            px, py = peers[j]
            return pltpu.make_async_remote_copy(
                src_ref=ins[a].at[:, 2 * px + py], dst_ref=outs[a].at[j], send_sem=send_sems.at[a * 3 + j],
                recv_sem=recv_sems.at[a * 3 + j], device_id=(px, py, c), device_id_type=MESH)

        local = [pltpu.make_async_copy(ins[a].at[:, me], outs[a].at[3], loc_sems.at[a]) for a in range(n)]
        for cp in local:
            cp.start()
        sends = [remote(a, j) for a in range(n) for j in range(3)]
        for cp in sends:
            cp.start()
        for cp in sends:
            cp.wait_recv()
        for cp in sends:
            cp.wait_send()
        for cp in local:
            cp.wait()

    return pl.pallas_call(
        body, name=name, in_specs=[ANY] * n, out_specs=[ANY] * n,
        out_shape=[jax.ShapeDtypeStruct((N_CHIPS, f.shape[0]) + f.shape[2:], f.dtype) for f in fulls],
        scratch_shapes=[pltpu.SemaphoreType.DMA((3 * n,)), pltpu.SemaphoreType.DMA((3 * n,)),
                        pltpu.SemaphoreType.DMA((n,))],
    )(*fulls)


def sibling_exchange(name, arrs):
    n = len(arrs)

    def body(*refs):
        ins, outs = refs[:n], refs[n:2 * n]
        send_sems, recv_sems = refs[2 * n:]
        x, y, c = _mesh_pos()
        copies = [pltpu.make_async_remote_copy(
            src_ref=ins[a], dst_ref=outs[a], send_sem=send_sems.at[a], recv_sem=recv_sems.at[a],
            device_id=(x, y, 1 - c), device_id_type=MESH) for a in range(n)]
        for cp in copies:
            cp.start()
        for cp in copies:
            cp.wait_recv()
        for cp in copies:
            cp.wait_send()

    return pl.pallas_call(
        body, name=name, in_specs=[ANY] * n, out_specs=[ANY] * n,
        out_shape=[jax.ShapeDtypeStruct(a.shape, a.dtype) for a in arrs],
        scratch_shapes=[pltpu.SemaphoreType.DMA((n,)), pltpu.SemaphoreType.DMA((n,))],
    )(*arrs)


def all_gather8(name, buf):
    r, cdim = buf.shape

    def body(in_ref, out_ref, send_sems, recv_sems, loc_sem):
        x, y, c = _mesh_pos()
        flips = [(fx, fy, fc) for fx in (0, 1) for fy in (0, 1) for fc in (0, 1)][1:]

        def peer(f):
            return tuple((1 - p) if b else p for p, b in zip((x, y, c), f))

        def remote(j, slot):
            return pltpu.make_async_remote_copy(
                src_ref=in_ref, dst_ref=out_ref.at[slot], send_sem=send_sems.at[j], recv_sem=recv_sems.at[j],
                device_id=peer(flips[j]), device_id_type=MESH)

        me = 4 * x + 2 * y + c
        local = pltpu.make_async_copy(in_ref, out_ref.at[me], loc_sem)
        local.start()
        sends = [remote(j, me) for j in range(7)]
        for cp in sends:
            cp.start()
        for j in range(7):
            px, py, pc = peer(flips[j])
            remote(j, 4 * px + 2 * py + pc).wait_recv()
        for cp in sends:
            cp.wait_send()
        local.wait()

    return pl.pallas_call(
        body, name=name, in_specs=[ANY], out_specs=ANY,
        out_shape=jax.ShapeDtypeStruct((8, r, cdim), buf.dtype),
        scratch_shapes=[pltpu.SemaphoreType.DMA((7,)), pltpu.SemaphoreType.DMA((7,)), pltpu.SemaphoreType.DMA],
    )(buf)


def _rot_cols(w):
    half = w.shape[-1] // 2
    return jnp.concatenate([-w[..., half:], w[..., :half]], axis=-1)


def _fold_rot(g):
    rot = g[..., DN + DR:]
    half = DR // 2
    return jnp.concatenate([g[..., :DN], g[..., DN:DN + DR] + jnp.concatenate([rot[..., half:], -rot[..., :half]], -1)], -1)


def _with_rot(w):
    return jnp.concatenate([w, _rot_cols(w[..., DN:])], axis=-1)


def _rope_table(s):
    pos = jnp.arange(s, dtype=F32)
    inv_freq = ROPE_THETA ** (-jnp.arange(0, DR, 2, dtype=F32) / DR)
    ang = pos[:, None] * inv_freq[None, :]
    cos, sin = jnp.cos(ang), jnp.sin(ang)
    return jnp.concatenate([cos, cos, sin, sin], axis=-1)


def _row_stats(x, tq):
    nh, s, _ = x.shape
    return jnp.broadcast_to(x[:, :, 0].reshape(nh, s // tq, 1, tq), (nh, s // tq, 8, tq))


REPL_NAMES = ["ffn_pre_norm", "mix_norm", "ffn_post_norm", "kv_in_norm", "final_norm", "ckv_norm",
              "q_lora_norm", "pool_scale_full", "w_uk", "w_uv"]
PACK_COLS = 1024


def _pack_rows(arrs):
    rows, counts = [], []
    for a in arrs:
        flat = a.reshape(-1).astype(F32)
        n = -(-flat.shape[0] // PACK_COLS)
        flat = jnp.pad(flat, (0, n * PACK_COLS - flat.shape[0]))
        rows.append(flat.reshape(n, PACK_COLS))
        counts.append(n)
    total = sum(counts)
    pad = -total % 8
    if pad:
        rows.append(jnp.zeros((pad, PACK_COLS), F32))
    return jnp.concatenate(rows, axis=0), counts


def _unpack_rows(buf, counts, shapes):
    out, r = [], 0
    for n, shp in zip(counts, shapes):
        size = 1
        for dd in shp:
            size *= dd
        out.append(buf[r:r + n].reshape(-1)[:size].reshape(shp))
        r += n
    return out


def kernel(x, ffn_pre_norm, ffn_pre_wg, ffn_pre_wu, ffn_pre_wd, mix_norm, ffn_post_norm, ffn_post_wg, ffn_post_wu, ffn_post_wd, pool_w, pool_scale, kv_in_norm, w_dkv, ckv_norm, w_uk, w_uv, q_lora_norm, w_dq, w_uq, w_o, final_norm, loss_target, m_ffn_pre_norm, m_ffn_pre_wg, m_ffn_pre_wu, m_ffn_pre_wd, m_mix_norm, m_ffn_post_norm, m_ffn_post_wg, m_ffn_post_wu, m_ffn_post_wd, m_pool_w, m_pool_scale, m_kv_in_norm, m_w_dkv, m_ckv_norm, m_w_uk, m_w_uv, m_q_lora_norm, m_w_dq, m_w_uq, m_w_o, m_final_norm, v_ffn_pre_norm, v_ffn_pre_wg, v_ffn_pre_wu, v_ffn_pre_wd, v_mix_norm, v_ffn_post_norm, v_ffn_post_wg, v_ffn_post_wu, v_ffn_post_wd, v_pool_w, v_pool_scale, v_kv_in_norm, v_w_dkv, v_ckv_norm, v_w_uk, v_w_uv, v_q_lora_norm, v_w_dq, v_w_uq, v_w_o, v_final_norm):
    weights = dict(ffn_pre_norm=ffn_pre_norm, ffn_pre_wg=ffn_pre_wg, ffn_pre_wu=ffn_pre_wu, ffn_pre_wd=ffn_pre_wd,
                   mix_norm=mix_norm, ffn_post_norm=ffn_post_norm, ffn_post_wg=ffn_post_wg, ffn_post_wu=ffn_post_wu,
                   ffn_post_wd=ffn_post_wd, pool_w=pool_w, pool_scale=pool_scale, kv_in_norm=kv_in_norm, w_dkv=w_dkv,
                   ckv_norm=ckv_norm, w_uk=w_uk, w_uv=w_uv, q_lora_norm=q_lora_norm, w_dq=w_dq, w_uq=w_uq, w_o=w_o,
                   final_norm=final_norm)
    mom1 = dict(ffn_pre_norm=m_ffn_pre_norm, ffn_pre_wg=m_ffn_pre_wg, ffn_pre_wu=m_ffn_pre_wu, ffn_pre_wd=m_ffn_pre_wd,
                mix_norm=m_mix_norm, ffn_post_norm=m_ffn_post_norm, ffn_post_wg=m_ffn_post_wg, ffn_post_wu=m_ffn_post_wu,
                ffn_post_wd=m_ffn_post_wd, pool_w=m_pool_w, pool_scale=m_pool_scale, kv_in_norm=m_kv_in_norm,
                w_dkv=m_w_dkv, ckv_norm=m_ckv_norm, w_uk=m_w_uk, w_uv=m_w_uv, q_lora_norm=m_q_lora_norm, w_dq=m_w_dq,
                w_uq=m_w_uq, w_o=m_w_o, final_norm=m_final_norm)
    mom2 = dict(ffn_pre_norm=v_ffn_pre_norm, ffn_pre_wg=v_ffn_pre_wg, ffn_pre_wu=v_ffn_pre_wu, ffn_pre_wd=v_ffn_pre_wd,
                mix_norm=v_mix_norm, ffn_post_norm=v_ffn_post_norm, ffn_post_wg=v_ffn_post_wg, ffn_post_wu=v_ffn_post_wu,
                ffn_post_wd=v_ffn_post_wd, pool_w=v_pool_w, pool_scale=v_pool_scale, kv_in_norm=v_kv_in_norm,
                w_dkv=v_w_dkv, ckv_norm=v_ckv_norm, w_uk=v_w_uk, w_uv=v_w_uv, q_lora_norm=v_q_lora_norm, w_dq=v_w_dq,
                w_uq=v_w_uq, w_o=v_w_o, final_norm=v_final_norm)
    names = list(weights)

    h0 = x[0]
    target = loss_target[0]
    s, d = h0.shape
    nh = N_HEADS
    nb = DEPTH - N_A
    f_loc = ffn_pre_wd.shape[1]
    ffn_dim = N_CHIPS * f_loc
    my_chip = 2 * lax.axis_index("x") + lax.axis_index("y")

    hidden_major = lambda w: jnp.transpose(w, (0, 2, 1)).astype(BF16)
    uq_loc = jnp.transpose(w_uq, (0, 2, 1, 3))
    shards = [
        hidden_major(ffn_pre_wg), hidden_major(ffn_pre_wu), ffn_pre_wd.astype(BF16),
        hidden_major(ffn_post_wg), hidden_major(ffn_post_wu), ffn_post_wd.astype(BF16),
        pool_w.reshape(N_A * 4, PG // N_CHIPS, PG).astype(BF16),
        _with_rot(w_dkv)[None].astype(BF16),
        w_dq.astype(BF16),
        _with_rot(uq_loc).reshape(nb * nh, uq_loc.shape[2], HD).astype(BF16),
        w_o.astype(BF16),
        pool_scale.reshape(N_A, 1, PG),
    ]
    gathered = chip_all_gather("gather_weights", shards)
    merge = lambda g: g.reshape(g.shape[0], N_CHIPS * g.shape[2], g.shape[3])
    pre_wgT, pre_wuT, pre_wd, post_wgT, post_wuT, post_wd = [merge(g) for g in gathered[:6]]
    pool_w_full = merge(gathered[6]).reshape(N_A, 4, PG, PG)
    wdkv256 = merge(gathered[7])[0]
    wdq_full = merge(gathered[8])
    wq256 = merge(gathered[9]).reshape(nb, nh, HD, HD)
    wo_full = merge(gathered[10])
    pool_scale_full = gathered[11].reshape(N_A, d)
    wuk_h = jnp.transpose(w_uk, (1, 0, 2)).astype(BF16)
    wuv_h = jnp.transpose(w_uv, (1, 0, 2)).astype(BF16)
    cs = _rope_table(s)

    ffn_w = {"pre": (ffn_pre_norm, pre_wgT, pre_wuT, pre_wd), "post": (ffn_post_norm, post_wgT, post_wuT, post_wd)}

    saved = {}
    h = h0
    kv = None
    for l in range(DEPTH):
        saved["a", l] = h
        g_, wg_, wu_, wd_ = ffn_w["pre"]
        h = ffn_fwd(f"ffn_pre_fwd{l}", h, g_[l:l + 1], wg_, wu_, wd_, l)
        saved["b", l] = h
        if l < N_A:
            h, y = pool_fwd(f"pool_fwd{l}", h, mix_norm[l:l + 1], pool_w_full, pool_scale_full[l:l + 1], l)
            saved["y", l] = y
        else:
            j = l - N_A
            u = rms_fwd(f"mix_norm_fwd{l}", h, mix_norm[l:l + 1])
            cq_raw = mm_nn(f"q_down{l}", u, wdq_full[j], F32, 2048, 256, 1024)
            cq = rms_fwd(f"q_norm_fwd{l}", cq_raw, q_lora_norm[j:j + 1])
            q = q_heads(f"q_heads{l}", cq, wq256[j], cs)
            o, lse = flash_fwd(f"flash_fwd{l}", q, kv[0], kv[1])
            saved["att", l] = (u, cq_raw, cq, q, o, lse)
            h = mm_nn(f"attn_out{l}", o, wo_full[j], F32, 1024, 1024, 1024, res=h)
        saved["c", l] = h
        g_, wg_, wu_, wd_ = ffn_w["post"]
        h = ffn_fwd(f"ffn_post_fwd{l}", h, g_[l:l + 1], wg_, wu_, wd_, l)
        if l == N_A - 1:
            u_kv = rms_fwd("kv_in_norm_fwd", h, kv_in_norm[None])
            kv_raw = mm_nn("kv_down", u_kv, wdkv256, F32, 2048, 256, 1024)
            c_kv, kr = kv_post("kv_post", kv_raw, ckv_norm[None], cs)
            kv = kv_heads("kv_heads", c_kv, kr, wuk_h, wuv_h)
            saved["kv"] = (u_kv, kv_raw, c_kv)

    sq, dh, g_final = loss_head("loss_head", h, final_norm[None], target)
    loss = lax.psum(0.5 * jnp.sum(sq) / d, ("x", "y", "c"))

    grads = {}
    gvec = {n: [None] * DEPTH for n in ("ffn_pre_norm", "mix_norm", "ffn_post_norm")}
    gffn = {(p, m): [None] * DEPTH for p in ("pre", "post") for m in ("wg", "wu", "wd")}
    g_pool_w, g_pool_scale = [None] * N_A, [None] * N_A
    g_qnorm, g_wdq, g_wq256, g_wo = [None] * nb, [None] * nb, [None] * nb, [None] * nb
    dkv = None

    def ffn_backward(part, l, hin, dh):
        g_, wg_, wu_, wd_ = ffn_w[part]
        dh, u, dob, act, da, db, dgain = ffn_bwd(f"ffn_{part}_bwd{l}", hin, g_[l:l + 1], wg_, wu_, wd_, l, dh)
        gvec[f"ffn_{part}_norm"][l] = dgain[0]
        gffn[part, "wg"][l] = mm_tn(f"ffn_{part}_dwg{l}", da, u, BF16, ffn_dim // 2, d, 1024)
        gffn[part, "wu"][l] = mm_tn(f"ffn_{part}_dwu{l}", db, u, BF16, ffn_dim // 2, d, 1024)
        gffn[part, "wd"][l] = mm_tn(f"ffn_{part}_dwd{l}", act, dob, BF16, ffn_dim // 2, d, 1024)
        return dh

    for l in reversed(range(DEPTH)):
        if l == N_A - 1:
            u_kv, kv_raw, c_kv = saved["kv"]
            dk, dv = dkv
            tm = _tile(s, 2048)
            head_red = dict(grid=(s // tm, 1, nh), b_block=(None, DN, DN), b_map=lambda i, jj, kk: (kk, 0, 0),
                            o_block=(tm, DN), o_map=lambda i, jj, kk: (i, 0), out_shape=(s, DN), dims=NT, out_dtype=F32)
            dc = _mm("kv_dc_k", dk, wuk_h, a_block=(None, tm, DN), a_map=lambda i, jj, kk: (kk, i, 0), **head_red)
            dc = _mm("kv_dc_v", dv, wuv_h, a_block=(None, tm, DN), a_map=lambda i, jj, kk: (kk, i, 0), res=dc, **head_red)
            tk = _tile(s, 2048)
            head_tn = dict(grid=(nh, 1, s // tk), a_block=(tk, DN), a_map=lambda hh, jj, kk: (kk, 0),
                           b_block=(None, tk, DN), b_map=lambda hh, jj, kk: (hh, kk, 0),
                           o_block=(None, DN, DN), o_map=lambda hh, jj, kk: (hh, 0, 0),
                           out_shape=(nh, DN, DN), dims=TN, out_dtype=F32)
            grads["w_uk"] = jnp.transpose(_mm("kv_dwuk", c_kv, dk, **head_tn), (1, 0, 2))
            grads["w_uv"] = jnp.transpose(_mm("kv_dwuv", c_kv, dv, **head_tn), (1, 0, 2))
            dkv_raw, dg_ckv = kv_post_bwd("kv_post_bwd", kv_raw, ckv_norm[None], dc, dk, cs)
            grads["ckv_norm"] = dg_ckv[0]
            du_kv = mm_nt("kv_du", dkv_raw, wdkv256, F32, 2048, 1024, 256)
            g_wdkv256 = mm_tn("kv_dwdkv", u_kv, dkv_raw, BF16, 1024, 256, 2048)
            dh, dg_kvin = rms_bwd("kv_in_norm_bwd", saved["a", l + 1], kv_in_norm[None], du_kv, dres=dh)
            grads["kv_in_norm"] = dg_kvin[0]

        dh = ffn_backward("post", l, saved["c", l], dh)

        if l < N_A:
            dh, dz, dgain, dscale = pool_bwd(f"pool_bwd{l}", saved["b", l], mix_norm[l:l + 1], saved["y", l],
                                             pool_w_full, pool_scale_full[l:l + 1], l, dh)
            gvec["mix_norm"][l] = dgain[0]
            g_pool_scale[l] = dscale[0]
            tk = _tile(s, 2048)
            g_pool_w[l] = _mm(f"pool_dw{l}", saved["y", l], dz, grid=(4, 1, s // tk),
                              a_block=(tk, PG), a_map=lambda g, jj, kk: (kk, g),
                              b_block=(tk, PG), b_map=lambda g, jj, kk: (kk, g),
                              o_block=(None, PG, PG), o_map=lambda g, jj, kk: (g, 0, 0),
                              out_shape=(4, PG, PG), dims=TN, out_dtype=BF16)
        else:
            j = l - N_A
            u, cq_raw, cq, q, o, lse = saved["att", l]
            k_, v_ = kv
            do = mm_nt(f"attn_do{l}", dh, wo_full[j], BF16, 1024, 1024, 1024)
            g_wo[j] = mm_tn(f"attn_dwo{l}", o, dh, BF16, 1024, 1024, 1024)
            do2 = attn_do2(f"attn_do2_{l}", o, do, nh)
            dk_, dv_, dq_acc = flash_bwd(f"flash_bwd{l}", q, k_, v_, do2, _row_stats(lse, _tile(s, TC)), prev=dkv)
            dkv = (dk_, dv_)
            dq_raw = dq_post(f"dq_post{l}", dq_acc, cs)
            tm = _tile(s, 2048)
            dcq = _mm(f"q_dcq{l}", dq_raw, wq256[j], grid=(s // tm, 1, nh),
                      a_block=(None, tm, HD), a_map=lambda i, jj, kk: (kk, i, 0),
                      b_block=(None, HD, HD), b_map=lambda i, jj, kk: (kk, 0, 0),
                      o_block=(tm, HD), o_map=lambda i, jj, kk: (i, 0),
                      out_shape=(s, HD), dims=NT, out_dtype=F32)
            tk = _tile(s, 2048)
            g_wq256[j] = _mm(f"q_dwq{l}", cq, dq_raw, grid=(nh, 1, s // tk),
                             a_block=(tk, HD), a_map=lambda hh, jj, kk: (kk, 0),
                             b_block=(None, tk, HD), b_map=lambda hh, jj, kk: (hh, kk, 0),
                             o_block=(None, HD, HD), o_map=lambda hh, jj, kk: (hh, 0, 0),
                             out_shape=(nh, HD, HD), dims=TN, out_dtype=BF16)
            dcq_raw, dg_q = rms_bwd(f"q_norm_bwd{l}", cq_raw, q_lora_norm[j:j + 1], dcq)
            g_qnorm[j] = dg_q[0]
            du = mm_nt(f"q_du{l}", dcq_raw, wdq_full[j], F32, 2048, 1024, 256)
            g_wdq[j] = mm_tn(f"q_dwdq{l}", u, dcq_raw, BF16, 1024, 256, 2048)
            dh, dgain = rms_bwd(f"mix_norm_bwd{l}", saved["b", l], mix_norm[l:l + 1], du, dres=dh)
            gvec["mix_norm"][l] = dgain[0]

        dh = ffn_backward("pre", l, saved["a", l], dh)

    grad_x = dh[None]

    split = lambda g: g.reshape(g.shape[0], N_CHIPS, g.shape[1] // N_CHIPS, g.shape[2])
    fulls = [split(jnp.stack(gffn[p, m])) for p in ("pre", "post") for m in ("wg", "wu", "wd")]
    fulls += [
        split(jnp.stack(g_pool_w).reshape(N_A * 4, PG, PG)),
        split(g_wdkv256[None]),
        split(jnp.stack(g_wdq)),
        split(jnp.stack(g_wq256).reshape(nb * nh, HD, HD)),
        split(jnp.stack(g_wo)),
    ]
    sharded_names = ["ffn_pre_wg", "ffn_pre_wu", "ffn_pre_wd", "ffn_post_wg", "ffn_post_wu", "ffn_post_wd",
                     "pool_w", "w_dkv", "w_dq", "w_uq", "w_o"]
    received = chip_all_to_all("exchange_grads", fulls)
    partial = [sum4(f"chip_sum_{n}", r) for n, r in zip(sharded_names, received)]
    sibling = sibling_exchange("exchange_sibling", partial)

    def natural(name, p):
        if name in ("ffn_pre_wg", "ffn_pre_wu", "ffn_post_wg", "ffn_post_wu"):
            return jnp.transpose(p, (0, 2, 1))
        if name == "w_dkv":
            return _fold_rot(p[0])
        if name == "w_uq":
            return jnp.transpose(_fold_rot(p).reshape(nb, nh, p.shape[1], QK_DIM), (0, 2, 1, 3))
        return p.reshape(weights[name].shape)

    results = {}
    for n, p, q_ in zip(sharded_names, partial, sibling):
        shp = weights[n].shape
        as2d = lambda a: a.reshape(-1, shp[-1])
        outs = adamw(f"adamw_{n}", [as2d(natural(n, p)), as2d(natural(n, q_))],
                     as2d(weights[n]), as2d(mom1[n]), as2d(mom2[n]))
        results[n] = [o_.reshape(shp) for o_ in outs]

    grads.update(ffn_pre_norm=jnp.stack(gvec["ffn_pre_norm"]), mix_norm=jnp.stack(gvec["mix_norm"]),
                 ffn_post_norm=jnp.stack(gvec["ffn_post_norm"]), final_norm=g_final[0],
                 q_lora_norm=jnp.stack(g_qnorm), pool_scale_full=jnp.stack(g_pool_scale))
    repl_shapes = [grads[n].shape for n in REPL_NAMES]
    packed_g, counts = _pack_rows([grads[n] for n in REPL_NAMES])
    all_g = all_gather8("gather_small_grads", packed_g)
    zeros_ps = jnp.zeros((N_A, d), F32)
    pick = lambda src: [zeros_ps if n == "pool_scale_full" else src[n] for n in REPL_NAMES]
    packed = [_pack_rows(pick(src))[0] for src in (weights, mom1, mom2)]
    outs = adamw("adamw_replicated", [all_g], *packed)
    unpacked = [_unpack_rows(o_, counts, repl_shapes) for o_ in outs]
    for idx, n in enumerate(REPL_NAMES):
        results[n] = [u_[idx] for u_ in unpacked]
    g_ps = lax.dynamic_index_in_dim(results["pool_scale_full"][0].reshape(N_A, N_CHIPS, PG), my_chip, axis=1, keepdims=False)
    results["pool_scale"] = adamw("adamw_pool_scale", [g_ps], pool_scale, m_pool_scale, v_pool_scale)

    out = [loss, grad_x]
    for part in range(4):
        out += [results[n][part] for n in names]
    return tuple(out)
```

```python
import functools

import jax
import jax.numpy as jnp
from jax import lax
from jax.experimental import pallas as pl
from jax.experimental.pallas import tpu as pltpu

F32 = jnp.float32
BF16 = jnp.bfloat16
MESH = pl.DeviceIdType.MESH

DEPTH = 4
N_A = 2
N_HEADS = 16
DN = 128
DR = 64
HD = 256
QK_DIM = DN + DR
POOL_WINDOWS = (2, 4, 8, 16)
PG = 256
HALO = 16
N_CHIPS = 4
RMS_EPS = 1e-6
ROPE_THETA = 10000.0
ATT_SCALE = QK_DIM ** -0.5
LOG2E = 1.4426950408889634
LN2 = 0.6931471805599453
Q_PRESCALE = ATT_SCALE * LOG2E
NEG = -1e30
LANES = 128

ADAM_LR = 0.001
ADAM_B1 = 0.9
ADAM_B2 = 0.999
ADAM_EPS = 1e-08
ADAM_WD = 0.01
ADAM_STEP = 10

VMEM_BYTES_V7X = 64 * 1024 * 1024
VMEM_LIMIT = VMEM_BYTES_V7X * 3 // 4
BIG_VMEM = VMEM_BYTES_V7X * 7 // 8
FFN_RB = 64

NN = ((1,), (0,))
NT = ((1,), (1,))
TN = ((0,), (0,))


def _cp(*sem):
    return pltpu.CompilerParams(dimension_semantics=sem, vmem_limit_bytes=VMEM_LIMIT)


def _dot(a, b, dims):
    return lax.dot_general(a, b, (dims, ((), ())), preferred_element_type=F32)


def _tile(n, pref):
    if n <= pref:
        return n
    if n % pref == 0:
        return pref
    t = 1 << (pref.bit_length() - 1)
    while n % t:
        t //= 2
    return t


def _rinv(x):
    return lax.rsqrt(jnp.mean(x * x, axis=-1, keepdims=True) + RMS_EPS)


def _sigmoid(a):
    return 1.0 / (1.0 + jnp.exp(-a))


def _mm(name, a, b, *, grid, a_block, a_map, b_block, b_map, o_block, o_map, out_shape, dims,
        out_dtype, res=None):
    nax = len(grid)
    nk = grid[-1]
    acc_shape = tuple(d for d in o_block if d is not None)
    has_res = res is not None

    def body(*refs):
        a_ref, b_ref = refs[0], refs[1]
        o_ref, acc = refs[-2], refs[-1]
        k = pl.program_id(nax - 1)

        @pl.when(k == 0)
        def _():
            acc[...] = jnp.zeros(acc_shape, F32)

        acc[...] += _dot(a_ref[...].astype(BF16), b_ref[...].astype(BF16), dims)

        @pl.when(k == nk - 1)
        def _():
            r = acc[...]
            if has_res:
                r = r + refs[2][...].astype(F32)
            o_ref[...] = r.astype(out_dtype)

    in_specs = [pl.BlockSpec(a_block, a_map), pl.BlockSpec(b_block, b_map)]
    args = [a, b]
    if has_res:
        in_specs.append(pl.BlockSpec(o_block, o_map))
        args.append(res)
    return pl.pallas_call(
        body, name=name, grid=grid, in_specs=in_specs, out_specs=pl.BlockSpec(o_block, o_map),
        out_shape=jax.ShapeDtypeStruct(out_shape, out_dtype),
        scratch_shapes=[pltpu.VMEM(acc_shape, F32)],
        compiler_params=_cp(*(("parallel",) * (nax - 1) + ("arbitrary",))),
    )(*args)


def mm_nn(name, a, b, out_dtype, tm, tn, tk, res=None):
    (m, kd), n = a.shape, b.shape[1]
    tm, tn, tk = _tile(m, tm), _tile(n, tn), _tile(kd, tk)
    return _mm(name, a, b, grid=(m // tm, n // tn, kd // tk),
               a_block=(tm, tk), a_map=lambda i, j, k: (i, k),
               b_block=(tk, tn), b_map=lambda i, j, k: (k, j),
               o_block=(tm, tn), o_map=lambda i, j, k: (i, j),
               out_shape=(m, n), dims=NN, out_dtype=out_dtype, res=res)


def mm_nt(name, a, b, out_dtype, tm, tn, tk):
    (m, kd), n = a.shape, b.shape[0]
    tm, tn, tk = _tile(m, tm), _tile(n, tn), _tile(kd, tk)
    return _mm(name, a, b, grid=(m // tm, n // tn, kd // tk),
               a_block=(tm, tk), a_map=lambda i, j, k: (i, k),
               b_block=(tn, tk), b_map=lambda i, j, k: (j, k),
               o_block=(tm, tn), o_map=lambda i, j, k: (i, j),
               out_shape=(m, n), dims=NT, out_dtype=out_dtype)


def mm_tn(name, a, b, out_dtype, tm, tn, tk):
    (kd, m), n = a.shape, b.shape[1]
    tm, tn, tk = _tile(m, tm), _tile(n, tn), _tile(kd, tk)
    return _mm(name, a, b, grid=(m // tm, n // tn, kd // tk),
               a_block=(tk, tm), a_map=lambda i, j, k: (k, i),
               b_block=(tk, tn), b_map=lambda i, j, k: (k, j),
               o_block=(tm, tn), o_map=lambda i, j, k: (i, j),
               out_shape=(m, n), dims=TN, out_dtype=out_dtype)


def rms_fwd(name, x, gain):
    s, d = x.shape
    tm = _tile(s, 1024)

    def body(x_ref, g_ref, o_ref):
        xv = x_ref[...]
        o_ref[...] = (xv * _rinv(xv) * g_ref[...]).astype(BF16)

    return pl.pallas_call(
        body, name=name, grid=(s // tm,),
        in_specs=[pl.BlockSpec((tm, d), lambda i: (i, 0)), pl.BlockSpec((1, d), lambda i: (0, 0))],
        out_specs=pl.BlockSpec((tm, d), lambda i: (i, 0)),
        out_shape=jax.ShapeDtypeStruct((s, d), BF16), compiler_params=_cp("parallel"),
    )(x, gain)


def _rms_bwd_math(xv, g, du):
    rinv = _rinv(xv)
    xhat = xv * rinv
    dgain = jnp.sum(du * xhat, axis=0, keepdims=True)
    dxh = du * g
    dx = rinv * (dxh - xhat * jnp.mean(dxh * xhat, axis=-1, keepdims=True))
    return dx, dgain


def rms_bwd(name, x, gain, du, dres=None):
    s, d = x.shape
    tm = _tile(s, 1024)
    has_res = dres is not None

    def body(*refs):
        x_ref, g_ref, du_ref = refs[:3]
        dx_ref, dg_ref = refs[-2:]

        @pl.when(pl.program_id(0) == 0)
        def _():
            dg_ref[...] = jnp.zeros((1, d), F32)

        dx, dgain = _rms_bwd_math(x_ref[...], g_ref[...], du_ref[...].astype(F32))
        if has_res:
            dx = dx + refs[3][...]
        dx_ref[...] = dx
        dg_ref[...] += dgain

    row = pl.BlockSpec((tm, d), lambda i: (i, 0))
    vec = pl.BlockSpec((1, d), lambda i: (0, 0))
    args = [x, gain, du] + ([dres] if has_res else [])
    return pl.pallas_call(
        body, name=name, grid=(s // tm,),
        in_specs=[row, vec, row] + ([row] if has_res else []),
        out_specs=[row, vec],
        out_shape=[jax.ShapeDtypeStruct((s, d), F32), jax.ShapeDtypeStruct((1, d), F32)],
        compiler_params=_cp("arbitrary"),
    )(*args)


def ffn_fwd(name, h, gain, wgT, wuT, wd, carry=None):
    s, d = h.shape
    f = wd.shape[0]
    tm, tf = _tile(s, 1024), _tile(f, 256)
    nf = f // tf
    carry = carry or _NoCarry()
    grid = (s // tm, nf)

    def body(*refs):
        (h_ref, g_ref, wg_ref, wu_ref, wd_ref), c_in, (o_ref,), c_out, (u_sc, acc), c_sem = carry.split(refs, 5, 1, 2)
        j = pl.program_id(1)
        carry.at_first_step(grid, c_in, c_out, c_sem)

        @pl.when(j == 0)
        def _():
            xv = h_ref[...]
            u_sc[...] = (xv * _rinv(xv) * g_ref[...]).astype(BF16)
            acc[...] = jnp.zeros((tm, d), F32)

        u = u_sc[...]
        a = _dot(u, wg_ref[...], NT)
        b = _dot(u, wu_ref[...], NT)
        act = (a * _sigmoid(a) * b).astype(BF16)
        acc[...] += _dot(act, wd_ref[...], NN)

        @pl.when(j == nf - 1)
        def _():
            o_ref[...] = h_ref[...] + 0.5 * acc[...]

        carry.at_last_step(grid, c_in, c_out, c_sem)

    row = pl.BlockSpec((tm, d), lambda i, j: (i, 0))
    wsp = pl.BlockSpec((tf, d), lambda i, j: (j, 0))
    outs = pl.pallas_call(
        body, name=name, grid=grid,
        in_specs=[row, pl.BlockSpec((1, d), lambda i, j: (0, 0)), wsp, wsp, wsp] + carry.in_specs(),
        out_specs=[row] + carry.out_specs(), out_shape=[jax.ShapeDtypeStruct((s, d), F32)] + carry.out_shapes(),
        scratch_shapes=[pltpu.VMEM((tm, d), BF16), pltpu.VMEM((tm, d), F32)] + carry.scratch(),
        compiler_params=_cp("arbitrary", "arbitrary"),
    )(h, gain, wgT, wuT, wd, *carry.arrays)
    return outs[0], list(outs[1:])


def ffn_bwd(name, h, gain, wgT, wuT, wd, dout, carry=None):
    s, d = h.shape
    f = wd.shape[0]
    tm, tf = _tile(s, 1024), _tile(f, 256)
    nf = f // tf
    halves = 2 if tm % (2 * FFN_RB) == 0 else 1
    hm = tm // halves
    carry = carry or _NoCarry()
    grid = (s // tm, nf)

    def body(*refs):
        ((h_ref, g_ref, wg_ref, wu_ref, wd_ref, do_ref), c_in,
         (dh_ref, u_ref, dob_ref, act_ref, da_ref, db_ref, dg_ref), c_out,
         (du_acc, a_sc, b_sc, c_sc), c_sem) = carry.split(refs, 6, 7, 4)
        i, j = pl.program_id(0), pl.program_id(1)
        carry.at_first_step(grid, c_in, c_out, c_sem)

        @pl.when(j == 0)
        def _():
            xv = h_ref[...]
            u_ref[...] = (xv * _rinv(xv) * g_ref[...]).astype(BF16)
            dob_ref[...] = (0.5 * do_ref[...]).astype(BF16)
            du_acc[...] = jnp.zeros((tm, d), F32)

        @pl.when((i == 0) & (j == 0))
        def _():
            dg_ref[...] = jnp.zeros((1, d), F32)

        wg, wu = wg_ref[...], wu_ref[...]
        for part in range(halves):
            prow = pl.ds(part * hm, hm)
            u = u_ref[prow, :]
            a_sc[prow, :] = _dot(u, wg, NT)
            b_sc[prow, :] = _dot(u, wu, NT)
            c_sc[prow, :] = _dot(dob_ref[prow, :], wd_ref[...], NT)
            for r in range(hm // FFN_RB):
                rows = pl.ds(part * hm + r * FFN_RB, FFN_RB)
                for t in range(tf // LANES):
                    cols = slice(t * LANES, (t + 1) * LANES)
                    a, b, dact = a_sc[rows, cols], b_sc[rows, cols], c_sc[rows, cols]
                    sig = _sigmoid(a)
                    sa = a * sig
                    act_ref[rows, cols] = (sa * b).astype(BF16)
                    db_ref[rows, cols] = (dact * sa).astype(BF16)
                    da_ref[rows, cols] = (dact * b * (sig * (1.0 + a * (1.0 - sig)))).astype(BF16)
            du_acc[prow, :] += _dot(da_ref[prow, :], wg, NN) + _dot(db_ref[prow, :], wu, NN)

        @pl.when(j == nf - 1)
        def _():
            dx, dgain = _rms_bwd_math(h_ref[...], g_ref[...], du_acc[...])
            dh_ref[...] = do_ref[...] + dx
            dg_ref[...] += dgain

        carry.at_last_step(grid, c_in, c_out, c_sem)

    row = pl.BlockSpec((tm, d), lambda i, j: (i, 0))
    vec = pl.BlockSpec((1, d), lambda i, j: (0, 0))
    wsp = pl.BlockSpec((tf, d), lambda i, j: (j, 0))
    hid = pl.BlockSpec((tm, tf), lambda i, j: (i, j))
    sd = lambda dt: jax.ShapeDtypeStruct((s, d), dt)
    sf = jax.ShapeDtypeStruct((s, f), BF16)
    outs = pl.pallas_call(
        body, name=name, grid=grid,
        in_specs=[row, vec, wsp, wsp, wsp, row] + carry.in_specs(),
        out_specs=[row, row, row, hid, hid, hid, vec] + carry.out_specs(),
        out_shape=[sd(F32), sd(BF16), sd(BF16), sf, sf, sf, jax.ShapeDtypeStruct((1, d), F32)] + carry.out_shapes(),
        scratch_shapes=[pltpu.VMEM((tm, d), F32)] + [pltpu.VMEM((tm, tf), F32)] * 3 + carry.scratch(),
        compiler_params=pltpu.CompilerParams(dimension_semantics=("arbitrary", "arbitrary"),
                                             vmem_limit_bytes=BIG_VMEM),
    )(h, gain, wgT, wuT, wd, dout, *carry.arrays)
    return outs[:7], list(outs[7:])


def pool_fwd(name, h, gain, pw, scale, layer):
    s, d = h.shape
    tm = _tile(s, 512)

    def body(h_ref, g_ref, pw_ref, sc_ref, o_ref, y_ref, ext):
        i = pl.program_id(0)

        @pl.when(i == 0)
        def _():
            ext[0:HALO, :] = jnp.zeros((HALO, d), F32)

        xv = h_ref[...]
        ext[HALO:HALO + tm, :] = xv * _rinv(xv) * g_ref[...]
        pos = i * tm + lax.broadcasted_iota(jnp.int32, (tm, 1), 0)
        for g, w in enumerate(POOL_WINDOWS):
            lo, hi = g * PG, (g + 1) * PG
            ug = ext[HALO:HALO + tm, lo:hi]
            tot = ug
            for k in range(1, w):
                tot = tot + ext[HALO - k:HALO - k + tm, lo:hi]
            cnt = jnp.minimum(pos + 1, w).astype(F32)
            yb = (tot / cnt - ug).astype(BF16)
            y_ref[:, lo:hi] = yb
            o_ref[:, lo:hi] = xv[:, lo:hi] + _dot(yb, pw_ref[g], NN) * sc_ref[:, lo:hi]
        ext[0:HALO, :] = ext[tm:tm + HALO, :]

    row = pl.BlockSpec((tm, d), lambda i: (i, 0))
    vec = pl.BlockSpec((1, d), lambda i: (0, 0))
    return pl.pallas_call(
        body, name=name, grid=(s // tm,),
        in_specs=[row, vec, pl.BlockSpec((None, 4, PG, PG), lambda i: (layer, 0, 0, 0)), vec],
        out_specs=[row, row],
        out_shape=[jax.ShapeDtypeStruct((s, d), F32), jax.ShapeDtypeStruct((s, d), BF16)],
        scratch_shapes=[pltpu.VMEM((HALO + tm, d), F32)],
        compiler_params=_cp("arbitrary"),
    )(h, gain, pw, scale)


def pool_bwd(name, h, gain, y, pw, scale, layer, dout):
    s, d = h.shape
    tm = _tile(s, 512)
    ns = s // tm

    def body(h_ref, g_ref, y_ref, pw_ref, sc_ref, do_ref, dh_ref, dz_ref, dg_ref, dsc_ref, ext, du_sc):
        i = pl.program_id(0)
        t = ns - 1 - i

        @pl.when(i == 0)
        def _():
            ext[tm:tm + HALO, :] = jnp.zeros((HALO, d), F32)
            dg_ref[...] = jnp.zeros((1, d), F32)
            dsc_ref[...] = jnp.zeros((1, d), F32)

        pos = t * tm + lax.broadcasted_iota(jnp.int32, (tm, 1), 0)
        for g, w in enumerate(POOL_WINDOWS):
            lo, hi = g * PG, (g + 1) * PG
            dog = do_ref[:, lo:hi]
            z = _dot(y_ref[:, lo:hi], pw_ref[g], NN)
            dsc_ref[:, lo:hi] += jnp.sum(dog * z, axis=0, keepdims=True)
            dzb = (dog * sc_ref[:, lo:hi]).astype(BF16)
            dz_ref[:, lo:hi] = dzb
            dy = _dot(dzb, pw_ref[g], NT)
            cnt = jnp.minimum(pos + 1, w).astype(F32)
            ext[0:tm, lo:hi] = dy / cnt
            tot = ext[0:tm, lo:hi]
            for k in range(1, w):
                tot = tot + ext[k:k + tm, lo:hi]
            du_sc[:, lo:hi] = tot - dy
        ext[tm:tm + HALO, :] = ext[0:HALO, :]
        dx, dgain = _rms_bwd_math(h_ref[...], g_ref[...], du_sc[...])
        dh_ref[...] = do_ref[...] + dx
        dg_ref[...] += dgain

    row = pl.BlockSpec((tm, d), lambda i: (ns - 1 - i, 0))
    vec = pl.BlockSpec((1, d), lambda i: (0, 0))
    return pl.pallas_call(
        body, name=name, grid=(ns,),
        in_specs=[row, vec, row, pl.BlockSpec((None, 4, PG, PG), lambda i: (layer, 0, 0, 0)), vec, row],
        out_specs=[row, row, vec, vec],
        out_shape=[jax.ShapeDtypeStruct((s, d), F32), jax.ShapeDtypeStruct((s, d), BF16),
                   jax.ShapeDtypeStruct((1, d), F32), jax.ShapeDtypeStruct((1, d), F32)],
        scratch_shapes=[pltpu.VMEM((tm + HALO, d), F32), pltpu.VMEM((tm, d), F32)],
        compiler_params=_cp("arbitrary"),
    )(h, gain, y, pw, scale, dout)


def _rope128(raw_hi, cs):
    b = raw_hi * cs
    r = b + pltpu.roll(b, DR, 1)
    lane = lax.broadcasted_iota(jnp.int32, r.shape, 1)
    return jnp.where(lane < DR, r, 0.0)


def _rope128_bwd(t, cs):
    return (t + pltpu.roll(t, DR, 1)) * cs


def q_heads(name, cq, w256, cs):
    s = cq.shape[0]
    nh = w256.shape[0]
    tm = _tile(s, 1024)

    def body(cq_ref, w_ref, cs_ref, o_ref):
        r = _dot(cq_ref[...], w_ref[...], NN) * Q_PRESCALE
        o_ref[:, 0:DN] = r[:, 0:DN].astype(BF16)
        o_ref[:, DN:HD] = _rope128(r[:, DN:HD], cs_ref[...]).astype(BF16)

    return pl.pallas_call(
        body, name=name, grid=(s // tm, nh),
        in_specs=[pl.BlockSpec((tm, HD), lambda i, hh: (i, 0)),
                  pl.BlockSpec((None, HD, HD), lambda i, hh: (hh, 0, 0)),
                  pl.BlockSpec((tm, DN), lambda i, hh: (i, 0))],
        out_specs=pl.BlockSpec((None, tm, HD), lambda i, hh: (hh, i, 0)),
        out_shape=jax.ShapeDtypeStruct((nh, s, HD), BF16),
        compiler_params=_cp("parallel", "parallel"),
    )(cq, w256, cs)


def kv_post(name, kv_raw, gain, cs):
    s = kv_raw.shape[0]
    tm = _tile(s, 1024)

    def body(x_ref, g_ref, cs_ref, c_ref, kr_ref):
        c = x_ref[:, 0:DN]
        c_ref[...] = (c * _rinv(c) * g_ref[...]).astype(BF16)
        kr_ref[...] = _rope128(x_ref[:, DN:HD], cs_ref[...]).astype(BF16)

    half = pl.BlockSpec((tm, DN), lambda i: (i, 0))
    return pl.pallas_call(
        body, name=name, grid=(s // tm,),
        in_specs=[pl.BlockSpec((tm, HD), lambda i: (i, 0)), pl.BlockSpec((1, DN), lambda i: (0, 0)), half],
        out_specs=[half, half],
        out_shape=[jax.ShapeDtypeStruct((s, DN), BF16), jax.ShapeDtypeStruct((s, DN), BF16)],
        compiler_params=_cp("parallel"),
    )(kv_raw, gain, cs)


def kv_heads(name, c_kv, kr, wuk, wuv):
    s = c_kv.shape[0]
    nh = wuk.shape[0]
    tm = _tile(s, 1024)

    def body(c_ref, kr_ref, wk_ref, wv_ref, k_ref, v_ref):
        c = c_ref[...]
        k_ref[:, 0:DN] = _dot(c, wk_ref[...], NN).astype(BF16)
        k_ref[:, DN:HD] = kr_ref[...]
        v_ref[:, 0:DN] = _dot(c, wv_ref[...], NN).astype(BF16)
        v_ref[:, DN:HD] = jnp.ones((tm, DN), BF16)

    half = pl.BlockSpec((tm, DN), lambda i, hh: (i, 0))
    wsp = pl.BlockSpec((None, DN, DN), lambda i, hh: (hh, 0, 0))
    head = pl.BlockSpec((None, tm, HD), lambda i, hh: (hh, i, 0))
    return pl.pallas_call(
        body, name=name, grid=(s // tm, nh),
        in_specs=[half, half, wsp, wsp], out_specs=[head, head],
        out_shape=[jax.ShapeDtypeStruct((nh, s, HD), BF16), jax.ShapeDtypeStruct((nh, s, HD), BF16)],
        compiler_params=_cp("parallel", "parallel"),
    )(c_kv, kr, wuk, wuv)


def kv_post_bwd(name, kv_raw, gain, dc, dk, cs):
    s = kv_raw.shape[0]
    nh = dk.shape[0]
    tm = _tile(s, 1024)

    def body(x_ref, g_ref, dc_ref, dk_ref, cs_ref, o_ref, dg_ref, acc):
        i, hh = pl.program_id(0), pl.program_id(1)

        @pl.when(hh == 0)
        def _():
            acc[...] = jnp.zeros((tm, DN), F32)

        @pl.when((i == 0) & (hh == 0))
        def _():
            dg_ref[...] = jnp.zeros((1, DN), F32)

        acc[...] += dk_ref[...]

        @pl.when(hh == nh - 1)
        def _():
            dx, dgain = _rms_bwd_math(x_ref[:, 0:DN], g_ref[...], dc_ref[...])
            o_ref[:, 0:DN] = dx.astype(BF16)
            o_ref[:, DN:HD] = _rope128_bwd(acc[...], cs_ref[...]).astype(BF16)
            dg_ref[...] += dgain

    half = pl.BlockSpec((tm, DN), lambda i, hh: (i, 0))
    vec = pl.BlockSpec((1, DN), lambda i, hh: (0, 0))
    full = pl.BlockSpec((tm, HD), lambda i, hh: (i, 0))
    return pl.pallas_call(
        body, name=name, grid=(s // tm, nh),
        in_specs=[full, vec, half, pl.BlockSpec((None, tm, DN), lambda i, hh: (hh, i, 1)), half],
        out_specs=[full, vec],
        out_shape=[jax.ShapeDtypeStruct((s, HD), BF16), jax.ShapeDtypeStruct((1, DN), F32)],
        scratch_shapes=[pltpu.VMEM((tm, DN), F32)],
        compiler_params=_cp("arbitrary", "arbitrary"),
    )(kv_raw, gain, dc, dk, cs)


TQ = 1024
TC = 512
RB = 64


def _flash_cp():
    return pltpu.CompilerParams(dimension_semantics=("parallel", "arbitrary"), vmem_limit_bytes=BIG_VMEM)


def _causal(x, row0, col0, keys_on_rows=False):
    r = row0 + lax.broadcasted_iota(jnp.int32, x.shape, 0)
    c = col0 + lax.broadcasted_iota(jnp.int32, x.shape, 1)
    return jnp.where((r <= c) if keys_on_rows else (c <= r), x, NEG)


def _chunk_pipeline(n_pairs, chunk_of, scores, finish, bufs):
    b0, b1 = bufs
    scores(chunk_of(0), b0)

    def pair(t, carry):
        m = 2 * t
        scores(chunk_of(m + 1), b1)
        finish(chunk_of(m), b0, False)
        scores(chunk_of(m + 2), b0)
        finish(chunk_of(m + 1), b1, False)
        return carry

    lax.fori_loop(0, n_pairs, pair, 0)
    m = 2 * n_pairs
    scores(chunk_of(m + 1), b1)
    finish(chunk_of(m), b0, True)
    finish(chunk_of(m + 1), b1, True)


def flash_fwd(name, q, k, v2):
    nh, s, _ = q.shape
    tq, tk = _tile(s, TQ), _tile(s, TC)
    assert tq == 2 * tk
    nt = tk // LANES

    def body(q_ref, k_ref, v_ref, o_ref, lse_ref, s0_sc, s1_sc, p_sc, m_sc, acc):
        i = pl.program_id(1)
        m_sc[...] = jnp.full((tq, LANES), NEG, F32)
        acc[...] = jnp.zeros((tq, HD), F32)

        def scores(j, dst):
            dst[...] = _dot(q_ref[...], k_ref[pl.ds(pl.multiple_of(j * tk, tk), tk), :], NT)

        def softmax_rows(src, r0, j, masked):
            rows = pl.ds(r0, RB)

            def tile(t):
                x = src[rows, t * LANES:(t + 1) * LANES]
                return _causal(x, i * tq + r0, j * tk + t * LANES) if masked else x

            mx = tile(0)
            for t in range(1, nt):
                mx = jnp.maximum(mx, tile(t))
            m_prev = m_sc[rows, :]
            m_new = jnp.maximum(m_prev, jnp.max(mx, axis=-1, keepdims=True))
            alpha = jnp.exp2(m_prev - m_new)
            for t in range(nt):
                p_sc[rows, t * LANES:(t + 1) * LANES] = jnp.exp2(tile(t) - m_new).astype(BF16)
            m_sc[rows, :] = m_new
            acc[rows, 0:DN] = alpha * acc[rows, 0:DN]
            acc[rows, DN:HD] = alpha * acc[rows, DN:HD]

        def finish(j, src, masked):
            for r in range(tq // RB):
                softmax_rows(src, r * RB, j, masked)
            acc[...] += _dot(p_sc[...], v_ref[pl.ds(pl.multiple_of(j * tk, tk), tk), :], NN)

        _chunk_pipeline(i, lambda m: m, scores, finish, (s0_sc, s1_sc))
        l = acc[:, DN:HD]
        o_ref[...] = (acc[:, 0:DN] / l).astype(BF16)
        lse_ref[...] = m_sc[...] + jnp.log2(l)

    whole = pl.BlockSpec((None, s, HD), lambda hh, i: (hh, 0, 0))
    return pl.pallas_call(
        body, name=name, grid=(nh, s // tq),
        in_specs=[pl.BlockSpec((None, tq, HD), lambda hh, i: (hh, i, 0)), whole, whole],
        out_specs=[pl.BlockSpec((tq, DN), lambda hh, i: (i, hh)),
                   pl.BlockSpec((None, tq, DN), lambda hh, i: (hh, i, 0))],
        out_shape=[jax.ShapeDtypeStruct((s, nh * DN), BF16), jax.ShapeDtypeStruct((nh, s, DN), F32)],
        scratch_shapes=[pltpu.VMEM((tq, tk), F32), pltpu.VMEM((tq, tk), F32), pltpu.VMEM((tq, tk), BF16),
                        pltpu.VMEM((tq, LANES), F32), pltpu.VMEM((tq, HD), F32)],
        compiler_params=_flash_cp(),
    )(q, k, v2)


def attn_do2(name, o, do, nh):
    s = o.shape[0]
    tm = _tile(s, 1024)

    def body(o_ref, do_ref, d_ref):
        dov = do_ref[...]
        delta = jnp.sum(o_ref[...].astype(F32) * dov.astype(F32), axis=-1, keepdims=True)
        hi = delta.astype(BF16).astype(F32)
        lo = delta - hi
        lane = lax.broadcasted_iota(jnp.int32, (tm, DN), 1)
        d_ref[:, 0:DN] = dov
        d_ref[:, DN:HD] = jnp.where(lane == 0, -hi, jnp.where(lane == 1, -lo, 0.0)).astype(BF16)

    blk = pl.BlockSpec((tm, DN), lambda hh, i: (i, hh))
    return pl.pallas_call(
        body, name=name, grid=(nh, s // tm), in_specs=[blk, blk],
        out_specs=pl.BlockSpec((None, tm, HD), lambda hh, i: (hh, i, 0)),
        out_shape=jax.ShapeDtypeStruct((nh, s, HD), BF16), compiler_params=_cp("parallel", "parallel"),
    )(o, do)


def dq_post(name, dq_acc, cs):
    nh, s, _ = dq_acc.shape
    tm = _tile(s, 1024)

    def body(a_ref, cs_ref, o_ref):
        o_ref[:, 0:DN] = (a_ref[:, 0:DN] * ATT_SCALE).astype(BF16)
        o_ref[:, DN:HD] = _rope128_bwd(a_ref[:, DN:HD] * ATT_SCALE, cs_ref[...]).astype(BF16)

    tile = pl.BlockSpec((None, tm, HD), lambda i, hh: (hh, i, 0))
    return pl.pallas_call(
        body, name=name, grid=(s // tm, nh),
        in_specs=[tile, pl.BlockSpec((tm, DN), lambda i, hh: (i, 0))], out_specs=tile,
        out_shape=jax.ShapeDtypeStruct((nh, s, HD), BF16), compiler_params=_cp("parallel", "parallel"),
    )(dq_acc, cs)


def flash_bwd(name, q, k, v2, do2, lse_row, prev=None):
    nh, s, _ = q.shape
    tk, tq = _tile(s, TQ), _tile(s, TC)
    nqc = s // tq
    per = tk // tq
    nt = tq // LANES
    nkt = s // tk
    has_prev = prev is not None

    def body(*refs):
        k_ref, v_ref, q_ref, do_ref, lse_ref = refs[:5]
        dk_ref, dv_ref, dq_hbm, s_sc, d_sc, p_sc, ds_sc, dk_acc, dv_acc, dq_sc, sem = refs[-11:]
        hh, i = pl.program_id(0), pl.program_id(1)
        dk_acc[...] = jnp.zeros((tk, HD), F32)
        dv_acc[...] = jnp.zeros((tk, DN), F32)

        @pl.when(i == 0)
        def _():
            dq_sc[...] = jnp.zeros((s, HD), F32)

        def p_rows(r0, j, masked):
            rows = pl.ds(r0, RB)
            for t in range(nt):
                cols = slice(t * LANES, (t + 1) * LANES)
                x = s_sc[rows, cols]
                if masked:
                    x = _causal(x, i * tk + r0, j * tq + t * LANES, keys_on_rows=True)
                p = jnp.exp2(x - jnp.tile(lse_ref[j][:, cols], (RB // 8, 1)))
                p_sc[rows, cols] = p.astype(BF16)
                ds_sc[rows, cols] = (p * d_sc[rows, cols]).astype(BF16)

        def step(j, masked):
            start = pl.multiple_of(j * tq, tq)
            qj = q_ref[pl.ds(start, tq), :]
            s_sc[...] = _dot(k_ref[...], qj, NT)
            d_sc[...] = _dot(v_ref[...], do_ref[pl.ds(start, tq), :], NT)
            for r in range(tk // RB):
                p_rows(r * RB, j, masked)
            dv_acc[...] += _dot(p_sc[...], do_ref[pl.ds(start, tq), 0:DN], NN)
            ds = ds_sc[...]
            dk_acc[...] += _dot(ds, qj, NN)
            dq_sc[pl.ds(start, tq), :] += _dot(ds, k_ref[...], TN)

        def full_step(m, carry):
            step(nqc - 1 - m, False)
            return carry

        lax.fori_loop(0, nqc - (i + 1) * per, full_step, 0)
        for dd in reversed(range(per)):
            step(i * per + dd, True)
        if has_prev:
            dk_ref[...] = dk_acc[...] * LN2 + refs[5][...]
            dv_ref[...] = dv_acc[...] + refs[6][...]
        else:
            dk_ref[...] = dk_acc[...] * LN2
            dv_ref[...] = dv_acc[...]

        @pl.when(i == nkt - 1)
        def _():
            cp = pltpu.make_async_copy(dq_sc, dq_hbm.at[hh], sem)
            cp.start()
            cp.wait()

    kblk = pl.BlockSpec((None, tk, HD), lambda hh, i: (hh, i, 0))
    vblk = pl.BlockSpec((None, tk, DN), lambda hh, i: (hh, i, 0))
    whole = pl.BlockSpec((None, s, HD), lambda hh, i: (hh, 0, 0), pipeline_mode=pl.Buffered(1))
    in_specs = [kblk, kblk, whole, whole, pl.BlockSpec((None, nqc, 8, tq), lambda hh, i: (hh, 0, 0, 0))]
    args = [k, v2, q, do2, lse_row]
    if has_prev:
        in_specs += [kblk, vblk]
        args += list(prev)
    big = jax.ShapeDtypeStruct((nh, s, HD), F32)
    return pl.pallas_call(
        body, name=name, grid=(nh, nkt), in_specs=in_specs, out_specs=[kblk, vblk, ANY],
        out_shape=[big, jax.ShapeDtypeStruct((nh, s, DN), F32), big],
        scratch_shapes=[pltpu.VMEM((tk, tq), F32)] * 2 + [pltpu.VMEM((tk, tq), BF16)] * 2
        + [pltpu.VMEM((tk, HD), F32), pltpu.VMEM((tk, DN), F32), pltpu.VMEM((s, HD), F32), pltpu.SemaphoreType.DMA],
        compiler_params=pltpu.CompilerParams(dimension_semantics=("arbitrary", "arbitrary"), vmem_limit_bytes=BIG_VMEM),
    )(*args)


def loss_head(name, h, gain, target):
    s, d = h.shape
    tm = _tile(s, 1024)

    def body(h_ref, g_ref, t_ref, sq_ref, dh_ref, dg_ref):
        @pl.when(pl.program_id(0) == 0)
        def _():
            sq_ref[...] = jnp.zeros((1, d), F32)
            dg_ref[...] = jnp.zeros((1, d), F32)

        xv = h_ref[...]
        g = g_ref[...]
        rinv = _rinv(xv)
        xhat = xv * rinv
        err = xhat * g - t_ref[...]
        sq_ref[...] += jnp.sum(err * err, axis=0, keepdims=True)
        dy = err / d
        dg_ref[...] += jnp.sum(dy * xhat, axis=0, keepdims=True)
        dxh = dy * g
        dh_ref[...] = rinv * (dxh - xhat * jnp.mean(dxh * xhat, axis=-1, keepdims=True))

    row = pl.BlockSpec((tm, d), lambda i: (i, 0))
    vec = pl.BlockSpec((1, d), lambda i: (0, 0))
    return pl.pallas_call(
        body, name=name, grid=(s // tm,), in_specs=[row, vec, row], out_specs=[vec, row, vec],
        out_shape=[jax.ShapeDtypeStruct((1, d), F32), jax.ShapeDtypeStruct((s, d), F32),
                   jax.ShapeDtypeStruct((1, d), F32)],
        compiler_params=_cp("arbitrary"),
    )(h, gain, target)


def _adamw_math(w, g, m, v):
    m2 = ADAM_B1 * m + (1.0 - ADAM_B1) * g
    v2 = ADAM_B2 * v + (1.0 - ADAM_B2) * (g * g)
    mh = m2 / (1.0 - ADAM_B1 ** ADAM_STEP)
    vh = v2 / (1.0 - ADAM_B2 ** ADAM_STEP)
    delta = -ADAM_LR * (mh / (jnp.sqrt(vh) + ADAM_EPS) + ADAM_WD * w)
    return delta, m2, v2


def adamw(name, parts, w, m, v):
    rows, cols = w.shape
    tr = _tile(rows, max(8, (1 << 18) // cols))
    stacked = parts[0].ndim == 3
    npart = len(parts)

    def body(*refs):
        p_refs = refs[:npart]
        w_ref, m_ref, v_ref, g_ref, d_ref, m2_ref, v2_ref = refs[npart:]
        if stacked:
            g = p_refs[0][0]
            for n in range(1, parts[0].shape[0]):
                g = g + p_refs[0][n]
        else:
            g = p_refs[0][...]
            for r in p_refs[1:]:
                g = g + r[...]
        delta, m2, v2 = _adamw_math(w_ref[...], g, m_ref[...], v_ref[...])
        g_ref[...] = g
        d_ref[...] = delta
        m2_ref[...] = m2
        v2_ref[...] = v2

    blk = pl.BlockSpec((tr, cols), lambda i: (i, 0))
    pblk = pl.BlockSpec((parts[0].shape[0], tr, cols), lambda i: (0, i, 0)) if stacked else blk
    sds = jax.ShapeDtypeStruct((rows, cols), F32)
    return pl.pallas_call(
        body, name=name, grid=(rows // tr,), in_specs=[pblk] * npart + [blk] * 3,
        out_specs=[blk] * 4, out_shape=[sds] * 4, compiler_params=_cp("parallel"),
    )(*parts, w, m, v)


def sum4(name, x):
    _, a, r, c = x.shape
    tr = _tile(r, max(8, (1 << 18) // c))

    def body(x_ref, o_ref):
        o_ref[...] = ((x_ref[3].astype(F32) + x_ref[0].astype(F32)) + x_ref[1].astype(F32)) + x_ref[2].astype(F32)

    return pl.pallas_call(
        body, name=name, grid=(a, r // tr),
        in_specs=[pl.BlockSpec((4, None, tr, c), lambda i, j: (0, i, j, 0))],
        out_specs=pl.BlockSpec((None, tr, c), lambda i, j: (i, j, 0)),
        out_shape=jax.ShapeDtypeStruct((a, r, c), F32), compiler_params=_cp("parallel", "parallel"),
    )(x)


ANY = pl.BlockSpec(memory_space=pl.ANY)


def _mesh_pos():
    return lax.axis_index("x"), lax.axis_index("y"), lax.axis_index("c")


def _chip_peers(x, y):
    return [(1 - x, y), (x, 1 - y), (1 - x, 1 - y)]


class _NoCarry:
    arrays = ()

    def split(self, refs, n_in, n_out, n_scratch):
        return (refs[:n_in], (), refs[n_in:n_in + n_out], (), refs[n_in + n_out:n_in + n_out + n_scratch], ())

    def in_specs(self):
        return []

    out_specs = out_shapes = scratch = in_specs

    def at_first_step(self, grid, ins, outs, sems):
        pass

    at_last_step = at_first_step


class ChipCopies:
    def __init__(self, kind, arrays):
        assert kind in ("gather", "exchange")
        self.kind, self.arrays, self.n = kind, tuple(arrays), len(arrays)

    def split(self, refs, n_in, n_out, n_scratch):
        n, pos, parts = self.n, 0, []
        for count in (n_in, n, n_out, n, n_scratch, 3):
            parts.append(refs[pos:pos + count])
            pos += count
        return parts

    def in_specs(self):
        return [ANY] * self.n

    out_specs = in_specs

    def out_shapes(self):
        if self.kind == "gather":
            return [jax.ShapeDtypeStruct((a.shape[0], N_CHIPS) + a.shape[1:], a.dtype) for a in self.arrays]
        return [jax.ShapeDtypeStruct((N_CHIPS, a.shape[0]) + a.shape[2:], a.dtype) for a in self.arrays]

    def scratch(self):
        n = self.n
        return [pltpu.SemaphoreType.DMA((3 * n,)), pltpu.SemaphoreType.DMA((3 * n,)), pltpu.SemaphoreType.DMA((n,))]

    def _copies(self, ins, outs, sems):
        send_sems, recv_sems, loc_sems = sems
        x, y, c = _mesh_pos()
        me = 2 * x + y
        peers = _chip_peers(x, y)
        gather = self.kind == "gather"

        def remote(a, j, block):
            px, py = peers[j]
            src = ins[a] if gather else ins[a].at[:, 2 * px + py]
            dst = outs[a].at[:, block] if gather else outs[a].at[j]
            return pltpu.make_async_remote_copy(
                src_ref=src, dst_ref=dst, send_sem=send_sems.at[a * 3 + j], recv_sem=recv_sems.at[a * 3 + j],
                device_id=(px, py, c), device_id_type=MESH)

        pairs = [(a, j) for a in range(self.n) for j in range(3)]
        if gather:
            local = [pltpu.make_async_copy(ins[a], outs[a].at[:, me], loc_sems.at[a]) for a in range(self.n)]
        else:
            local = [pltpu.make_async_copy(ins[a].at[:, me], outs[a].at[3], loc_sems.at[a]) for a in range(self.n)]
        sends = [remote(a, j, me) for a, j in pairs]
        recvs = [remote(a, j, 2 * peers[j][0] + peers[j][1]) for a, j in pairs]
        return local, sends, recvs

    def start(self, ins, outs, sems):
        local, sends, _ = self._copies(ins, outs, sems)
        for cp in local + sends:
            cp.start()

    def wait(self, ins, outs, sems):
        local, sends, recvs = self._copies(ins, outs, sems)
        for cp in recvs:
            cp.wait_recv()
        for cp in sends:
            cp.wait_send()
        for cp in local:
            cp.wait()

    @staticmethod
    def _is_step(grid, last):
        cond = None
        for ax, size in enumerate(grid):
            c = pl.program_id(ax) == (size - 1 if last else 0)
            cond = c if cond is None else cond & c
        return cond

    def at_first_step(self, grid, ins, outs, sems):
        @pl.when(self._is_step(grid, False))
        def _():
            self.start(ins, outs, sems)

    def at_last_step(self, grid, ins, outs, sems):
        @pl.when(self._is_step(grid, True))
        def _():
            self.wait(ins, outs, sems)


def chip_copies(name, copies):
    n = copies.n

    def body(*refs):
        ins, outs, sems = refs[:n], refs[n:2 * n], refs[2 * n:]
        copies.start(ins, outs, sems)
        copies.wait(ins, outs, sems)

    return pl.pallas_call(
        body, name=name, in_specs=copies.in_specs(), out_specs=copies.out_specs(),
        out_shape=copies.out_shapes(), scratch_shapes=copies.scratch(),
    )(*copies.arrays)


def sibling_exchange(name, arrs):
    n = len(arrs)

    def body(*refs):
        ins, outs = refs[:n], refs[n:2 * n]
        send_sems, recv_sems = refs[2 * n:]
        x, y, c = _mesh_pos()
        copies = [pltpu.make_async_remote_copy(
            src_ref=ins[a], dst_ref=outs[a], send_sem=send_sems.at[a], recv_sem=recv_sems.at[a],
            device_id=(x, y, 1 - c), device_id_type=MESH) for a in range(n)]
        for cp in copies:
            cp.start()
        for cp in copies:
            cp.wait_recv()
        for cp in copies:
            cp.wait_send()

    return pl.pallas_call(
        body, name=name, in_specs=[ANY] * n, out_specs=[ANY] * n,
        out_shape=[jax.ShapeDtypeStruct(a.shape, a.dtype) for a in arrs],
        scratch_shapes=[pltpu.SemaphoreType.DMA((n,)), pltpu.SemaphoreType.DMA((n,))],
    )(*arrs)


def all_gather8(name, buf):
    r, cdim = buf.shape

    def body(in_ref, out_ref, send_sems, recv_sems, loc_sem):
        x, y, c = _mesh_pos()
        flips = [(fx, fy, fc) for fx in (0, 1) for fy in (0, 1) for fc in (0, 1)][1:]

        def peer(f):
            return tuple((1 - p) if b else p for p, b in zip((x, y, c), f))

        def remote(j, slot):
            return pltpu.make_async_remote_copy(
                src_ref=in_ref, dst_ref=out_ref.at[slot], send_sem=send_sems.at[j], recv_sem=recv_sems.at[j],
                device_id=peer(flips[j]), device_id_type=MESH)

        me = 4 * x + 2 * y + c
        local = pltpu.make_async_copy(in_ref, out_ref.at[me], loc_sem)
        local.start()
        sends = [remote(j, me) for j in range(7)]
        for cp in sends:
            cp.start()
        for j in range(7):
            px, py, pc = peer(flips[j])
            remote(j, 4 * px + 2 * py + pc).wait_recv()
        for cp in sends:
            cp.wait_send()
        local.wait()

    return pl.pallas_call(
        body, name=name, in_specs=[ANY], out_specs=ANY,
        out_shape=jax.ShapeDtypeStruct((8, r, cdim), buf.dtype),
        scratch_shapes=[pltpu.SemaphoreType.DMA((7,)), pltpu.SemaphoreType.DMA((7,)), pltpu.SemaphoreType.DMA],
    )(buf)


def _rot_cols(w):
    half = w.shape[-1] // 2
    return jnp.concatenate([-w[..., half:], w[..., :half]], axis=-1)


def _fold_rot(g):
    rot = g[..., DN + DR:]
    half = DR // 2
    return jnp.concatenate([g[..., :DN], g[..., DN:DN + DR] + jnp.concatenate([rot[..., half:], -rot[..., :half]], -1)], -1)


def _with_rot(w):
    return jnp.concatenate([w, _rot_cols(w[..., DN:])], axis=-1)


def _rope_table(s):
    pos = jnp.arange(s, dtype=F32)
    inv_freq = ROPE_THETA ** (-jnp.arange(0, DR, 2, dtype=F32) / DR)
    ang = pos[:, None] * inv_freq[None, :]
    cos, sin = jnp.cos(ang), jnp.sin(ang)
    return jnp.concatenate([cos, cos, sin, sin], axis=-1)


def _row_stats(x, tq):
    nh, s, _ = x.shape
    return jnp.broadcast_to(x[:, :, 0].reshape(nh, s // tq, 1, tq), (nh, s // tq, 8, tq))


REPL_NAMES = ["ffn_pre_norm", "mix_norm", "ffn_post_norm", "kv_in_norm", "final_norm", "ckv_norm",
              "q_lora_norm", "pool_scale_full", "w_uk", "w_uv"]
PACK_COLS = 1024


def _pack_rows(arrs):
    rows, counts = [], []
    for a in arrs:
        flat = a.reshape(-1).astype(F32)
        n = -(-flat.shape[0] // PACK_COLS)
        flat = jnp.pad(flat, (0, n * PACK_COLS - flat.shape[0]))
        rows.append(flat.reshape(n, PACK_COLS))
        counts.append(n)
    total = sum(counts)
    pad = -total % 8
    if pad:
        rows.append(jnp.zeros((pad, PACK_COLS), F32))
    return jnp.concatenate(rows, axis=0), counts


def _unpack_rows(buf, counts, shapes):
    out, r = [], 0
    for n, shp in zip(counts, shapes):
        size = 1
        for dd in shp:
            size *= dd
        out.append(buf[r:r + n].reshape(-1)[:size].reshape(shp))
        r += n
    return out


def kernel(x, ffn_pre_norm, ffn_pre_wg, ffn_pre_wu, ffn_pre_wd, mix_norm, ffn_post_norm, ffn_post_wg, ffn_post_wu, ffn_post_wd, pool_w, pool_scale, kv_in_norm, w_dkv, ckv_norm, w_uk, w_uv, q_lora_norm, w_dq, w_uq, w_o, final_norm, loss_target, m_ffn_pre_norm, m_ffn_pre_wg, m_ffn_pre_wu, m_ffn_pre_wd, m_mix_norm, m_ffn_post_norm, m_ffn_post_wg, m_ffn_post_wu, m_ffn_post_wd, m_pool_w, m_pool_scale, m_kv_in_norm, m_w_dkv, m_ckv_norm, m_w_uk, m_w_uv, m_q_lora_norm, m_w_dq, m_w_uq, m_w_o, m_final_norm, v_ffn_pre_norm, v_ffn_pre_wg, v_ffn_pre_wu, v_ffn_pre_wd, v_mix_norm, v_ffn_post_norm, v_ffn_post_wg, v_ffn_post_wu, v_ffn_post_wd, v_pool_w, v_pool_scale, v_kv_in_norm, v_w_dkv, v_ckv_norm, v_w_uk, v_w_uv, v_q_lora_norm, v_w_dq, v_w_uq, v_w_o, v_final_norm):
    weights = dict(ffn_pre_norm=ffn_pre_norm, ffn_pre_wg=ffn_pre_wg, ffn_pre_wu=ffn_pre_wu, ffn_pre_wd=ffn_pre_wd,
                   mix_norm=mix_norm, ffn_post_norm=ffn_post_norm, ffn_post_wg=ffn_post_wg, ffn_post_wu=ffn_post_wu,
                   ffn_post_wd=ffn_post_wd, pool_w=pool_w, pool_scale=pool_scale, kv_in_norm=kv_in_norm, w_dkv=w_dkv,
                   ckv_norm=ckv_norm, w_uk=w_uk, w_uv=w_uv, q_lora_norm=q_lora_norm, w_dq=w_dq, w_uq=w_uq, w_o=w_o,
                   final_norm=final_norm)
    mom1 = dict(ffn_pre_norm=m_ffn_pre_norm, ffn_pre_wg=m_ffn_pre_wg, ffn_pre_wu=m_ffn_pre_wu, ffn_pre_wd=m_ffn_pre_wd,
                mix_norm=m_mix_norm, ffn_post_norm=m_ffn_post_norm, ffn_post_wg=m_ffn_post_wg, ffn_post_wu=m_ffn_post_wu,
                ffn_post_wd=m_ffn_post_wd, pool_w=m_pool_w, pool_scale=m_pool_scale, kv_in_norm=m_kv_in_norm,
                w_dkv=m_w_dkv, ckv_norm=m_ckv_norm, w_uk=m_w_uk, w_uv=m_w_uv, q_lora_norm=m_q_lora_norm, w_dq=m_w_dq,
                w_uq=m_w_uq, w_o=m_w_o, final_norm=m_final_norm)
    mom2 = dict(ffn_pre_norm=v_ffn_pre_norm, ffn_pre_wg=v_ffn_pre_wg, ffn_pre_wu=v_ffn_pre_wu, ffn_pre_wd=v_ffn_pre_wd,
                mix_norm=v_mix_norm, ffn_post_norm=v_ffn_post_norm, ffn_post_wg=v_ffn_post_wg, ffn_post_wu=v_ffn_post_wu,
                ffn_post_wd=v_ffn_post_wd, pool_w=v_pool_w, pool_scale=v_pool_scale, kv_in_norm=v_kv_in_norm,
                w_dkv=v_w_dkv, ckv_norm=v_ckv_norm, w_uk=v_w_uk, w_uv=v_w_uv, q_lora_norm=v_q_lora_norm, w_dq=v_w_dq,
                w_uq=v_w_uq, w_o=v_w_o, final_norm=v_final_norm)
    names = list(weights)

    h0 = x[0]
    target = loss_target[0]
    s, d = h0.shape
    nh = N_HEADS
    nb = DEPTH - N_A
    f_loc = ffn_pre_wd.shape[1]
    ffn_dim = N_CHIPS * f_loc
    my_chip = 2 * lax.axis_index("x") + lax.axis_index("y")

    ffn_in = {"pre": (ffn_pre_norm, ffn_pre_wg, ffn_pre_wu, ffn_pre_wd),
              "post": (ffn_post_norm, ffn_post_wg, ffn_post_wu, ffn_post_wd)}
    uq_loc = jnp.transpose(w_uq, (0, 2, 1, 3))
    shard = {
        "pool_w": pool_w.reshape(N_A * 4, PG // N_CHIPS, PG).astype(BF16),
        "pool_scale": pool_scale.reshape(N_A, 1, PG),
        "wdkv": _with_rot(w_dkv)[None].astype(BF16),
        "wdq": w_dq.astype(BF16),
        "wq": _with_rot(uq_loc).reshape(nb * nh, uq_loc.shape[2], HD).astype(BF16),
        "wo": w_o.astype(BF16),
    }
    for part, (_, wg_, wu_, wd_) in ffn_in.items():
        for l in range(DEPTH):
            shard[f"{part}_wg{l}"] = jnp.transpose(wg_[l]).astype(BF16)[None]
            shard[f"{part}_wu{l}"] = jnp.transpose(wu_[l]).astype(BF16)[None]
            shard[f"{part}_wd{l}"] = wd_[l].astype(BF16)[None]
    ffn_names = lambda part, l: [f"{part}_wg{l}", f"{part}_wu{l}", f"{part}_wd{l}"]
    gather_plan = {
        None: ffn_names("pre", 0),
        ("pre", 0): ["pool_w", "pool_scale"] + ffn_names("post", 0) + ffn_names("pre", 1),
        ("post", 0): ffn_names("post", 1) + ["wdkv"] + ffn_names("pre", 2),
        ("pre", 1): ffn_names("post", 2) + ["wdq", "wq", "wo"] + ffn_names("pre", 3),
        ("post", 1): ffn_names("post", 3),
    }
    full = {}

    def absorb(plan_names, outs):
        for n, g in zip(plan_names, outs):
            full[n] = g.reshape(g.shape[0], N_CHIPS * g.shape[2], g.shape[3])

    absorb(gather_plan[None], chip_copies("gather_first", ChipCopies("gather", [shard[n] for n in gather_plan[None]])))
    wuk_h = jnp.transpose(w_uk, (1, 0, 2)).astype(BF16)
    wuv_h = jnp.transpose(w_uv, (1, 0, 2)).astype(BF16)
    cs = _rope_table(s)

    def ffn_forward(part, l, hin):
        plan_names = gather_plan.get((part, l))
        carry = ChipCopies("gather", [shard[n] for n in plan_names]) if plan_names else None
        wg_, wu_, wd_ = [full[n][0] for n in ffn_names(part, l)]
        hout, carried = ffn_fwd(f"ffn_{part}_fwd{l}", hin, ffn_in[part][0][l:l + 1], wg_, wu_, wd_, carry)
        if plan_names:
            absorb(plan_names, carried)
        return hout

    saved = {}
    h = h0
    kv = None
    for l in range(DEPTH):
        saved["a", l] = h
        h = ffn_forward("pre", l, h)
        saved["b", l] = h
        if l == 0:
            pool_w_full = full["pool_w"].reshape(N_A, 4, PG, PG)
            pool_scale_full = full["pool_scale"].reshape(N_A, d)
        if l < N_A:
            h, y = pool_fwd(f"pool_fwd{l}", h, mix_norm[l:l + 1], pool_w_full, pool_scale_full[l:l + 1], l)
            saved["y", l] = y
        else:
            j = l - N_A
            u = rms_fwd(f"mix_norm_fwd{l}", h, mix_norm[l:l + 1])
            cq_raw = mm_nn(f"q_down{l}", u, wdq_full[j], F32, 2048, 256, 1024)
            cq = rms_fwd(f"q_norm_fwd{l}", cq_raw, q_lora_norm[j:j + 1])
            q = q_heads(f"q_heads{l}", cq, wq256[j], cs)
            o, lse = flash_fwd(f"flash_fwd{l}", q, kv[0], kv[1])
            saved["att", l] = (u, cq_raw, cq, q, o, lse)
            h = mm_nn(f"attn_out{l}", o, wo_full[j], F32, 1024, 1024, 1024, res=h)
        saved["c", l] = h
        h = ffn_forward("post", l, h)
        if l == N_A - 1:
            wdkv256 = full["wdkv"][0]
            wdq_full = full["wdq"]
            wq256 = full["wq"].reshape(nb, nh, HD, HD)
            wo_full = full["wo"]
            u_kv = rms_fwd("kv_in_norm_fwd", h, kv_in_norm[None])
            kv_raw = mm_nn("kv_down", u_kv, wdkv256, F32, 2048, 256, 1024)
            c_kv, kr = kv_post("kv_post", kv_raw, ckv_norm[None], cs)
            kv = kv_heads("kv_heads", c_kv, kr, wuk_h, wuv_h)
            saved["kv"] = (u_kv, kv_raw, c_kv)

    sq, dh, g_final = loss_head("loss_head", h, final_norm[None], target)
    loss = lax.psum(0.5 * jnp.sum(sq) / d, ("x", "y", "c"))

    grads = {}
    gvec = {n: [None] * DEPTH for n in ("ffn_pre_norm", "mix_norm", "ffn_post_norm")}
    g_pool_scale, g_qnorm = [None] * N_A, [None] * nb
    dkv = None
    gfull, partial = {}, {}
    split = lambda g: g.reshape(g.shape[0], N_CHIPS, g.shape[1] // N_CHIPS, g.shape[2])

    def layer_grad_names(l):
        extra = [f"pool_w{l}"] if l < N_A else [f"wdq{l - N_A}", f"wq{l - N_A}", f"wo{l - N_A}"]
        return ffn_names("pre", l) + ffn_names("post", l) + extra + (["wdkv"] if l == N_A - 1 else [])

    def chip_sums(plan_names, received):
        for n, r in zip(plan_names, received):
            partial[n] = sum4(f"chip_sum_{n}", r)

    def ffn_backward(part, l, hin, dh):
        plan_names = layer_grad_names(l + 1) if part == "post" and l + 1 < DEPTH else None
        carry = ChipCopies("exchange", [gfull[n] for n in plan_names]) if plan_names else None
        wg_, wu_, wd_ = [full[n][0] for n in ffn_names(part, l)]
        (dh, u, dob, act, da, db, dgain), received = ffn_bwd(
            f"ffn_{part}_bwd{l}", hin, ffn_in[part][0][l:l + 1], wg_, wu_, wd_, dh, carry)
        if plan_names:
            chip_sums(plan_names, received)
        gvec[f"ffn_{part}_norm"][l] = dgain[0]
        for n, lhs, rhs in zip(ffn_names(part, l), (da, db, act), (u, u, dob)):
            gfull[n] = split(mm_tn(f"d_{n}", lhs, rhs, BF16, ffn_dim // 2, d, 1024)[None])
        return dh

    for l in reversed(range(DEPTH)):
        if l == N_A - 1:
            u_kv, kv_raw, c_kv = saved["kv"]
            dk, dv = dkv
            tm = _tile(s, 2048)
            head_red = dict(grid=(s // tm, 1, nh), b_block=(None, DN, DN), b_map=lambda i, jj, kk: (kk, 0, 0),
                            o_block=(tm, DN), o_map=lambda i, jj, kk: (i, 0), out_shape=(s, DN), dims=NT, out_dtype=F32)
            dc = _mm("kv_dc_k", dk, wuk_h, a_block=(None, tm, DN), a_map=lambda i, jj, kk: (kk, i, 0), **head_red)
            dc = _mm("kv_dc_v", dv, wuv_h, a_block=(None, tm, DN), a_map=lambda i, jj, kk: (kk, i, 0), res=dc, **head_red)
            tk = _tile(s, 2048)
            head_tn = dict(grid=(nh, 1, s // tk), a_block=(tk, DN), a_map=lambda hh, jj, kk: (kk, 0),
                           b_block=(None, tk, DN), b_map=lambda hh, jj, kk: (hh, kk, 0),
                           o_block=(None, DN, DN), o_map=lambda hh, jj, kk: (hh, 0, 0),
                           out_shape=(nh, DN, DN), dims=TN, out_dtype=F32)
            grads["w_uk"] = jnp.transpose(_mm("kv_dwuk", c_kv, dk, **head_tn), (1, 0, 2))
            grads["w_uv"] = jnp.transpose(_mm("kv_dwuv", c_kv, dv, **head_tn), (1, 0, 2))
            dkv_raw, dg_ckv = kv_post_bwd("kv_post_bwd", kv_raw, ckv_norm[None], dc, dk, cs)
            grads["ckv_norm"] = dg_ckv[0]
            du_kv = mm_nt("kv_du", dkv_raw, wdkv256, F32, 2048, 1024, 256)
            gfull["wdkv"] = split(mm_tn("kv_dwdkv", u_kv, dkv_raw, BF16, 1024, 256, 2048)[None])
            dh, dg_kvin = rms_bwd("kv_in_norm_bwd", saved["a", l + 1], kv_in_norm[None], du_kv, dres=dh)
            grads["kv_in_norm"] = dg_kvin[0]

        dh = ffn_backward("post", l, saved["c", l], dh)

        if l < N_A:
            dh, dz, dgain, dscale = pool_bwd(f"pool_bwd{l}", saved["b", l], mix_norm[l:l + 1], saved["y", l],
                                             pool_w_full, pool_scale_full[l:l + 1], l, dh)
            gvec["mix_norm"][l] = dgain[0]
            g_pool_scale[l] = dscale[0]
            tk = _tile(s, 2048)
            gfull[f"pool_w{l}"] = split(_mm(
                f"pool_dw{l}", saved["y", l], dz, grid=(4, 1, s // tk),
                a_block=(tk, PG), a_map=lambda g, jj, kk: (kk, g), b_block=(tk, PG), b_map=lambda g, jj, kk: (kk, g),
                o_block=(None, PG, PG), o_map=lambda g, jj, kk: (g, 0, 0),
                out_shape=(4, PG, PG), dims=TN, out_dtype=BF16))
        else:
            j = l - N_A
            u, cq_raw, cq, q, o, lse = saved["att", l]
            k_, v_ = kv
            do = mm_nt(f"attn_do{l}", dh, wo_full[j], BF16, 1024, 1024, 1024)
            gfull[f"wo{j}"] = split(mm_tn(f"attn_dwo{l}", o, dh, BF16, 1024, 1024, 1024)[None])
            do2 = attn_do2(f"attn_do2_{l}", o, do, nh)
            dk_, dv_, dq_acc = flash_bwd(f"flash_bwd{l}", q, k_, v_, do2, _row_stats(lse, _tile(s, TC)), prev=dkv)
            dkv = (dk_, dv_)
            dq_raw = dq_post(f"dq_post{l}", dq_acc, cs)
            tm = _tile(s, 2048)
            dcq = _mm(f"q_dcq{l}", dq_raw, wq256[j], grid=(s // tm, 1, nh),
                      a_block=(None, tm, HD), a_map=lambda i, jj, kk: (kk, i, 0),
                      b_block=(None, HD, HD), b_map=lambda i, jj, kk: (kk, 0, 0),
                      o_block=(tm, HD), o_map=lambda i, jj, kk: (i, 0),
                      out_shape=(s, HD), dims=NT, out_dtype=F32)
            tk = _tile(s, 2048)
            gfull[f"wq{j}"] = split(_mm(
                f"q_dwq{l}", cq, dq_raw, grid=(nh, 1, s // tk), a_block=(tk, HD), a_map=lambda hh, jj, kk: (kk, 0),
                b_block=(None, tk, HD), b_map=lambda hh, jj, kk: (hh, kk, 0),
                o_block=(None, HD, HD), o_map=lambda hh, jj, kk: (hh, 0, 0),
                out_shape=(nh, HD, HD), dims=TN, out_dtype=BF16))
            dcq_raw, dg_q = rms_bwd(f"q_norm_bwd{l}", cq_raw, q_lora_norm[j:j + 1], dcq)
            g_qnorm[j] = dg_q[0]
            du = mm_nt(f"q_du{l}", dcq_raw, wdq_full[j], F32, 2048, 1024, 256)
            gfull[f"wdq{j}"] = split(mm_tn(f"q_dwdq{l}", u, dcq_raw, BF16, 1024, 256, 2048)[None])
            dh, dgain = rms_bwd(f"mix_norm_bwd{l}", saved["b", l], mix_norm[l:l + 1], du, dres=dh)
            gvec["mix_norm"][l] = dgain[0]

        dh = ffn_backward("pre", l, saved["a", l], dh)

    grad_x = dh[None]

    last_names = layer_grad_names(0)
    chip_sums(last_names, chip_copies("exchange_last", ChipCopies("exchange", [gfull[n] for n in last_names])))
    part_names = sorted(partial)
    sibling = dict(zip(part_names, sibling_exchange("exchange_sibling", [partial[n] for n in part_names])))
    pieces = {
        "pool_w": [f"pool_w{l}" for l in range(N_A)], "w_dkv": ["wdkv"], "w_dq": [f"wdq{j}" for j in range(nb)],
        "w_uq": [f"wq{j}" for j in range(nb)], "w_o": [f"wo{j}" for j in range(nb)],
    }
    for part in ("pre", "post"):
        for m in ("wg", "wu", "wd"):
            pieces[f"ffn_{part}_{m}"] = [f"{part}_{m}{l}" for l in range(DEPTH)]

    def natural(name, src):
        p = jnp.concatenate([src[n] for n in pieces[name]], axis=0)
        if name in ("ffn_pre_wg", "ffn_pre_wu", "ffn_post_wg", "ffn_post_wu"):
            return jnp.transpose(p, (0, 2, 1))
        if name == "w_dkv":
            return _fold_rot(p[0])
        if name == "w_uq":
            return jnp.transpose(_fold_rot(p).reshape(nb, nh, p.shape[1], QK_DIM), (0, 2, 1, 3))
        return p.reshape(weights[name].shape)

    results = {}
    for n in pieces:
        shp = weights[n].shape
        as2d = lambda a: a.reshape(-1, shp[-1])
        outs = adamw(f"adamw_{n}", [as2d(natural(n, partial)), as2d(natural(n, sibling))],
                     as2d(weights[n]), as2d(mom1[n]), as2d(mom2[n]))
        results[n] = [o_.reshape(shp) for o_ in outs]

    grads.update(ffn_pre_norm=jnp.stack(gvec["ffn_pre_norm"]), mix_norm=jnp.stack(gvec["mix_norm"]),
                 ffn_post_norm=jnp.stack(gvec["ffn_post_norm"]), final_norm=g_final[0],
                 q_lora_norm=jnp.stack(g_qnorm), pool_scale_full=jnp.stack(g_pool_scale))
    repl_shapes = [grads[n].shape for n in REPL_NAMES]
    packed_g, counts = _pack_rows([grads[n] for n in REPL_NAMES])
    all_g = all_gather8("gather_small_grads", packed_g)
    zeros_ps = jnp.zeros((N_A, d), F32)
    pick = lambda src: [zeros_ps if n == "pool_scale_full" else src[n] for n in REPL_NAMES]
    packed = [_pack_rows(pick(src))[0] for src in (weights, mom1, mom2)]
    outs = adamw("adamw_replicated", [all_g], *packed)
    unpacked = [_unpack_rows(o_, counts, repl_shapes) for o_ in outs]
    for idx, n in enumerate(REPL_NAMES):
        results[n] = [u_[idx] for u_ in unpacked]
    g_ps = lax.dynamic_index_in_dim(results["pool_scale_full"][0].reshape(N_A, N_CHIPS, PG), my_chip, axis=1, keepdims=False)
    results["pool_scale"] = adamw("adamw_pool_scale", [g_ps], pool_scale, m_pool_scale, v_pool_scale)

    out = [loss, grad_x]
    for part in range(4):
        out += [results[n][part] for n in names]
    return tuple(out)
```

```python
import functools

import jax
import jax.numpy as jnp
from jax import lax
from jax.experimental import pallas as pl
from jax.experimental.pallas import tpu as pltpu

F32 = jnp.float32
BF16 = jnp.bfloat16
MESH = pl.DeviceIdType.MESH

DEPTH = 4
N_A = 2
N_HEADS = 16
DN = 128
DR = 64
HD = 256
QK_DIM = DN + DR
POOL_WINDOWS = (2, 4, 8, 16)
PG = 256
HALO = 16
N_CHIPS = 4
RMS_EPS = 1e-6
ROPE_THETA = 10000.0
ATT_SCALE = QK_DIM ** -0.5
LOG2E = 1.4426950408889634
LN2 = 0.6931471805599453
Q_PRESCALE = ATT_SCALE * LOG2E
NEG = -1e30
LANES = 128

ADAM_LR = 0.001
ADAM_B1 = 0.9
ADAM_B2 = 0.999
ADAM_EPS = 1e-08
ADAM_WD = 0.01
ADAM_STEP = 10

VMEM_BYTES_V7X = 64 * 1024 * 1024
VMEM_LIMIT = VMEM_BYTES_V7X * 3 // 4
BIG_VMEM = VMEM_BYTES_V7X * 7 // 8
FFN_RB = 64

NN = ((1,), (0,))
NT = ((1,), (1,))
TN = ((0,), (0,))


def _cp(*sem):
    return pltpu.CompilerParams(dimension_semantics=sem, vmem_limit_bytes=VMEM_LIMIT)


def _dot(a, b, dims):
    return lax.dot_general(a, b, (dims, ((), ())), preferred_element_type=F32)


def _tile(n, pref):
    if n <= pref:
        return n
    if n % pref == 0:
        return pref
    t = 1 << (pref.bit_length() - 1)
    while n % t:
        t //= 2
    return t


def _rinv(x):
    return lax.rsqrt(jnp.mean(x * x, axis=-1, keepdims=True) + RMS_EPS)


def _sigmoid(a):
    return 1.0 / (1.0 + jnp.exp(-a))


def _mm(name, a, b, *, grid, a_block, a_map, b_block, b_map, o_block, o_map, out_shape, dims,
        out_dtype, res=None):
    nax = len(grid)
    nk = grid[-1]
    acc_shape = tuple(d for d in o_block if d is not None)
    has_res = res is not None

    def body(*refs):
        a_ref, b_ref = refs[0], refs[1]
        o_ref, acc = refs[-2], refs[-1]
        k = pl.program_id(nax - 1)

        @pl.when(k == 0)
        def _():
            acc[...] = jnp.zeros(acc_shape, F32)

        acc[...] += _dot(a_ref[...].astype(BF16), b_ref[...].astype(BF16), dims)

        @pl.when(k == nk - 1)
        def _():
            r = acc[...]
            if has_res:
                r = r + refs[2][...].astype(F32)
            o_ref[...] = r.astype(out_dtype)

    in_specs = [pl.BlockSpec(a_block, a_map), pl.BlockSpec(b_block, b_map)]
    args = [a, b]
    if has_res:
        in_specs.append(pl.BlockSpec(o_block, o_map))
        args.append(res)
    return pl.pallas_call(
        body, name=name, grid=grid, in_specs=in_specs, out_specs=pl.BlockSpec(o_block, o_map),
        out_shape=jax.ShapeDtypeStruct(out_shape, out_dtype),
        scratch_shapes=[pltpu.VMEM(acc_shape, F32)],
        compiler_params=_cp(*(("parallel",) * (nax - 1) + ("arbitrary",))),
    )(*args)


def mm_nn(name, a, b, out_dtype, tm, tn, tk, res=None):
    (m, kd), n = a.shape, b.shape[1]
    tm, tn, tk = _tile(m, tm), _tile(n, tn), _tile(kd, tk)
    return _mm(name, a, b, grid=(m // tm, n // tn, kd // tk),
               a_block=(tm, tk), a_map=lambda i, j, k: (i, k),
               b_block=(tk, tn), b_map=lambda i, j, k: (k, j),
               o_block=(tm, tn), o_map=lambda i, j, k: (i, j),
               out_shape=(m, n), dims=NN, out_dtype=out_dtype, res=res)


def mm_nt(name, a, b, out_dtype, tm, tn, tk):
    (m, kd), n = a.shape, b.shape[0]
    tm, tn, tk = _tile(m, tm), _tile(n, tn), _tile(kd, tk)
    return _mm(name, a, b, grid=(m // tm, n // tn, kd // tk),
               a_block=(tm, tk), a_map=lambda i, j, k: (i, k),
               b_block=(tn, tk), b_map=lambda i, j, k: (j, k),
               o_block=(tm, tn), o_map=lambda i, j, k: (i, j),
               out_shape=(m, n), dims=NT, out_dtype=out_dtype)


def mm_tn(name, a, b, out_dtype, tm, tn, tk):
    (kd, m), n = a.shape, b.shape[1]
    tm, tn, tk = _tile(m, tm), _tile(n, tn), _tile(kd, tk)
    return _mm(name, a, b, grid=(m // tm, n // tn, kd // tk),
               a_block=(tk, tm), a_map=lambda i, j, k: (k, i),
               b_block=(tk, tn), b_map=lambda i, j, k: (k, j),
               o_block=(tm, tn), o_map=lambda i, j, k: (i, j),
               out_shape=(m, n), dims=TN, out_dtype=out_dtype)


def rms_fwd(name, x, gain):
    s, d = x.shape
    tm = _tile(s, 1024)

    def body(x_ref, g_ref, o_ref):
        xv = x_ref[...]
        o_ref[...] = (xv * _rinv(xv) * g_ref[...]).astype(BF16)

    return pl.pallas_call(
        body, name=name, grid=(s // tm,),
        in_specs=[pl.BlockSpec((tm, d), lambda i: (i, 0)), pl.BlockSpec((1, d), lambda i: (0, 0))],
        out_specs=pl.BlockSpec((tm, d), lambda i: (i, 0)),
        out_shape=jax.ShapeDtypeStruct((s, d), BF16), compiler_params=_cp("parallel"),
    )(x, gain)


def _rms_bwd_math(xv, g, du):
    rinv = _rinv(xv)
    xhat = xv * rinv
    dgain = jnp.sum(du * xhat, axis=0, keepdims=True)
    dxh = du * g
    dx = rinv * (dxh - xhat * jnp.mean(dxh * xhat, axis=-1, keepdims=True))
    return dx, dgain


def rms_bwd(name, x, gain, du, dres=None):
    s, d = x.shape
    tm = _tile(s, 1024)
    has_res = dres is not None

    def body(*refs):
        x_ref, g_ref, du_ref = refs[:3]
        dx_ref, dg_ref = refs[-2:]

        @pl.when(pl.program_id(0) == 0)
        def _():
            dg_ref[...] = jnp.zeros((1, d), F32)

        dx, dgain = _rms_bwd_math(x_ref[...], g_ref[...], du_ref[...].astype(F32))
        if has_res:
            dx = dx + refs[3][...]
        dx_ref[...] = dx
        dg_ref[...] += dgain

    row = pl.BlockSpec((tm, d), lambda i: (i, 0))
    vec = pl.BlockSpec((1, d), lambda i: (0, 0))
    args = [x, gain, du] + ([dres] if has_res else [])
    return pl.pallas_call(
        body, name=name, grid=(s // tm,),
        in_specs=[row, vec, row] + ([row] if has_res else []),
        out_specs=[row, vec],
        out_shape=[jax.ShapeDtypeStruct((s, d), F32), jax.ShapeDtypeStruct((1, d), F32)],
        compiler_params=_cp("arbitrary"),
    )(*args)


def ffn_fwd(name, h, gain, wgT, wuT, wd, carry=None):
    s, d = h.shape
    f = wd.shape[0]
    tm, tf = _tile(s, 1024), _tile(f, 256)
    nf = f // tf
    carry = carry or _NoCarry()
    grid = (s // tm, nf)

    def body(*refs):
        (h_ref, g_ref, wg_ref, wu_ref, wd_ref), c_in, (o_ref,), c_out, (u_sc, acc), c_sem = carry.split(refs, 5, 1, 2)
        j = pl.program_id(1)
        carry.at_first_step(grid, c_in, c_out, c_sem)

        @pl.when(j == 0)
        def _():
            xv = h_ref[...]
            u_sc[...] = (xv * _rinv(xv) * g_ref[...]).astype(BF16)
            acc[...] = jnp.zeros((tm, d), F32)

        u = u_sc[...]
        a = _dot(u, wg_ref[...], NT)
        b = _dot(u, wu_ref[...], NT)
        act = (a * _sigmoid(a) * b).astype(BF16)
        acc[...] += _dot(act, wd_ref[...], NN)

        @pl.when(j == nf - 1)
        def _():
            o_ref[...] = h_ref[...] + 0.5 * acc[...]

        carry.at_last_step(grid, c_in, c_out, c_sem)

    row = pl.BlockSpec((tm, d), lambda i, j: (i, 0))
    wsp = pl.BlockSpec((tf, d), lambda i, j: (j, 0))
    outs = pl.pallas_call(
        body, name=name, grid=grid,
        in_specs=[row, pl.BlockSpec((1, d), lambda i, j: (0, 0)), wsp, wsp, wsp] + carry.in_specs(),
        out_specs=[row] + carry.out_specs(), out_shape=[jax.ShapeDtypeStruct((s, d), F32)] + carry.out_shapes(),
        scratch_shapes=[pltpu.VMEM((tm, d), BF16), pltpu.VMEM((tm, d), F32)] + carry.scratch(),
        compiler_params=_cp("arbitrary", "arbitrary"),
    )(h, gain, wgT, wuT, wd, *carry.arrays)
    return outs[0], list(outs[1:])


def ffn_bwd(name, h, gain, wgT, wuT, wd, dout, carry=None):
    s, d = h.shape
    f = wd.shape[0]
    tm, tf = _tile(s, 1024), _tile(f, 256)
    nf = f // tf
    halves = 2 if tm % (2 * FFN_RB) == 0 else 1
    hm = tm // halves
    carry = carry or _NoCarry()
    grid = (s // tm, nf)

    def body(*refs):
        ((h_ref, g_ref, wg_ref, wu_ref, wd_ref, do_ref), c_in,
         (dh_ref, u_ref, dob_ref, act_ref, da_ref, db_ref, dg_ref), c_out,
         (du_acc, a_sc, b_sc, c_sc), c_sem) = carry.split(refs, 6, 7, 4)
        i, j = pl.program_id(0), pl.program_id(1)
        carry.at_first_step(grid, c_in, c_out, c_sem)

        @pl.when(j == 0)
        def _():
            xv = h_ref[...]
            u_ref[...] = (xv * _rinv(xv) * g_ref[...]).astype(BF16)
            dob_ref[...] = (0.5 * do_ref[...]).astype(BF16)
            du_acc[...] = jnp.zeros((tm, d), F32)

        @pl.when((i == 0) & (j == 0))
        def _():
            dg_ref[...] = jnp.zeros((1, d), F32)

        wg, wu = wg_ref[...], wu_ref[...]
        for part in range(halves):
            prow = pl.ds(part * hm, hm)
            u = u_ref[prow, :]
            a_sc[prow, :] = _dot(u, wg, NT)
            b_sc[prow, :] = _dot(u, wu, NT)
            c_sc[prow, :] = _dot(dob_ref[prow, :], wd_ref[...], NT)
            for r in range(hm // FFN_RB):
                rows = pl.ds(part * hm + r * FFN_RB, FFN_RB)
                for t in range(tf // LANES):
                    cols = slice(t * LANES, (t + 1) * LANES)
                    a, b, dact = a_sc[rows, cols], b_sc[rows, cols], c_sc[rows, cols]
                    sig = _sigmoid(a)
                    sa = a * sig
                    act_ref[rows, cols] = (sa * b).astype(BF16)
                    db_ref[rows, cols] = (dact * sa).astype(BF16)
                    da_ref[rows, cols] = (dact * b * (sig * (1.0 + a * (1.0 - sig)))).astype(BF16)
            du_acc[prow, :] += _dot(da_ref[prow, :], wg, NN) + _dot(db_ref[prow, :], wu, NN)

        @pl.when(j == nf - 1)
        def _():
            dx, dgain = _rms_bwd_math(h_ref[...], g_ref[...], du_acc[...])
            dh_ref[...] = do_ref[...] + dx
            dg_ref[...] += dgain

        carry.at_last_step(grid, c_in, c_out, c_sem)

    row = pl.BlockSpec((tm, d), lambda i, j: (i, 0))
    vec = pl.BlockSpec((1, d), lambda i, j: (0, 0))
    wsp = pl.BlockSpec((tf, d), lambda i, j: (j, 0))
    hid = pl.BlockSpec((tm, tf), lambda i, j: (i, j))
    sd = lambda dt: jax.ShapeDtypeStruct((s, d), dt)
    sf = jax.ShapeDtypeStruct((s, f), BF16)
    outs = pl.pallas_call(
        body, name=name, grid=grid,
        in_specs=[row, vec, wsp, wsp, wsp, row] + carry.in_specs(),
        out_specs=[row, row, row, hid, hid, hid, vec] + carry.out_specs(),
        out_shape=[sd(F32), sd(BF16), sd(BF16), sf, sf, sf, jax.ShapeDtypeStruct((1, d), F32)] + carry.out_shapes(),
        scratch_shapes=[pltpu.VMEM((tm, d), F32)] + [pltpu.VMEM((tm, tf), F32)] * 3 + carry.scratch(),
        compiler_params=pltpu.CompilerParams(dimension_semantics=("arbitrary", "arbitrary"),
                                             vmem_limit_bytes=BIG_VMEM),
    )(h, gain, wgT, wuT, wd, dout, *carry.arrays)
    return outs[:7], list(outs[7:])


def pool_fwd(name, h, gain, pw, scale, layer):
    s, d = h.shape
    tm = _tile(s, 512)

    def body(h_ref, g_ref, pw_ref, sc_ref, o_ref, y_ref, ext):
        i = pl.program_id(0)

        @pl.when(i == 0)
        def _():
            ext[0:HALO, :] = jnp.zeros((HALO, d), F32)

        xv = h_ref[...]
        ext[HALO:HALO + tm, :] = xv * _rinv(xv) * g_ref[...]
        pos = i * tm + lax.broadcasted_iota(jnp.int32, (tm, 1), 0)
        for g, w in enumerate(POOL_WINDOWS):
            lo, hi = g * PG, (g + 1) * PG
            ug = ext[HALO:HALO + tm, lo:hi]
            tot = ug
            for k in range(1, w):
                tot = tot + ext[HALO - k:HALO - k + tm, lo:hi]
            cnt = jnp.minimum(pos + 1, w).astype(F32)
            yb = (tot / cnt - ug).astype(BF16)
            y_ref[:, lo:hi] = yb
            o_ref[:, lo:hi] = xv[:, lo:hi] + _dot(yb, pw_ref[g], NN) * sc_ref[:, lo:hi]
        ext[0:HALO, :] = ext[tm:tm + HALO, :]

    row = pl.BlockSpec((tm, d), lambda i: (i, 0))
    vec = pl.BlockSpec((1, d), lambda i: (0, 0))
    return pl.pallas_call(
        body, name=name, grid=(s // tm,),
        in_specs=[row, vec, pl.BlockSpec((None, 4, PG, PG), lambda i: (layer, 0, 0, 0)), vec],
        out_specs=[row, row],
        out_shape=[jax.ShapeDtypeStruct((s, d), F32), jax.ShapeDtypeStruct((s, d), BF16)],
        scratch_shapes=[pltpu.VMEM((HALO + tm, d), F32)],
        compiler_params=_cp("arbitrary"),
    )(h, gain, pw, scale)


def pool_bwd(name, h, gain, y, pw, scale, layer, dout):
    s, d = h.shape
    tm = _tile(s, 512)
    ns = s // tm

    def body(h_ref, g_ref, y_ref, pw_ref, sc_ref, do_ref, dh_ref, dz_ref, dg_ref, dsc_ref, ext, du_sc):
        i = pl.program_id(0)
        t = ns - 1 - i

        @pl.when(i == 0)
        def _():
            ext[tm:tm + HALO, :] = jnp.zeros((HALO, d), F32)
            dg_ref[...] = jnp.zeros((1, d), F32)
            dsc_ref[...] = jnp.zeros((1, d), F32)

        pos = t * tm + lax.broadcasted_iota(jnp.int32, (tm, 1), 0)
        for g, w in enumerate(POOL_WINDOWS):
            lo, hi = g * PG, (g + 1) * PG
            dog = do_ref[:, lo:hi]
            z = _dot(y_ref[:, lo:hi], pw_ref[g], NN)
            dsc_ref[:, lo:hi] += jnp.sum(dog * z, axis=0, keepdims=True)
            dzb = (dog * sc_ref[:, lo:hi]).astype(BF16)
            dz_ref[:, lo:hi] = dzb
            dy = _dot(dzb, pw_ref[g], NT)
            cnt = jnp.minimum(pos + 1, w).astype(F32)
            ext[0:tm, lo:hi] = dy / cnt
            tot = ext[0:tm, lo:hi]
            for k in range(1, w):
                tot = tot + ext[k:k + tm, lo:hi]
            du_sc[:, lo:hi] = tot - dy
        ext[tm:tm + HALO, :] = ext[0:HALO, :]
        dx, dgain = _rms_bwd_math(h_ref[...], g_ref[...], du_sc[...])
        dh_ref[...] = do_ref[...] + dx
        dg_ref[...] += dgain

    row = pl.BlockSpec((tm, d), lambda i: (ns - 1 - i, 0))
    vec = pl.BlockSpec((1, d), lambda i: (0, 0))
    return pl.pallas_call(
        body, name=name, grid=(ns,),
        in_specs=[row, vec, row, pl.BlockSpec((None, 4, PG, PG), lambda i: (layer, 0, 0, 0)), vec, row],
        out_specs=[row, row, vec, vec],
        out_shape=[jax.ShapeDtypeStruct((s, d), F32), jax.ShapeDtypeStruct((s, d), BF16),
                   jax.ShapeDtypeStruct((1, d), F32), jax.ShapeDtypeStruct((1, d), F32)],
        scratch_shapes=[pltpu.VMEM((tm + HALO, d), F32), pltpu.VMEM((tm, d), F32)],
        compiler_params=_cp("arbitrary"),
    )(h, gain, y, pw, scale, dout)


def _rope128(raw_hi, cs):
    b = raw_hi * cs
    r = b + pltpu.roll(b, DR, 1)
    lane = lax.broadcasted_iota(jnp.int32, r.shape, 1)
    return jnp.where(lane < DR, r, 0.0)


def _rope128_bwd(t, cs):
    return (t + pltpu.roll(t, DR, 1)) * cs


def q_heads(name, cq, w256, cs):
    s = cq.shape[0]
    nh = w256.shape[0]
    tm = _tile(s, 1024)

    def body(cq_ref, w_ref, cs_ref, o_ref):
        r = _dot(cq_ref[...], w_ref[...], NN) * Q_PRESCALE
        o_ref[:, 0:DN] = r[:, 0:DN].astype(BF16)
        o_ref[:, DN:HD] = _rope128(r[:, DN:HD], cs_ref[...]).astype(BF16)

    return pl.pallas_call(
        body, name=name, grid=(s // tm, nh),
        in_specs=[pl.BlockSpec((tm, HD), lambda i, hh: (i, 0)),
                  pl.BlockSpec((None, HD, HD), lambda i, hh: (hh, 0, 0)),
                  pl.BlockSpec((tm, DN), lambda i, hh: (i, 0))],
        out_specs=pl.BlockSpec((None, tm, HD), lambda i, hh: (hh, i, 0)),
        out_shape=jax.ShapeDtypeStruct((nh, s, HD), BF16),
        compiler_params=_cp("parallel", "parallel"),
    )(cq, w256, cs)


def kv_post(name, kv_raw, gain, cs):
    s = kv_raw.shape[0]
    tm = _tile(s, 1024)

    def body(x_ref, g_ref, cs_ref, c_ref, kr_ref):
        c = x_ref[:, 0:DN]
        c_ref[...] = (c * _rinv(c) * g_ref[...]).astype(BF16)
        kr_ref[...] = _rope128(x_ref[:, DN:HD], cs_ref[...]).astype(BF16)

    half = pl.BlockSpec((tm, DN), lambda i: (i, 0))
    return pl.pallas_call(
        body, name=name, grid=(s // tm,),
        in_specs=[pl.BlockSpec((tm, HD), lambda i: (i, 0)), pl.BlockSpec((1, DN), lambda i: (0, 0)), half],
        out_specs=[half, half],
        out_shape=[jax.ShapeDtypeStruct((s, DN), BF16), jax.ShapeDtypeStruct((s, DN), BF16)],
        compiler_params=_cp("parallel"),
    )(kv_raw, gain, cs)


def kv_heads(name, c_kv, kr, wuk, wuv):
    s = c_kv.shape[0]
    nh = wuk.shape[0]
    tm = _tile(s, 1024)

    def body(c_ref, kr_ref, wk_ref, wv_ref, k_ref, v_ref):
        c = c_ref[...]
        k_ref[:, 0:DN] = _dot(c, wk_ref[...], NN).astype(BF16)
        k_ref[:, DN:HD] = kr_ref[...]
        v_ref[:, 0:DN] = _dot(c, wv_ref[...], NN).astype(BF16)
        v_ref[:, DN:HD] = jnp.ones((tm, DN), BF16)

    half = pl.BlockSpec((tm, DN), lambda i, hh: (i, 0))
    wsp = pl.BlockSpec((None, DN, DN), lambda i, hh: (hh, 0, 0))
    head = pl.BlockSpec((None, tm, HD), lambda i, hh: (hh, i, 0))
    return pl.pallas_call(
        body, name=name, grid=(s // tm, nh),
        in_specs=[half, half, wsp, wsp], out_specs=[head, head],
        out_shape=[jax.ShapeDtypeStruct((nh, s, HD), BF16), jax.ShapeDtypeStruct((nh, s, HD), BF16)],
        compiler_params=_cp("parallel", "parallel"),
    )(c_kv, kr, wuk, wuv)


def kv_heads_bwd(name, dk, dv, c_kv, wuk, wuv):
    nh, s, _ = dk.shape
    tm = _tile(s, 1024)

    def body(dk_ref, dv_ref, c_ref, wk_ref, wv_ref, dc_ref, dkr_ref, dwk_ref, dwv_ref):
        i, hh = pl.program_id(0), pl.program_id(1)

        @pl.when(hh == 0)
        def _():
            dc_ref[...] = jnp.zeros((tm, DN), F32)
            dkr_ref[...] = jnp.zeros((tm, DN), F32)

        @pl.when(i == 0)
        def _():
            dwk_ref[hh] = jnp.zeros((DN, DN), F32)
            dwv_ref[hh] = jnp.zeros((DN, DN), F32)

        dkn = dk_ref[:, 0:DN].astype(BF16)
        dvb = dv_ref[...].astype(BF16)
        c = c_ref[...]
        dc_ref[...] += _dot(dkn, wk_ref[...], NT) + _dot(dvb, wv_ref[...], NT)
        dkr_ref[...] += dk_ref[:, DN:HD]
        dwk_ref[hh] += _dot(c, dkn, TN)
        dwv_ref[hh] += _dot(c, dvb, TN)

    half = pl.BlockSpec((tm, DN), lambda i, hh: (i, 0))
    wsp = pl.BlockSpec((None, DN, DN), lambda i, hh: (hh, 0, 0))
    allw = pl.BlockSpec((nh, DN, DN), lambda i, hh: (0, 0, 0))
    return pl.pallas_call(
        body, name=name, grid=(s // tm, nh),
        in_specs=[pl.BlockSpec((None, tm, HD), lambda i, hh: (hh, i, 0)),
                  pl.BlockSpec((None, tm, DN), lambda i, hh: (hh, i, 0)), half, wsp, wsp],
        out_specs=[half, half, allw, allw],
        out_shape=[jax.ShapeDtypeStruct((s, DN), F32), jax.ShapeDtypeStruct((s, DN), F32),
                   jax.ShapeDtypeStruct((nh, DN, DN), F32), jax.ShapeDtypeStruct((nh, DN, DN), F32)],
        compiler_params=_cp("arbitrary", "arbitrary"),
    )(dk, dv, c_kv, wuk, wuv)


def kv_post_bwd(name, kv_raw, gain, dc, dkr, cs):
    s = kv_raw.shape[0]
    tm = _tile(s, 1024)

    def body(x_ref, g_ref, dc_ref, dkr_ref, cs_ref, o_ref, dg_ref):
        @pl.when(pl.program_id(0) == 0)
        def _():
            dg_ref[...] = jnp.zeros((1, DN), F32)

        dx, dgain = _rms_bwd_math(x_ref[:, 0:DN], g_ref[...], dc_ref[...])
        o_ref[:, 0:DN] = dx.astype(BF16)
        o_ref[:, DN:HD] = _rope128_bwd(dkr_ref[...], cs_ref[...]).astype(BF16)
        dg_ref[...] += dgain

    half = pl.BlockSpec((tm, DN), lambda i: (i, 0))
    vec = pl.BlockSpec((1, DN), lambda i: (0, 0))
    full = pl.BlockSpec((tm, HD), lambda i: (i, 0))
    return pl.pallas_call(
        body, name=name, grid=(s // tm,), in_specs=[full, vec, half, half, half], out_specs=[full, vec],
        out_shape=[jax.ShapeDtypeStruct((s, HD), BF16), jax.ShapeDtypeStruct((1, DN), F32)],
        compiler_params=_cp("arbitrary"),
    )(kv_raw, gain, dc, dkr, cs)


TQ = 1024
TC = 512
RB = 64


def _flash_cp():
    return pltpu.CompilerParams(dimension_semantics=("parallel", "arbitrary"), vmem_limit_bytes=BIG_VMEM)


def _causal(x, row0, col0, keys_on_rows=False):
    r = row0 + lax.broadcasted_iota(jnp.int32, x.shape, 0)
    c = col0 + lax.broadcasted_iota(jnp.int32, x.shape, 1)
    return jnp.where((r <= c) if keys_on_rows else (c <= r), x, NEG)


def _chunk_pipeline(n_pairs, chunk_of, scores, finish, bufs):
    b0, b1 = bufs
    scores(chunk_of(0), b0)

    def pair(t, carry):
        m = 2 * t
        scores(chunk_of(m + 1), b1)
        finish(chunk_of(m), b0, False)
        scores(chunk_of(m + 2), b0)
        finish(chunk_of(m + 1), b1, False)
        return carry

    lax.fori_loop(0, n_pairs, pair, 0)
    m = 2 * n_pairs
    scores(chunk_of(m + 1), b1)
    finish(chunk_of(m), b0, True)
    finish(chunk_of(m + 1), b1, True)


def flash_fwd(name, q, k, v2):
    nh, s, _ = q.shape
    tq, tk = _tile(s, TQ), _tile(s, TC)
    assert tq == 2 * tk
    nt = tk // LANES

    def body(q_ref, k_ref, v_ref, o_ref, lse_ref, s0_sc, s1_sc, p_sc, m_sc, acc):
        i = pl.program_id(1)
        m_sc[...] = jnp.full((tq, LANES), NEG, F32)
        acc[...] = jnp.zeros((tq, HD), F32)

        def scores(j, dst):
            dst[...] = _dot(q_ref[...], k_ref[pl.ds(pl.multiple_of(j * tk, tk), tk), :], NT)

        def softmax_rows(src, r0, j, masked):
            rows = pl.ds(r0, RB)

            def tile(t):
                x = src[rows, t * LANES:(t + 1) * LANES]
                return _causal(x, i * tq + r0, j * tk + t * LANES) if masked else x

            mx = tile(0)
            for t in range(1, nt):
                mx = jnp.maximum(mx, tile(t))
            m_prev = m_sc[rows, :]
            m_new = jnp.maximum(m_prev, jnp.max(mx, axis=-1, keepdims=True))
            alpha = jnp.exp2(m_prev - m_new)
            for t in range(nt):
                p_sc[rows, t * LANES:(t + 1) * LANES] = jnp.exp2(tile(t) - m_new).astype(BF16)
            m_sc[rows, :] = m_new
            acc[rows, 0:DN] = alpha * acc[rows, 0:DN]
            acc[rows, DN:HD] = alpha * acc[rows, DN:HD]

        def finish(j, src, masked):
            for r in range(tq // RB):
                softmax_rows(src, r * RB, j, masked)
            acc[...] += _dot(p_sc[...], v_ref[pl.ds(pl.multiple_of(j * tk, tk), tk), :], NN)

        _chunk_pipeline(i, lambda m: m, scores, finish, (s0_sc, s1_sc))
        l = acc[:, DN:HD]
        o_ref[...] = (acc[:, 0:DN] / l).astype(BF16)
        lse_ref[...] = m_sc[...] + jnp.log2(l)

    whole = pl.BlockSpec((None, s, HD), lambda hh, i: (hh, 0, 0))
    return pl.pallas_call(
        body, name=name, grid=(nh, s // tq),
        in_specs=[pl.BlockSpec((None, tq, HD), lambda hh, i: (hh, i, 0)), whole, whole],
        out_specs=[pl.BlockSpec((tq, DN), lambda hh, i: (i, hh)),
                   pl.BlockSpec((None, tq, DN), lambda hh, i: (hh, i, 0))],
        out_shape=[jax.ShapeDtypeStruct((s, nh * DN), BF16), jax.ShapeDtypeStruct((nh, s, DN), F32)],
        scratch_shapes=[pltpu.VMEM((tq, tk), F32), pltpu.VMEM((tq, tk), F32), pltpu.VMEM((tq, tk), BF16),
                        pltpu.VMEM((tq, LANES), F32), pltpu.VMEM((tq, HD), F32)],
        compiler_params=_flash_cp(),
    )(q, k, v2)


def attn_do2(name, o, do, nh):
    s = o.shape[0]
    tm = _tile(s, 1024)

    def body(o_ref, do_ref, d_ref):
        dov = do_ref[...]
        delta = jnp.sum(o_ref[...].astype(F32) * dov.astype(F32), axis=-1, keepdims=True)
        hi = delta.astype(BF16).astype(F32)
        lo = delta - hi
        lane = lax.broadcasted_iota(jnp.int32, (tm, DN), 1)
        d_ref[:, 0:DN] = dov
        d_ref[:, DN:HD] = jnp.where(lane == 0, -hi, jnp.where(lane == 1, -lo, 0.0)).astype(BF16)

    blk = pl.BlockSpec((tm, DN), lambda hh, i: (i, hh))
    return pl.pallas_call(
        body, name=name, grid=(nh, s // tm), in_specs=[blk, blk],
        out_specs=pl.BlockSpec((None, tm, HD), lambda hh, i: (hh, i, 0)),
        out_shape=jax.ShapeDtypeStruct((nh, s, HD), BF16), compiler_params=_cp("parallel", "parallel"),
    )(o, do)


def q_heads_bwd(name, dq_acc, cs, w256, cq):
    nh, s, _ = dq_acc.shape
    tm = _tile(s, 1024)
    ns = s // tm

    def body(a_ref, cs_ref, w_ref, cq_ref, dcq_ref, dw_ref, dw_acc):
        i, hh = pl.program_id(0), pl.program_id(1)

        @pl.when(hh == 0)
        def _():
            dcq_ref[...] = jnp.zeros((tm, HD), F32)

        @pl.when(i == 0)
        def _():
            dw_acc[hh] = jnp.zeros((HD, HD), F32)

        draw = jnp.concatenate(
            [(a_ref[:, 0:DN] * ATT_SCALE).astype(BF16),
             _rope128_bwd(a_ref[:, DN:HD] * ATT_SCALE, cs_ref[...]).astype(BF16)], axis=-1)
        dcq_ref[...] += _dot(draw, w_ref[...], NT)
        dw_acc[hh] += _dot(cq_ref[...], draw, TN)

        @pl.when(i == ns - 1)
        def _():
            dw_ref[hh] = dw_acc[hh].astype(BF16)

    row = pl.BlockSpec((tm, HD), lambda i, hh: (i, 0))
    return pl.pallas_call(
        body, name=name, grid=(ns, nh),
        in_specs=[pl.BlockSpec((None, tm, HD), lambda i, hh: (hh, i, 0)), pl.BlockSpec((tm, DN), lambda i, hh: (i, 0)),
                  pl.BlockSpec((None, HD, HD), lambda i, hh: (hh, 0, 0)), row],
        out_specs=[row, pl.BlockSpec((nh, HD, HD), lambda i, hh: (0, 0, 0))],
        out_shape=[jax.ShapeDtypeStruct((s, HD), F32), jax.ShapeDtypeStruct((nh, HD, HD), BF16)],
        scratch_shapes=[pltpu.VMEM((nh, HD, HD), F32)],
        compiler_params=_cp("arbitrary", "arbitrary"),
    )(dq_acc, cs, w256, cq)


def flash_bwd(name, q, k, v2, do2, lse_row, prev=None):
    nh, s, _ = q.shape
    tk, tq = _tile(s, TQ), _tile(s, TC)
    nqc = s // tq
    per = tk // tq
    nt = tq // LANES
    nkt = s // tk
    has_prev = prev is not None

    def body(*refs):
        k_ref, v_ref, q_ref, do_ref, lse_ref = refs[:5]
        dk_ref, dv_ref, dq_hbm, s_sc, d_sc, p_sc, ds_sc, dk_acc, dv_acc, dq_sc, sem = refs[-11:]
        hh, i = pl.program_id(0), pl.program_id(1)
        dk_acc[...] = jnp.zeros((tk, HD), F32)
        dv_acc[...] = jnp.zeros((tk, DN), F32)

        @pl.when(i == 0)
        def _():
            dq_sc[...] = jnp.zeros((s, HD), F32)

        def p_rows(r0, j, masked):
            rows = pl.ds(r0, RB)
            for t in range(nt):
                cols = slice(t * LANES, (t + 1) * LANES)
                x = s_sc[rows, cols]
                if masked:
                    x = _causal(x, i * tk + r0, j * tq + t * LANES, keys_on_rows=True)
                p = jnp.exp2(x - jnp.tile(lse_ref[j][:, cols], (RB // 8, 1)))
                p_sc[rows, cols] = p.astype(BF16)
                ds_sc[rows, cols] = (p * d_sc[rows, cols]).astype(BF16)

        def step(j, masked):
            start = pl.multiple_of(j * tq, tq)
            qj = q_ref[pl.ds(start, tq), :]
            s_sc[...] = _dot(k_ref[...], qj, NT)
            d_sc[...] = _dot(v_ref[...], do_ref[pl.ds(start, tq), :], NT)
            for r in range(tk // RB):
                p_rows(r * RB, j, masked)
            dv_acc[...] += _dot(p_sc[...], do_ref[pl.ds(start, tq), 0:DN], NN)
            ds = ds_sc[...]
            dk_acc[...] += _dot(ds, qj, NN)
            dq_sc[pl.ds(start, tq), :] += _dot(ds, k_ref[...], TN)

        def full_step(m, carry):
            step(nqc - 1 - m, False)
            return carry

        lax.fori_loop(0, nqc - (i + 1) * per, full_step, 0)
        for dd in reversed(range(per)):
            step(i * per + dd, True)
        if has_prev:
            dk_ref[...] = dk_acc[...] * LN2 + refs[5][...]
            dv_ref[...] = dv_acc[...] + refs[6][...]
        else:
            dk_ref[...] = dk_acc[...] * LN2
            dv_ref[...] = dv_acc[...]

        @pl.when(i == nkt - 1)
        def _():
            cp = pltpu.make_async_copy(dq_sc, dq_hbm.at[hh], sem)
            cp.start()
            cp.wait()

    kblk = pl.BlockSpec((None, tk, HD), lambda hh, i: (hh, i, 0))
    vblk = pl.BlockSpec((None, tk, DN), lambda hh, i: (hh, i, 0))
    whole = pl.BlockSpec((None, s, HD), lambda hh, i: (hh, 0, 0), pipeline_mode=pl.Buffered(1))
    in_specs = [kblk, kblk, whole, whole, pl.BlockSpec((None, nqc, 8, tq), lambda hh, i: (hh, 0, 0, 0))]
    args = [k, v2, q, do2, lse_row]
    if has_prev:
        in_specs += [kblk, vblk]
        args += list(prev)
    big = jax.ShapeDtypeStruct((nh, s, HD), F32)
    return pl.pallas_call(
        body, name=name, grid=(nh, nkt), in_specs=in_specs, out_specs=[kblk, vblk, ANY],
        out_shape=[big, jax.ShapeDtypeStruct((nh, s, DN), F32), big],
        scratch_shapes=[pltpu.VMEM((tk, tq), F32)] * 2 + [pltpu.VMEM((tk, tq), BF16)] * 2
        + [pltpu.VMEM((tk, HD), F32), pltpu.VMEM((tk, DN), F32), pltpu.VMEM((s, HD), F32), pltpu.SemaphoreType.DMA],
        compiler_params=pltpu.CompilerParams(dimension_semantics=("arbitrary", "arbitrary"), vmem_limit_bytes=BIG_VMEM),
    )(*args)


def loss_head(name, h, gain, target):
    s, d = h.shape
    tm = _tile(s, 1024)

    def body(h_ref, g_ref, t_ref, sq_ref, dh_ref, dg_ref):
        @pl.when(pl.program_id(0) == 0)
        def _():
            sq_ref[...] = jnp.zeros((1, d), F32)
            dg_ref[...] = jnp.zeros((1, d), F32)

        xv = h_ref[...]
        g = g_ref[...]
        rinv = _rinv(xv)
        xhat = xv * rinv
        err = xhat * g - t_ref[...]
        sq_ref[...] += jnp.sum(err * err, axis=0, keepdims=True)
        dy = err / d
        dg_ref[...] += jnp.sum(dy * xhat, axis=0, keepdims=True)
        dxh = dy * g
        dh_ref[...] = rinv * (dxh - xhat * jnp.mean(dxh * xhat, axis=-1, keepdims=True))

    row = pl.BlockSpec((tm, d), lambda i: (i, 0))
    vec = pl.BlockSpec((1, d), lambda i: (0, 0))
    return pl.pallas_call(
        body, name=name, grid=(s // tm,), in_specs=[row, vec, row], out_specs=[vec, row, vec],
        out_shape=[jax.ShapeDtypeStruct((1, d), F32), jax.ShapeDtypeStruct((s, d), F32),
                   jax.ShapeDtypeStruct((1, d), F32)],
        compiler_params=_cp("arbitrary"),
    )(h, gain, target)


def _adamw_math(w, g, m, v):
    m2 = ADAM_B1 * m + (1.0 - ADAM_B1) * g
    v2 = ADAM_B2 * v + (1.0 - ADAM_B2) * (g * g)
    mh = m2 / (1.0 - ADAM_B1 ** ADAM_STEP)
    vh = v2 / (1.0 - ADAM_B2 ** ADAM_STEP)
    delta = -ADAM_LR * (mh / (jnp.sqrt(vh) + ADAM_EPS) + ADAM_WD * w)
    return delta, m2, v2


def adamw(name, parts, w, m, v):
    rows, cols = w.shape
    tr = _tile(rows, max(8, (1 << 18) // cols))
    stacked = parts[0].ndim == 3
    npart = len(parts)

    def body(*refs):
        p_refs = refs[:npart]
        w_ref, m_ref, v_ref, g_ref, d_ref, m2_ref, v2_ref = refs[npart:]
        if stacked:
            g = p_refs[0][0]
            for n in range(1, parts[0].shape[0]):
                g = g + p_refs[0][n]
        else:
            g = p_refs[0][...]
            for r in p_refs[1:]:
                g = g + r[...]
        delta, m2, v2 = _adamw_math(w_ref[...], g, m_ref[...], v_ref[...])
        g_ref[...] = g
        d_ref[...] = delta
        m2_ref[...] = m2
        v2_ref[...] = v2

    blk = pl.BlockSpec((tr, cols), lambda i: (i, 0))
    pblk = pl.BlockSpec((parts[0].shape[0], tr, cols), lambda i: (0, i, 0)) if stacked else blk
    sds = jax.ShapeDtypeStruct((rows, cols), F32)
    return pl.pallas_call(
        body, name=name, grid=(rows // tr,), in_specs=[pblk] * npart + [blk] * 3,
        out_specs=[blk] * 4, out_shape=[sds] * 4, compiler_params=_cp("parallel"),
    )(*parts, w, m, v)


def sum4(name, x):
    _, a, r, c = x.shape
    tr = _tile(r, max(8, (1 << 18) // c))

    def body(x_ref, o_ref):
        o_ref[...] = ((x_ref[3].astype(F32) + x_ref[0].astype(F32)) + x_ref[1].astype(F32)) + x_ref[2].astype(F32)

    return pl.pallas_call(
        body, name=name, grid=(a, r // tr),
        in_specs=[pl.BlockSpec((4, None, tr, c), lambda i, j: (0, i, j, 0))],
        out_specs=pl.BlockSpec((None, tr, c), lambda i, j: (i, j, 0)),
        out_shape=jax.ShapeDtypeStruct((a, r, c), F32), compiler_params=_cp("parallel", "parallel"),
    )(x)


ANY = pl.BlockSpec(memory_space=pl.ANY)


def _mesh_pos():
    return lax.axis_index("x"), lax.axis_index("y"), lax.axis_index("c")


def _chip_peers(x, y):
    return [(1 - x, y), (x, 1 - y), (1 - x, 1 - y)]


class _NoCarry:
    arrays = ()

    def split(self, refs, n_in, n_out, n_scratch):
        return (refs[:n_in], (), refs[n_in:n_in + n_out], (), refs[n_in + n_out:n_in + n_out + n_scratch], ())

    def in_specs(self):
        return []

    out_specs = out_shapes = scratch = in_specs

    def at_first_step(self, grid, ins, outs, sems):
        pass

    at_last_step = at_first_step


class ChipCopies:
    def __init__(self, kind, arrays):
        assert kind in ("gather", "exchange")
        self.kind, self.arrays, self.n = kind, tuple(arrays), len(arrays)

    def split(self, refs, n_in, n_out, n_scratch):
        n, pos, parts = self.n, 0, []
        for count in (n_in, n, n_out, n, n_scratch, 3):
            parts.append(refs[pos:pos + count])
            pos += count
        return parts

    def in_specs(self):
        return [ANY] * self.n

    out_specs = in_specs

    def out_shapes(self):
        if self.kind == "gather":
            return [jax.ShapeDtypeStruct((a.shape[0], N_CHIPS) + a.shape[1:], a.dtype) for a in self.arrays]
        return [jax.ShapeDtypeStruct((N_CHIPS, a.shape[0]) + a.shape[2:], a.dtype) for a in self.arrays]

    def scratch(self):
        n = self.n
        return [pltpu.SemaphoreType.DMA((3 * n,)), pltpu.SemaphoreType.DMA((3 * n,)), pltpu.SemaphoreType.DMA((n,))]

    def _copies(self, ins, outs, sems):
        send_sems, recv_sems, loc_sems = sems
        x, y, c = _mesh_pos()
        me = 2 * x + y
        peers = _chip_peers(x, y)
        gather = self.kind == "gather"

        def remote(a, j, block):
            px, py = peers[j]
            src = ins[a] if gather else ins[a].at[:, 2 * px + py]
            dst = outs[a].at[:, block] if gather else outs[a].at[j]
            return pltpu.make_async_remote_copy(
                src_ref=src, dst_ref=dst, send_sem=send_sems.at[a * 3 + j], recv_sem=recv_sems.at[a * 3 + j],
                device_id=(px, py, c), device_id_type=MESH)

        pairs = [(a, j) for a in range(self.n) for j in range(3)]
        if gather:
            local = [pltpu.make_async_copy(ins[a], outs[a].at[:, me], loc_sems.at[a]) for a in range(self.n)]
        else:
            local = [pltpu.make_async_copy(ins[a].at[:, me], outs[a].at[3], loc_sems.at[a]) for a in range(self.n)]
        sends = [remote(a, j, me) for a, j in pairs]
        recvs = [remote(a, j, 2 * peers[j][0] + peers[j][1]) for a, j in pairs]
        return local, sends, recvs

    def start(self, ins, outs, sems):
        local, sends, _ = self._copies(ins, outs, sems)
        for cp in local + sends:
            cp.start()

    def wait(self, ins, outs, sems):
        local, sends, recvs = self._copies(ins, outs, sems)
        for cp in recvs:
            cp.wait_recv()
        for cp in sends:
            cp.wait_send()
        for cp in local:
            cp.wait()

    @staticmethod
    def _is_step(grid, last):
        cond = None
        for ax, size in enumerate(grid):
            c = pl.program_id(ax) == (size - 1 if last else 0)
            cond = c if cond is None else cond & c
        return cond

    def at_first_step(self, grid, ins, outs, sems):
        @pl.when(self._is_step(grid, False))
        def _():
            self.start(ins, outs, sems)

    def at_last_step(self, grid, ins, outs, sems):
        @pl.when(self._is_step(grid, True))
        def _():
            self.wait(ins, outs, sems)


def chip_copies(name, copies):
    n = copies.n

    def body(*refs):
        ins, outs, sems = refs[:n], refs[n:2 * n], refs[2 * n:]
        copies.start(ins, outs, sems)
        copies.wait(ins, outs, sems)

    return pl.pallas_call(
        body, name=name, in_specs=copies.in_specs(), out_specs=copies.out_specs(),
        out_shape=copies.out_shapes(), scratch_shapes=copies.scratch(),
    )(*copies.arrays)


def sibling_exchange(name, arrs):
    n = len(arrs)

    def body(*refs):
        ins, outs = refs[:n], refs[n:2 * n]
        send_sems, recv_sems = refs[2 * n:]
        x, y, c = _mesh_pos()
        copies = [pltpu.make_async_remote_copy(
            src_ref=ins[a], dst_ref=outs[a], send_sem=send_sems.at[a], recv_sem=recv_sems.at[a],
            device_id=(x, y, 1 - c), device_id_type=MESH) for a in range(n)]
        for cp in copies:
            cp.start()
        for cp in copies:
            cp.wait_recv()
        for cp in copies:
            cp.wait_send()

    return pl.pallas_call(
        body, name=name, in_specs=[ANY] * n, out_specs=[ANY] * n,
        out_shape=[jax.ShapeDtypeStruct(a.shape, a.dtype) for a in arrs],
        scratch_shapes=[pltpu.SemaphoreType.DMA((n,)), pltpu.SemaphoreType.DMA((n,))],
    )(*arrs)


def all_gather8(name, buf):
    r, cdim = buf.shape

    def body(in_ref, out_ref, send_sems, recv_sems, loc_sem):
        x, y, c = _mesh_pos()
        flips = [(fx, fy, fc) for fx in (0, 1) for fy in (0, 1) for fc in (0, 1)][1:]

        def peer(f):
            return tuple((1 - p) if b else p for p, b in zip((x, y, c), f))

        def remote(j, slot):
            return pltpu.make_async_remote_copy(
                src_ref=in_ref, dst_ref=out_ref.at[slot], send_sem=send_sems.at[j], recv_sem=recv_sems.at[j],
                device_id=peer(flips[j]), device_id_type=MESH)

        me = 4 * x + 2 * y + c
        local = pltpu.make_async_copy(in_ref, out_ref.at[me], loc_sem)
        local.start()
        sends = [remote(j, me) for j in range(7)]
        for cp in sends:
            cp.start()
        for j in range(7):
            px, py, pc = peer(flips[j])
            remote(j, 4 * px + 2 * py + pc).wait_recv()
        for cp in sends:
            cp.wait_send()
        local.wait()

    return pl.pallas_call(
        body, name=name, in_specs=[ANY], out_specs=ANY,
        out_shape=jax.ShapeDtypeStruct((8, r, cdim), buf.dtype),
        scratch_shapes=[pltpu.SemaphoreType.DMA((7,)), pltpu.SemaphoreType.DMA((7,)), pltpu.SemaphoreType.DMA],
    )(buf)


def _rot_cols(w):
    half = w.shape[-1] // 2
    return jnp.concatenate([-w[..., half:], w[..., :half]], axis=-1)


def _fold_rot(g):
    rot = g[..., DN + DR:]
    half = DR // 2
    return jnp.concatenate([g[..., :DN], g[..., DN:DN + DR] + jnp.concatenate([rot[..., half:], -rot[..., :half]], -1)], -1)


def _with_rot(w):
    return jnp.concatenate([w, _rot_cols(w[..., DN:])], axis=-1)


def _rope_table(s):
    pos = jnp.arange(s, dtype=F32)
    inv_freq = ROPE_THETA ** (-jnp.arange(0, DR, 2, dtype=F32) / DR)
    ang = pos[:, None] * inv_freq[None, :]
    cos, sin = jnp.cos(ang), jnp.sin(ang)
    return jnp.concatenate([cos, cos, sin, sin], axis=-1)


def _row_stats(x, tq):
    nh, s, _ = x.shape
    return jnp.broadcast_to(x[:, :, 0].reshape(nh, s // tq, 1, tq), (nh, s // tq, 8, tq))


REPL_NAMES = ["ffn_pre_norm", "mix_norm", "ffn_post_norm", "kv_in_norm", "final_norm", "ckv_norm",
              "q_lora_norm", "pool_scale_full", "w_uk", "w_uv"]
PACK_COLS = 1024


def _pack_rows(arrs):
    rows, counts = [], []
    for a in arrs:
        flat = a.reshape(-1).astype(F32)
        n = -(-flat.shape[0] // PACK_COLS)
        flat = jnp.pad(flat, (0, n * PACK_COLS - flat.shape[0]))
        rows.append(flat.reshape(n, PACK_COLS))
        counts.append(n)
    total = sum(counts)
    pad = -total % 8
    if pad:
        rows.append(jnp.zeros((pad, PACK_COLS), F32))
    return jnp.concatenate(rows, axis=0), counts


def _unpack_rows(buf, counts, shapes):
    out, r = [], 0
    for n, shp in zip(counts, shapes):
        size = 1
        for dd in shp:
            size *= dd
        out.append(buf[r:r + n].reshape(-1)[:size].reshape(shp))
        r += n
    return out


def kernel(x, ffn_pre_norm, ffn_pre_wg, ffn_pre_wu, ffn_pre_wd, mix_norm, ffn_post_norm, ffn_post_wg, ffn_post_wu, ffn_post_wd, pool_w, pool_scale, kv_in_norm, w_dkv, ckv_norm, w_uk, w_uv, q_lora_norm, w_dq, w_uq, w_o, final_norm, loss_target, m_ffn_pre_norm, m_ffn_pre_wg, m_ffn_pre_wu, m_ffn_pre_wd, m_mix_norm, m_ffn_post_norm, m_ffn_post_wg, m_ffn_post_wu, m_ffn_post_wd, m_pool_w, m_pool_scale, m_kv_in_norm, m_w_dkv, m_ckv_norm, m_w_uk, m_w_uv, m_q_lora_norm, m_w_dq, m_w_uq, m_w_o, m_final_norm, v_ffn_pre_norm, v_ffn_pre_wg, v_ffn_pre_wu, v_ffn_pre_wd, v_mix_norm, v_ffn_post_norm, v_ffn_post_wg, v_ffn_post_wu, v_ffn_post_wd, v_pool_w, v_pool_scale, v_kv_in_norm, v_w_dkv, v_ckv_norm, v_w_uk, v_w_uv, v_q_lora_norm, v_w_dq, v_w_uq, v_w_o, v_final_norm):
    weights = dict(ffn_pre_norm=ffn_pre_norm, ffn_pre_wg=ffn_pre_wg, ffn_pre_wu=ffn_pre_wu, ffn_pre_wd=ffn_pre_wd,
                   mix_norm=mix_norm, ffn_post_norm=ffn_post_norm, ffn_post_wg=ffn_post_wg, ffn_post_wu=ffn_post_wu,
                   ffn_post_wd=ffn_post_wd, pool_w=pool_w, pool_scale=pool_scale, kv_in_norm=kv_in_norm, w_dkv=w_dkv,
                   ckv_norm=ckv_norm, w_uk=w_uk, w_uv=w_uv, q_lora_norm=q_lora_norm, w_dq=w_dq, w_uq=w_uq, w_o=w_o,
                   final_norm=final_norm)
    mom1 = dict(ffn_pre_norm=m_ffn_pre_norm, ffn_pre_wg=m_ffn_pre_wg, ffn_pre_wu=m_ffn_pre_wu, ffn_pre_wd=m_ffn_pre_wd,
                mix_norm=m_mix_norm, ffn_post_norm=m_ffn_post_norm, ffn_post_wg=m_ffn_post_wg, ffn_post_wu=m_ffn_post_wu,
                ffn_post_wd=m_ffn_post_wd, pool_w=m_pool_w, pool_scale=m_pool_scale, kv_in_norm=m_kv_in_norm,
                w_dkv=m_w_dkv, ckv_norm=m_ckv_norm, w_uk=m_w_uk, w_uv=m_w_uv, q_lora_norm=m_q_lora_norm, w_dq=m_w_dq,
                w_uq=m_w_uq, w_o=m_w_o, final_norm=m_final_norm)
    mom2 = dict(ffn_pre_norm=v_ffn_pre_norm, ffn_pre_wg=v_ffn_pre_wg, ffn_pre_wu=v_ffn_pre_wu, ffn_pre_wd=v_ffn_pre_wd,
                mix_norm=v_mix_norm, ffn_post_norm=v_ffn_post_norm, ffn_post_wg=v_ffn_post_wg, ffn_post_wu=v_ffn_post_wu,
                ffn_post_wd=v_ffn_post_wd, pool_w=v_pool_w, pool_scale=v_pool_scale, kv_in_norm=v_kv_in_norm,
                w_dkv=v_w_dkv, ckv_norm=v_ckv_norm, w_uk=v_w_uk, w_uv=v_w_uv, q_lora_norm=v_q_lora_norm, w_dq=v_w_dq,
                w_uq=v_w_uq, w_o=v_w_o, final_norm=v_final_norm)
    names = list(weights)

    h0 = x[0]
    target = loss_target[0]
    s, d = h0.shape
    nh = N_HEADS
    nb = DEPTH - N_A
    f_loc = ffn_pre_wd.shape[1]
    ffn_dim = N_CHIPS * f_loc
    my_chip = 2 * lax.axis_index("x") + lax.axis_index("y")

    ffn_in = {"pre": (ffn_pre_norm, ffn_pre_wg, ffn_pre_wu, ffn_pre_wd),
              "post": (ffn_post_norm, ffn_post_wg, ffn_post_wu, ffn_post_wd)}
    uq_loc = jnp.transpose(w_uq, (0, 2, 1, 3))
    shard = {
        "pool_w": pool_w.reshape(N_A * 4, PG // N_CHIPS, PG).astype(BF16),
        "pool_scale": pool_scale.reshape(N_A, 1, PG),
        "wdkv": _with_rot(w_dkv)[None].astype(BF16),
        "wdq": w_dq.astype(BF16),
        "wq": _with_rot(uq_loc).reshape(nb * nh, uq_loc.shape[2], HD).astype(BF16),
        "wo": w_o.astype(BF16),
    }
    for part, (_, wg_, wu_, wd_) in ffn_in.items():
        for l in range(DEPTH):
            shard[f"{part}_wg{l}"] = jnp.transpose(wg_[l]).astype(BF16)[None]
            shard[f"{part}_wu{l}"] = jnp.transpose(wu_[l]).astype(BF16)[None]
            shard[f"{part}_wd{l}"] = wd_[l].astype(BF16)[None]
    ffn_names = lambda part, l: [f"{part}_wg{l}", f"{part}_wu{l}", f"{part}_wd{l}"]
    gather_plan = {
        None: ffn_names("pre", 0),
        ("pre", 0): ["pool_w", "pool_scale"] + ffn_names("post", 0) + ffn_names("pre", 1),
        ("post", 0): ffn_names("post", 1) + ["wdkv"] + ffn_names("pre", 2),
        ("pre", 1): ffn_names("post", 2) + ["wdq", "wq", "wo"],
        ("post", 1): ffn_names("pre", 3) + ffn_names("post", 3),
    }
    full = {}

    def absorb(plan_names, outs):
        for n, g in zip(plan_names, outs):
            full[n] = g.reshape(g.shape[0], N_CHIPS * g.shape[2], g.shape[3])

    absorb(gather_plan[None], chip_copies("gather_first", ChipCopies("gather", [shard[n] for n in gather_plan[None]])))
    wuk_h = jnp.transpose(w_uk, (1, 0, 2)).astype(BF16)
    wuv_h = jnp.transpose(w_uv, (1, 0, 2)).astype(BF16)
    cs = _rope_table(s)

    def ffn_forward(part, l, hin):
        plan_names = gather_plan.get((part, l))
        carry = ChipCopies("gather", [shard[n] for n in plan_names]) if plan_names else None
        wg_, wu_, wd_ = [full[n][0] for n in ffn_names(part, l)]
        hout, carried = ffn_fwd(f"ffn_{part}_fwd{l}", hin, ffn_in[part][0][l:l + 1], wg_, wu_, wd_, carry)
        if plan_names:
            absorb(plan_names, carried)
        return hout

    saved = {}
    h = h0
    kv = None
    for l in range(DEPTH):
        saved["a", l] = h
        h = ffn_forward("pre", l, h)
        saved["b", l] = h
        if l == 0:
            pool_w_full = full["pool_w"].reshape(N_A, 4, PG, PG)
            pool_scale_full = full["pool_scale"].reshape(N_A, d)
        if l < N_A:
            h, y = pool_fwd(f"pool_fwd{l}", h, mix_norm[l:l + 1], pool_w_full, pool_scale_full[l:l + 1], l)
            saved["y", l] = y
        else:
            j = l - N_A
            u = rms_fwd(f"mix_norm_fwd{l}", h, mix_norm[l:l + 1])
            cq_raw = mm_nn(f"q_down{l}", u, wdq_full[j], F32, 2048, 256, 1024)
            cq = rms_fwd(f"q_norm_fwd{l}", cq_raw, q_lora_norm[j:j + 1])
            q = q_heads(f"q_heads{l}", cq, wq256[j], cs)
            o, lse = flash_fwd(f"flash_fwd{l}", q, kv[0], kv[1])
            saved["att", l] = (u, cq_raw, cq, q, o, lse)
            h = mm_nn(f"attn_out{l}", o, wo_full[j], F32, 1024, 1024, 1024, res=h)
        saved["c", l] = h
        h = ffn_forward("post", l, h)
        if l == N_A - 1:
            wdkv256 = full["wdkv"][0]
            wdq_full = full["wdq"]
            wq256 = full["wq"].reshape(nb, nh, HD, HD)
            wo_full = full["wo"]
            u_kv = rms_fwd("kv_in_norm_fwd", h, kv_in_norm[None])
            kv_raw = mm_nn("kv_down", u_kv, wdkv256, F32, 2048, 256, 1024)
            c_kv, kr = kv_post("kv_post", kv_raw, ckv_norm[None], cs)
            kv = kv_heads("kv_heads", c_kv, kr, wuk_h, wuv_h)
            saved["kv"] = (u_kv, kv_raw, c_kv)

    sq, dh, g_final = loss_head("loss_head", h, final_norm[None], target)
    loss = lax.psum(0.5 * jnp.sum(sq) / d, ("x", "y", "c"))

    grads = {}
    gvec = {n: [None] * DEPTH for n in ("ffn_pre_norm", "mix_norm", "ffn_post_norm")}
    g_pool_scale, g_qnorm = [None] * N_A, [None] * nb
    dkv = None
    gfull, partial = {}, {}
    split = lambda g: g.reshape(g.shape[0], N_CHIPS, g.shape[1] // N_CHIPS, g.shape[2])

    def layer_grad_names(l):
        extra = [f"pool_w{l}"] if l < N_A else [f"wdq{l - N_A}", f"wq{l - N_A}", f"wo{l - N_A}"]
        return ffn_names("pre", l) + ffn_names("post", l) + extra + (["wdkv"] if l == N_A - 1 else [])

    def chip_sums(plan_names, received):
        for n, r in zip(plan_names, received):
            partial[n] = sum4(f"chip_sum_{n}", r)

    def ffn_backward(part, l, hin, dh):
        plan_names = layer_grad_names(l + 1)[(0 if part == "post" else 1)::2] if l + 1 < DEPTH else []
        if part == "pre" and l == 0:
            plan_names = plan_names + layer_grad_names(0)[3:]
        carry = ChipCopies("exchange", [gfull[n] for n in plan_names]) if plan_names else None
        wg_, wu_, wd_ = [full[n][0] for n in ffn_names(part, l)]
        (dh, u, dob, act, da, db, dgain), received = ffn_bwd(
            f"ffn_{part}_bwd{l}", hin, ffn_in[part][0][l:l + 1], wg_, wu_, wd_, dh, carry)
        if plan_names:
            chip_sums(plan_names, received)
        gvec[f"ffn_{part}_norm"][l] = dgain[0]
        for n, lhs, rhs in zip(ffn_names(part, l), (da, db, act), (u, u, dob)):
            gfull[n] = split(mm_tn(f"d_{n}", lhs, rhs, BF16, ffn_dim // 2, d, 1024)[None])
        return dh

    for l in reversed(range(DEPTH)):
        if l == N_A - 1:
            u_kv, kv_raw, c_kv = saved["kv"]
            dk, dv = dkv
            dc, dkr, dwuk, dwuv = kv_heads_bwd("kv_heads_bwd", dk, dv, c_kv, wuk_h, wuv_h)
            grads["w_uk"] = jnp.transpose(dwuk, (1, 0, 2))
            grads["w_uv"] = jnp.transpose(dwuv, (1, 0, 2))
            dkv_raw, dg_ckv = kv_post_bwd("kv_post_bwd", kv_raw, ckv_norm[None], dc, dkr, cs)
            grads["ckv_norm"] = dg_ckv[0]
            du_kv = mm_nt("kv_du", dkv_raw, wdkv256, F32, 2048, 1024, 256)
            gfull["wdkv"] = split(mm_tn("kv_dwdkv", u_kv, dkv_raw, BF16, 1024, 256, 2048)[None])
            dh, dg_kvin = rms_bwd("kv_in_norm_bwd", saved["a", l + 1], kv_in_norm[None], du_kv, dres=dh)
            grads["kv_in_norm"] = dg_kvin[0]

        dh = ffn_backward("post", l, saved["c", l], dh)

        if l < N_A:
            dh, dz, dgain, dscale = pool_bwd(f"pool_bwd{l}", saved["b", l], mix_norm[l:l + 1], saved["y", l],
                                             pool_w_full, pool_scale_full[l:l + 1], l, dh)
            gvec["mix_norm"][l] = dgain[0]
            g_pool_scale[l] = dscale[0]
            tk = _tile(s, 2048)
            gfull[f"pool_w{l}"] = split(_mm(
                f"pool_dw{l}", saved["y", l], dz, grid=(4, 1, s // tk),
                a_block=(tk, PG), a_map=lambda g, jj, kk: (kk, g), b_block=(tk, PG), b_map=lambda g, jj, kk: (kk, g),
                o_block=(None, PG, PG), o_map=lambda g, jj, kk: (g, 0, 0),
                out_shape=(4, PG, PG), dims=TN, out_dtype=BF16))
        else:
            j = l - N_A
            u, cq_raw, cq, q, o, lse = saved["att", l]
            k_, v_ = kv
            do = mm_nt(f"attn_do{l}", dh, wo_full[j], BF16, 1024, 1024, 1024)
            gfull[f"wo{j}"] = split(mm_tn(f"attn_dwo{l}", o, dh, BF16, 1024, 1024, 1024)[None])
            do2 = attn_do2(f"attn_do2_{l}", o, do, nh)
            dk_, dv_, dq_acc = flash_bwd(f"flash_bwd{l}", q, k_, v_, do2, _row_stats(lse, _tile(s, TC)), prev=dkv)
            dkv = (dk_, dv_)
            dcq, dwq = q_heads_bwd(f"q_heads_bwd{l}", dq_acc, cs, wq256[j], cq)
            gfull[f"wq{j}"] = split(dwq)
            dcq_raw, dg_q = rms_bwd(f"q_norm_bwd{l}", cq_raw, q_lora_norm[j:j + 1], dcq)
            g_qnorm[j] = dg_q[0]
            du = mm_nt(f"q_du{l}", dcq_raw, wdq_full[j], F32, 2048, 1024, 256)
            gfull[f"wdq{j}"] = split(mm_tn(f"q_dwdq{l}", u, dcq_raw, BF16, 1024, 256, 2048)[None])
            dh, dgain = rms_bwd(f"mix_norm_bwd{l}", saved["b", l], mix_norm[l:l + 1], du, dres=dh)
            gvec["mix_norm"][l] = dgain[0]

        dh = ffn_backward("pre", l, saved["a", l], dh)

    grad_x = dh[None]

    last_names = layer_grad_names(0)[:3]
    chip_sums(last_names, chip_copies("exchange_last", ChipCopies("exchange", [gfull[n] for n in last_names])))
    part_names = sorted(partial)
    sibling = dict(zip(part_names, sibling_exchange("exchange_sibling", [partial[n] for n in part_names])))
    pieces = {
        "pool_w": [f"pool_w{l}" for l in range(N_A)], "w_dkv": ["wdkv"], "w_dq": [f"wdq{j}" for j in range(nb)],
        "w_uq": [f"wq{j}" for j in range(nb)], "w_o": [f"wo{j}" for j in range(nb)],
    }
    for part in ("pre", "post"):
        for m in ("wg", "wu", "wd"):
            pieces[f"ffn_{part}_{m}"] = [f"{part}_{m}{l}" for l in range(DEPTH)]

    def natural(name, src):
        p = jnp.concatenate([src[n] for n in pieces[name]], axis=0)
        if name in ("ffn_pre_wg", "ffn_pre_wu", "ffn_post_wg", "ffn_post_wu"):
            return jnp.transpose(p, (0, 2, 1))
        if name == "w_dkv":
            return _fold_rot(p[0])
        if name == "w_uq":
            return jnp.transpose(_fold_rot(p).reshape(nb, nh, p.shape[1], QK_DIM), (0, 2, 1, 3))
        return p.reshape(weights[name].shape)

    results = {}
    for n in pieces:
        shp = weights[n].shape
        as2d = lambda a: a.reshape(-1, shp[-1])
        outs = adamw(f"adamw_{n}", [as2d(natural(n, partial)), as2d(natural(n, sibling))],
                     as2d(weights[n]), as2d(mom1[n]), as2d(mom2[n]))
        results[n] = [o_.reshape(shp) for o_ in outs]

    grads.update(ffn_pre_norm=jnp.stack(gvec["ffn_pre_norm"]), mix_norm=jnp.stack(gvec["mix_norm"]),
                 ffn_post_norm=jnp.stack(gvec["ffn_post_norm"]), final_norm=g_final[0],
                 q_lora_norm=jnp.stack(g_qnorm), pool_scale_full=jnp.stack(g_pool_scale))
    repl_shapes = [grads[n].shape for n in REPL_NAMES]
    packed_g, counts = _pack_rows([grads[n] for n in REPL_NAMES])
    all_g = all_gather8("gather_small_grads", packed_g)
    zeros_ps = jnp.zeros((N_A, d), F32)
    pick = lambda src: [zeros_ps if n == "pool_scale_full" else src[n] for n in REPL_NAMES]
    packed = [_pack_rows(pick(src))[0] for src in (weights, mom1, mom2)]
    outs = adamw("adamw_replicated", [all_g], *packed)
    unpacked = [_unpack_rows(o_, counts, repl_shapes) for o_ in outs]
    for idx, n in enumerate(REPL_NAMES):
        results[n] = [u_[idx] for u_ in unpacked]
    g_ps = lax.dynamic_index_in_dim(results["pool_scale_full"][0].reshape(N_A, N_CHIPS, PG), my_chip, axis=1, keepdims=False)
    results["pool_scale"] = adamw("adamw_pool_scale", [g_ps], pool_scale, m_pool_scale, v_pool_scale)

    out = [loss, grad_x]
    for part in range(4):
        out += [results[n][part] for n in names]
    return tuple(out)
```

```python
import functools

import jax
import jax.numpy as jnp
from jax import lax
from jax.experimental import pallas as pl
from jax.experimental.pallas import tpu as pltpu

F32 = jnp.float32
BF16 = jnp.bfloat16
MESH = pl.DeviceIdType.MESH

DEPTH = 4
N_A = 2
N_HEADS = 16
DN = 128
DR = 64
HD = 256
QK_DIM = DN + DR
POOL_WINDOWS = (2, 4, 8, 16)
PG = 256
HALO = 16
N_CHIPS = 4
RMS_EPS = 1e-6
ROPE_THETA = 10000.0
ATT_SCALE = QK_DIM ** -0.5
LOG2E = 1.4426950408889634
LN2 = 0.6931471805599453
Q_PRESCALE = ATT_SCALE * LOG2E
NEG = -1e30
LANES = 128

ADAM_LR = 0.001
ADAM_B1 = 0.9
ADAM_B2 = 0.999
ADAM_EPS = 1e-08
ADAM_WD = 0.01
ADAM_STEP = 10

VMEM_BYTES_V7X = 64 * 1024 * 1024
VMEM_LIMIT = VMEM_BYTES_V7X * 3 // 4
BIG_VMEM = VMEM_BYTES_V7X * 7 // 8
FFN_RB = 64

NN = ((1,), (0,))
NT = ((1,), (1,))
TN = ((0,), (0,))


def _cp(*sem):
    return pltpu.CompilerParams(dimension_semantics=sem, vmem_limit_bytes=VMEM_LIMIT)


def _dot(a, b, dims):
    return lax.dot_general(a, b, (dims, ((), ())), preferred_element_type=F32)


def _tile(n, pref):
    if n <= pref:
        return n
    if n % pref == 0:
        return pref
    t = 1 << (pref.bit_length() - 1)
    while n % t:
        t //= 2
    return t


def _rinv(x):
    return lax.rsqrt(jnp.mean(x * x, axis=-1, keepdims=True) + RMS_EPS)


def _sigmoid(a):
    return 1.0 / (1.0 + jnp.exp(-a))


def _mm(name, a, b, *, grid, a_block, a_map, b_block, b_map, o_block, o_map, out_shape, dims,
        out_dtype, res=None):
    nax = len(grid)
    nk = grid[-1]
    acc_shape = tuple(d for d in o_block if d is not None)
    has_res = res is not None

    def body(*refs):
        a_ref, b_ref = refs[0], refs[1]
        o_ref, acc = refs[-2], refs[-1]
        k = pl.program_id(nax - 1)

        @pl.when(k == 0)
        def _():
            acc[...] = jnp.zeros(acc_shape, F32)

        acc[...] += _dot(a_ref[...].astype(BF16), b_ref[...].astype(BF16), dims)

        @pl.when(k == nk - 1)
        def _():
            r = acc[...]
            if has_res:
                r = r + refs[2][...].astype(F32)
            o_ref[...] = r.astype(out_dtype)

    in_specs = [pl.BlockSpec(a_block, a_map), pl.BlockSpec(b_block, b_map)]
    args = [a, b]
    if has_res:
        in_specs.append(pl.BlockSpec(o_block, o_map))
        args.append(res)
    return pl.pallas_call(
        body, name=name, grid=grid, in_specs=in_specs, out_specs=pl.BlockSpec(o_block, o_map),
        out_shape=jax.ShapeDtypeStruct(out_shape, out_dtype),
        scratch_shapes=[pltpu.VMEM(acc_shape, F32)],
        compiler_params=_cp(*(("parallel",) * (nax - 1) + ("arbitrary",))),
    )(*args)


def mm_nn(name, a, b, out_dtype, tm, tn, tk, res=None):
    (m, kd), n = a.shape, b.shape[1]
    tm, tn, tk = _tile(m, tm), _tile(n, tn), _tile(kd, tk)
    return _mm(name, a, b, grid=(m // tm, n // tn, kd // tk),
               a_block=(tm, tk), a_map=lambda i, j, k: (i, k),
               b_block=(tk, tn), b_map=lambda i, j, k: (k, j),
               o_block=(tm, tn), o_map=lambda i, j, k: (i, j),
               out_shape=(m, n), dims=NN, out_dtype=out_dtype, res=res)


def mm_nt(name, a, b, out_dtype, tm, tn, tk):
    (m, kd), n = a.shape, b.shape[0]
    tm, tn, tk = _tile(m, tm), _tile(n, tn), _tile(kd, tk)
    return _mm(name, a, b, grid=(m // tm, n // tn, kd // tk),
               a_block=(tm, tk), a_map=lambda i, j, k: (i, k),
               b_block=(tn, tk), b_map=lambda i, j, k: (j, k),
               o_block=(tm, tn), o_map=lambda i, j, k: (i, j),
               out_shape=(m, n), dims=NT, out_dtype=out_dtype)


def mm_tn(name, a, b, out_dtype, tm, tn, tk):
    (kd, m), n = a.shape, b.shape[1]
    tm, tn, tk = _tile(m, tm), _tile(n, tn), _tile(kd, tk)
    return _mm(name, a, b, grid=(m // tm, n // tn, kd // tk),
               a_block=(tk, tm), a_map=lambda i, j, k: (k, i),
               b_block=(tk, tn), b_map=lambda i, j, k: (k, j),
               o_block=(tm, tn), o_map=lambda i, j, k: (i, j),
               out_shape=(m, n), dims=TN, out_dtype=out_dtype)


def rms_fwd(name, x, gain):
    s, d = x.shape
    tm = _tile(s, 1024)

    def body(x_ref, g_ref, o_ref):
        xv = x_ref[...]
        o_ref[...] = (xv * _rinv(xv) * g_ref[...]).astype(BF16)

    return pl.pallas_call(
        body, name=name, grid=(s // tm,),
        in_specs=[pl.BlockSpec((tm, d), lambda i: (i, 0)), pl.BlockSpec((1, d), lambda i: (0, 0))],
        out_specs=pl.BlockSpec((tm, d), lambda i: (i, 0)),
        out_shape=jax.ShapeDtypeStruct((s, d), BF16), compiler_params=_cp("parallel"),
    )(x, gain)


def _rms_bwd_math(xv, g, du):
    rinv = _rinv(xv)
    xhat = xv * rinv
    dgain = jnp.sum(du * xhat, axis=0, keepdims=True)
    dxh = du * g
    dx = rinv * (dxh - xhat * jnp.mean(dxh * xhat, axis=-1, keepdims=True))
    return dx, dgain


def rms_bwd(name, x, gain, du, dres=None):
    s, d = x.shape
    tm = _tile(s, 1024)
    has_res = dres is not None

    def body(*refs):
        x_ref, g_ref, du_ref = refs[:3]
        dx_ref, dg_ref = refs[-2:]

        @pl.when(pl.program_id(0) == 0)
        def _():
            dg_ref[...] = jnp.zeros((1, d), F32)

        dx, dgain = _rms_bwd_math(x_ref[...], g_ref[...], du_ref[...].astype(F32))
        if has_res:
            dx = dx + refs[3][...]
        dx_ref[...] = dx
        dg_ref[...] += dgain

    row = pl.BlockSpec((tm, d), lambda i: (i, 0))
    vec = pl.BlockSpec((1, d), lambda i: (0, 0))
    args = [x, gain, du] + ([dres] if has_res else [])
    return pl.pallas_call(
        body, name=name, grid=(s // tm,),
        in_specs=[row, vec, row] + ([row] if has_res else []),
        out_specs=[row, vec],
        out_shape=[jax.ShapeDtypeStruct((s, d), F32), jax.ShapeDtypeStruct((1, d), F32)],
        compiler_params=_cp("arbitrary"),
    )(*args)


def ffn_fwd(name, h, gain, wgT, wuT, wd, carry=None):
    s, d = h.shape
    f = wd.shape[0]
    tm, tf = _tile(s, 1024), _tile(f, 256)
    nf = f // tf
    carry = carry or _NoCarry()
    grid = (s // tm, nf)

    def body(*refs):
        (h_ref, g_ref, wg_ref, wu_ref, wd_ref), c_in, (o_ref,), c_out, (u_sc, acc), c_sem = carry.split(refs, 5, 1, 2)
        j = pl.program_id(1)
        carry.at_first_step(grid, c_in, c_out, c_sem)

        @pl.when(j == 0)
        def _():
            xv = h_ref[...]
            u_sc[...] = (xv * _rinv(xv) * g_ref[...]).astype(BF16)
            acc[...] = jnp.zeros((tm, d), F32)

        u = u_sc[...]
        a = _dot(u, wg_ref[...], NT)
        b = _dot(u, wu_ref[...], NT)
        act = (a * _sigmoid(a) * b).astype(BF16)
        acc[...] += _dot(act, wd_ref[...], NN)

        @pl.when(j == nf - 1)
        def _():
            o_ref[...] = h_ref[...] + 0.5 * acc[...]

        carry.at_last_step(grid, c_in, c_out, c_sem)

    row = pl.BlockSpec((tm, d), lambda i, j: (i, 0))
    wsp = pl.BlockSpec((tf, d), lambda i, j: (j, 0))
    outs = pl.pallas_call(
        body, name=name, grid=grid,
        in_specs=[row, pl.BlockSpec((1, d), lambda i, j: (0, 0)), wsp, wsp, wsp] + carry.in_specs(),
        out_specs=[row] + carry.out_specs(), out_shape=[jax.ShapeDtypeStruct((s, d), F32)] + carry.out_shapes(),
        scratch_shapes=[pltpu.VMEM((tm, d), BF16), pltpu.VMEM((tm, d), F32)] + carry.scratch(),
        compiler_params=_cp("arbitrary", "arbitrary"),
    )(h, gain, wgT, wuT, wd, *carry.arrays)
    return outs[0], list(outs[1:])


def ffn_bwd(name, h, gain, wgT, wuT, wd, dout, carry=None):
    s, d = h.shape
    f = wd.shape[0]
    tm, tf = _tile(s, 1024), _tile(f, 256)
    nf = f // tf
    halves = 2 if tm % (2 * FFN_RB) == 0 else 1
    hm = tm // halves
    carry = carry or _NoCarry()
    grid = (s // tm, nf)

    def body(*refs):
        ((h_ref, g_ref, wg_ref, wu_ref, wd_ref, do_ref), c_in,
         (dh_ref, u_ref, dob_ref, act_ref, da_ref, db_ref, dg_ref), c_out,
         (du_acc, a_sc, b_sc, c_sc), c_sem) = carry.split(refs, 6, 7, 4)
        i, j = pl.program_id(0), pl.program_id(1)
        carry.at_first_step(grid, c_in, c_out, c_sem)

        @pl.when(j == 0)
        def _():
            xv = h_ref[...]
            u_ref[...] = (xv * _rinv(xv) * g_ref[...]).astype(BF16)
            dob_ref[...] = (0.5 * do_ref[...]).astype(BF16)
            du_acc[...] = jnp.zeros((tm, d), F32)

        @pl.when((i == 0) & (j == 0))
        def _():
            dg_ref[...] = jnp.zeros((1, d), F32)

        wg, wu = wg_ref[...], wu_ref[...]
        for part in range(halves):
            prow = pl.ds(part * hm, hm)
            u = u_ref[prow, :]
            a_sc[prow, :] = _dot(u, wg, NT)
            b_sc[prow, :] = _dot(u, wu, NT)
            c_sc[prow, :] = _dot(dob_ref[prow, :], wd_ref[...], NT)
            for r in range(hm // FFN_RB):
                rows = pl.ds(part * hm + r * FFN_RB, FFN_RB)
                for t in range(tf // LANES):
                    cols = slice(t * LANES, (t + 1) * LANES)
                    a, b, dact = a_sc[rows, cols], b_sc[rows, cols], c_sc[rows, cols]
                    sig = _sigmoid(a)
                    sa = a * sig
                    act_ref[rows, cols] = (sa * b).astype(BF16)
                    db_ref[rows, cols] = (dact * sa).astype(BF16)
                    da_ref[rows, cols] = (dact * b * (sig * (1.0 + a * (1.0 - sig)))).astype(BF16)
            du_acc[prow, :] += _dot(da_ref[prow, :], wg, NN) + _dot(db_ref[prow, :], wu, NN)

        @pl.when(j == nf - 1)
        def _():
            dx, dgain = _rms_bwd_math(h_ref[...], g_ref[...], du_acc[...])
            dh_ref[...] = do_ref[...] + dx
            dg_ref[...] += dgain

        carry.at_last_step(grid, c_in, c_out, c_sem)

    row = pl.BlockSpec((tm, d), lambda i, j: (i, 0))
    vec = pl.BlockSpec((1, d), lambda i, j: (0, 0))
    wsp = pl.BlockSpec((tf, d), lambda i, j: (j, 0))
    hid = pl.BlockSpec((tm, tf), lambda i, j: (i, j))
    sd = lambda dt: jax.ShapeDtypeStruct((s, d), dt)
    sf = jax.ShapeDtypeStruct((s, f), BF16)
    outs = pl.pallas_call(
        body, name=name, grid=grid,
        in_specs=[row, vec, wsp, wsp, wsp, row] + carry.in_specs(),
        out_specs=[row, row, row, hid, hid, hid, vec] + carry.out_specs(),
        out_shape=[sd(F32), sd(BF16), sd(BF16), sf, sf, sf, jax.ShapeDtypeStruct((1, d), F32)] + carry.out_shapes(),
        scratch_shapes=[pltpu.VMEM((tm, d), F32)] + [pltpu.VMEM((tm, tf), F32)] * 3 + carry.scratch(),
        compiler_params=pltpu.CompilerParams(dimension_semantics=("arbitrary", "arbitrary"),
                                             vmem_limit_bytes=BIG_VMEM),
    )(h, gain, wgT, wuT, wd, dout, *carry.arrays)
    return outs[:7], list(outs[7:])


def pool_fwd(name, h, gain, pw, scale, layer):
    s, d = h.shape
    tm = _tile(s, 512)

    def body(h_ref, g_ref, pw_ref, sc_ref, o_ref, y_ref, ext):
        i = pl.program_id(0)

        @pl.when(i == 0)
        def _():
            ext[0:HALO, :] = jnp.zeros((HALO, d), F32)

        xv = h_ref[...]
        ext[HALO:HALO + tm, :] = xv * _rinv(xv) * g_ref[...]
        pos = i * tm + lax.broadcasted_iota(jnp.int32, (tm, 1), 0)
        for g, w in enumerate(POOL_WINDOWS):
            lo, hi = g * PG, (g + 1) * PG
            ug = ext[HALO:HALO + tm, lo:hi]
            tot = ug
            for k in range(1, w):
                tot = tot + ext[HALO - k:HALO - k + tm, lo:hi]
            cnt = jnp.minimum(pos + 1, w).astype(F32)
            yb = (tot / cnt - ug).astype(BF16)
            y_ref[:, lo:hi] = yb
            o_ref[:, lo:hi] = xv[:, lo:hi] + _dot(yb, pw_ref[g], NN) * sc_ref[:, lo:hi]
        ext[0:HALO, :] = ext[tm:tm + HALO, :]

    row = pl.BlockSpec((tm, d), lambda i: (i, 0))
    vec = pl.BlockSpec((1, d), lambda i: (0, 0))
    return pl.pallas_call(
        body, name=name, grid=(s // tm,),
        in_specs=[row, vec, pl.BlockSpec((None, 4, PG, PG), lambda i: (layer, 0, 0, 0)), vec],
        out_specs=[row, row],
        out_shape=[jax.ShapeDtypeStruct((s, d), F32), jax.ShapeDtypeStruct((s, d), BF16)],
        scratch_shapes=[pltpu.VMEM((HALO + tm, d), F32)],
        compiler_params=_cp("arbitrary"),
    )(h, gain, pw, scale)


def pool_bwd(name, h, gain, y, pw, scale, layer, dout):
    s, d = h.shape
    tm = _tile(s, 512)
    ns = s // tm

    def body(h_ref, g_ref, y_ref, pw_ref, sc_ref, do_ref, dh_ref, dz_ref, dg_ref, dsc_ref, ext, du_sc):
        i = pl.program_id(0)
        t = ns - 1 - i

        @pl.when(i == 0)
        def _():
            ext[tm:tm + HALO, :] = jnp.zeros((HALO, d), F32)
            dg_ref[...] = jnp.zeros((1, d), F32)
            dsc_ref[...] = jnp.zeros((1, d), F32)

        pos = t * tm + lax.broadcasted_iota(jnp.int32, (tm, 1), 0)
        for g, w in enumerate(POOL_WINDOWS):
            lo, hi = g * PG, (g + 1) * PG
            dog = do_ref[:, lo:hi]
            z = _dot(y_ref[:, lo:hi], pw_ref[g], NN)
            dsc_ref[:, lo:hi] += jnp.sum(dog * z, axis=0, keepdims=True)
            dzb = (dog * sc_ref[:, lo:hi]).astype(BF16)
            dz_ref[:, lo:hi] = dzb
            dy = _dot(dzb, pw_ref[g], NT)
            cnt = jnp.minimum(pos + 1, w).astype(F32)
            ext[0:tm, lo:hi] = dy / cnt
            tot = ext[0:tm, lo:hi]
            for k in range(1, w):
                tot = tot + ext[k:k + tm, lo:hi]
            du_sc[:, lo:hi] = tot - dy
        ext[tm:tm + HALO, :] = ext[0:HALO, :]
        dx, dgain = _rms_bwd_math(h_ref[...], g_ref[...], du_sc[...])
        dh_ref[...] = do_ref[...] + dx
        dg_ref[...] += dgain

    row = pl.BlockSpec((tm, d), lambda i: (ns - 1 - i, 0))
    vec = pl.BlockSpec((1, d), lambda i: (0, 0))
    return pl.pallas_call(
        body, name=name, grid=(ns,),
        in_specs=[row, vec, row, pl.BlockSpec((None, 4, PG, PG), lambda i: (layer, 0, 0, 0)), vec, row],
        out_specs=[row, row, vec, vec],
        out_shape=[jax.ShapeDtypeStruct((s, d), F32), jax.ShapeDtypeStruct((s, d), BF16),
                   jax.ShapeDtypeStruct((1, d), F32), jax.ShapeDtypeStruct((1, d), F32)],
        scratch_shapes=[pltpu.VMEM((tm + HALO, d), F32), pltpu.VMEM((tm, d), F32)],
        compiler_params=_cp("arbitrary"),
    )(h, gain, y, pw, scale, dout)


def _rope128(raw_hi, cs):
    b = raw_hi * cs
    r = b + pltpu.roll(b, DR, 1)
    lane = lax.broadcasted_iota(jnp.int32, r.shape, 1)
    return jnp.where(lane < DR, r, 0.0)


def _rope128_bwd(t, cs):
    return (t + pltpu.roll(t, DR, 1)) * cs


def q_heads(name, cq, w256, cs):
    s = cq.shape[0]
    nh = w256.shape[0]
    tm = _tile(s, 1024)

    def body(cq_ref, w_ref, cs_ref, o_ref):
        r = _dot(cq_ref[...], w_ref[...], NN) * Q_PRESCALE
        o_ref[:, 0:DN] = r[:, 0:DN].astype(BF16)
        o_ref[:, DN:HD] = _rope128(r[:, DN:HD], cs_ref[...]).astype(BF16)

    return pl.pallas_call(
        body, name=name, grid=(s // tm, nh),
        in_specs=[pl.BlockSpec((tm, HD), lambda i, hh: (i, 0)),
                  pl.BlockSpec((None, HD, HD), lambda i, hh: (hh, 0, 0)),
                  pl.BlockSpec((tm, DN), lambda i, hh: (i, 0))],
        out_specs=pl.BlockSpec((None, tm, HD), lambda i, hh: (hh, i, 0)),
        out_shape=jax.ShapeDtypeStruct((nh, s, HD), BF16),
        compiler_params=_cp("parallel", "parallel"),
    )(cq, w256, cs)


def kv_post(name, kv_raw, gain, cs):
    s = kv_raw.shape[0]
    tm = _tile(s, 1024)

    def body(x_ref, g_ref, cs_ref, c_ref, kr_ref):
        c = x_ref[:, 0:DN]
        c_ref[...] = (c * _rinv(c) * g_ref[...]).astype(BF16)
        kr_ref[...] = _rope128(x_ref[:, DN:HD], cs_ref[...]).astype(BF16)

    half = pl.BlockSpec((tm, DN), lambda i: (i, 0))
    return pl.pallas_call(
        body, name=name, grid=(s // tm,),
        in_specs=[pl.BlockSpec((tm, HD), lambda i: (i, 0)), pl.BlockSpec((1, DN), lambda i: (0, 0)), half],
        out_specs=[half, half],
        out_shape=[jax.ShapeDtypeStruct((s, DN), BF16), jax.ShapeDtypeStruct((s, DN), BF16)],
        compiler_params=_cp("parallel"),
    )(kv_raw, gain, cs)


def kv_heads(name, c_kv, kr, wuk, wuv):
    s = c_kv.shape[0]
    nh = wuk.shape[0]
    tm = _tile(s, 1024)

    def body(c_ref, kr_ref, wk_ref, wv_ref, k_ref, v_ref):
        c = c_ref[...]
        k_ref[:, 0:DN] = _dot(c, wk_ref[...], NN).astype(BF16)
        k_ref[:, DN:HD] = kr_ref[...]
        v_ref[:, 0:DN] = _dot(c, wv_ref[...], NN).astype(BF16)
        v_ref[:, DN:HD] = jnp.ones((tm, DN), BF16)

    half = pl.BlockSpec((tm, DN), lambda i, hh: (i, 0))
    wsp = pl.BlockSpec((None, DN, DN), lambda i, hh: (hh, 0, 0))
    head = pl.BlockSpec((None, tm, HD), lambda i, hh: (hh, i, 0))
    return pl.pallas_call(
        body, name=name, grid=(s // tm, nh),
        in_specs=[half, half, wsp, wsp], out_specs=[head, head],
        out_shape=[jax.ShapeDtypeStruct((nh, s, HD), BF16), jax.ShapeDtypeStruct((nh, s, HD), BF16)],
        compiler_params=_cp("parallel", "parallel"),
    )(c_kv, kr, wuk, wuv)


def kv_heads_bwd(name, dk, dv, c_kv, wuk, wuv):
    nh, s, _ = dk.shape
    tm = _tile(s, 1024)

    def body(dk_ref, dv_ref, c_ref, wk_ref, wv_ref, dc_ref, dkr_ref, dwk_ref, dwv_ref):
        i, hh = pl.program_id(0), pl.program_id(1)

        @pl.when(hh == 0)
        def _():
            dc_ref[...] = jnp.zeros((tm, DN), F32)
            dkr_ref[...] = jnp.zeros((tm, DN), F32)

        @pl.when(i == 0)
        def _():
            dwk_ref[hh] = jnp.zeros((DN, DN), F32)
            dwv_ref[hh] = jnp.zeros((DN, DN), F32)

        dkn = dk_ref[:, 0:DN].astype(BF16)
        dvb = dv_ref[...].astype(BF16)
        c = c_ref[...]
        dc_ref[...] += _dot(dkn, wk_ref[...], NT) + _dot(dvb, wv_ref[...], NT)
        dkr_ref[...] += dk_ref[:, DN:HD]
        dwk_ref[hh] += _dot(c, dkn, TN)
        dwv_ref[hh] += _dot(c, dvb, TN)

    half = pl.BlockSpec((tm, DN), lambda i, hh: (i, 0))
    wsp = pl.BlockSpec((None, DN, DN), lambda i, hh: (hh, 0, 0))
    allw = pl.BlockSpec((nh, DN, DN), lambda i, hh: (0, 0, 0))
    return pl.pallas_call(
        body, name=name, grid=(s // tm, nh),
        in_specs=[pl.BlockSpec((None, tm, HD), lambda i, hh: (hh, i, 0)),
                  pl.BlockSpec((None, tm, DN), lambda i, hh: (hh, i, 0)), half, wsp, wsp],
        out_specs=[half, half, allw, allw],
        out_shape=[jax.ShapeDtypeStruct((s, DN), F32), jax.ShapeDtypeStruct((s, DN), F32),
                   jax.ShapeDtypeStruct((nh, DN, DN), F32), jax.ShapeDtypeStruct((nh, DN, DN), F32)],
        compiler_params=_cp("arbitrary", "arbitrary"),
    )(dk, dv, c_kv, wuk, wuv)


def kv_post_bwd(name, kv_raw, gain, dc, dkr, cs):
    s = kv_raw.shape[0]
    tm = _tile(s, 1024)

    def body(x_ref, g_ref, dc_ref, dkr_ref, cs_ref, o_ref, dg_ref):
        @pl.when(pl.program_id(0) == 0)
        def _():
            dg_ref[...] = jnp.zeros((1, DN), F32)

        dx, dgain = _rms_bwd_math(x_ref[:, 0:DN], g_ref[...], dc_ref[...])
        o_ref[:, 0:DN] = dx.astype(BF16)
        o_ref[:, DN:HD] = _rope128_bwd(dkr_ref[...], cs_ref[...]).astype(BF16)
        dg_ref[...] += dgain

    half = pl.BlockSpec((tm, DN), lambda i: (i, 0))
    vec = pl.BlockSpec((1, DN), lambda i: (0, 0))
    full = pl.BlockSpec((tm, HD), lambda i: (i, 0))
    return pl.pallas_call(
        body, name=name, grid=(s // tm,), in_specs=[full, vec, half, half, half], out_specs=[full, vec],
        out_shape=[jax.ShapeDtypeStruct((s, HD), BF16), jax.ShapeDtypeStruct((1, DN), F32)],
        compiler_params=_cp("arbitrary"),
    )(kv_raw, gain, dc, dkr, cs)


TQ = 1024
TC = 512
RB = 64


def _flash_cp():
    return pltpu.CompilerParams(dimension_semantics=("parallel", "arbitrary"), vmem_limit_bytes=BIG_VMEM)


def _causal(x, row0, col0, keys_on_rows=False):
    r = row0 + lax.broadcasted_iota(jnp.int32, x.shape, 0)
    c = col0 + lax.broadcasted_iota(jnp.int32, x.shape, 1)
    return jnp.where((r <= c) if keys_on_rows else (c <= r), x, NEG)


def _chunk_pipeline(n_pairs, chunk_of, scores, finish, bufs, last_from):
    b0, b1 = bufs
    scores(chunk_of(0), b0, 0)

    def pair(t, carry):
        m = 2 * t
        scores(chunk_of(m + 1), b1, 0)
        finish(chunk_of(m), b0, False, 0)
        scores(chunk_of(m + 2), b0, 0)
        finish(chunk_of(m + 1), b1, False, 0)
        return carry

    lax.fori_loop(0, n_pairs, pair, 0)
    m = 2 * n_pairs
    scores(chunk_of(m + 1), b1, last_from)
    finish(chunk_of(m), b0, True, 0)
    finish(chunk_of(m + 1), b1, True, last_from)


def flash_fwd(name, q, k, v2):
    nh, s, _ = q.shape
    tq, tk = _tile(s, TQ), _tile(s, TC)
    assert tq == 2 * tk
    nt = tk // LANES

    def body(q_ref, k_ref, v_ref, o_ref, lse_ref, s0_sc, s1_sc, p_sc, m_sc, acc):
        i = pl.program_id(1)
        m_sc[...] = jnp.full((tq, LANES), NEG, F32)
        acc[...] = jnp.zeros((tq, HD), F32)

        def scores(j, dst, row0):
            dst[row0:, :] = _dot(q_ref[row0:, :], k_ref[pl.ds(pl.multiple_of(j * tk, tk), tk), :], NT)

        def softmax_rows(src, r0, j, masked):
            rows = pl.ds(r0, RB)

            def tile(t):
                x = src[rows, t * LANES:(t + 1) * LANES]
                return _causal(x, i * tq + r0, j * tk + t * LANES) if masked else x

            mx = tile(0)
            for t in range(1, nt):
                mx = jnp.maximum(mx, tile(t))
            m_prev = m_sc[rows, :]
            m_new = jnp.maximum(m_prev, jnp.max(mx, axis=-1, keepdims=True))
            alpha = jnp.exp2(m_prev - m_new)
            for t in range(nt):
                p_sc[rows, t * LANES:(t + 1) * LANES] = jnp.exp2(tile(t) - m_new).astype(BF16)
            m_sc[rows, :] = m_new
            acc[rows, 0:DN] = alpha * acc[rows, 0:DN]
            acc[rows, DN:HD] = alpha * acc[rows, DN:HD]

        def finish(j, src, masked, row0):
            for r in range(row0 // RB, tq // RB):
                softmax_rows(src, r * RB, j, masked)
            acc[row0:, :] += _dot(p_sc[row0:, :], v_ref[pl.ds(pl.multiple_of(j * tk, tk), tk), :], NN)

        _chunk_pipeline(i, lambda m: m, scores, finish, (s0_sc, s1_sc), tk)
        l = acc[:, DN:HD]
        o_ref[...] = (acc[:, 0:DN] / l).astype(BF16)
        lse_ref[...] = m_sc[...] + jnp.log2(l)

    whole = pl.BlockSpec((None, s, HD), lambda hh, i: (hh, 0, 0))
    return pl.pallas_call(
        body, name=name, grid=(nh, s // tq),
        in_specs=[pl.BlockSpec((None, tq, HD), lambda hh, i: (hh, i, 0)), whole, whole],
        out_specs=[pl.BlockSpec((tq, DN), lambda hh, i: (i, hh)),
                   pl.BlockSpec((None, tq, DN), lambda hh, i: (hh, i, 0))],
        out_shape=[jax.ShapeDtypeStruct((s, nh * DN), BF16), jax.ShapeDtypeStruct((nh, s, DN), F32)],
        scratch_shapes=[pltpu.VMEM((tq, tk), F32), pltpu.VMEM((tq, tk), F32), pltpu.VMEM((tq, tk), BF16),
                        pltpu.VMEM((tq, LANES), F32), pltpu.VMEM((tq, HD), F32)],
        compiler_params=_flash_cp(),
    )(q, k, v2)


def attn_do2(name, o, do, nh):
    s = o.shape[0]
    tm = _tile(s, 512)

    def body(o_ref, do_ref, d_ref):
        lane = lax.broadcasted_iota(jnp.int32, (tm, DN), 1)
        for hh in range(nh):
            cols = slice(hh * DN, (hh + 1) * DN)
            dov = do_ref[:, cols]
            delta = jnp.sum(o_ref[:, cols].astype(F32) * dov.astype(F32), axis=-1, keepdims=True)
            hi = delta.astype(BF16).astype(F32)
            lo = delta - hi
            d_ref[hh, :, 0:DN] = dov
            d_ref[hh, :, DN:HD] = jnp.where(lane == 0, -hi, jnp.where(lane == 1, -lo, 0.0)).astype(BF16)

    blk = pl.BlockSpec((tm, nh * DN), lambda i: (i, 0))
    return pl.pallas_call(
        body, name=name, grid=(s // tm,), in_specs=[blk, blk],
        out_specs=pl.BlockSpec((nh, tm, HD), lambda i: (0, i, 0)),
        out_shape=jax.ShapeDtypeStruct((nh, s, HD), BF16), compiler_params=_cp("parallel"),
    )(o, do)


def q_heads_bwd(name, dq_acc, cs, w256, cq):
    nh, s, _ = dq_acc.shape
    tm = _tile(s, 1024)
    ns = s // tm

    def body(a_ref, cs_ref, w_ref, cq_ref, dcq_ref, dw_ref, dw_acc):
        i, hh = pl.program_id(0), pl.program_id(1)

        @pl.when(hh == 0)
        def _():
            dcq_ref[...] = jnp.zeros((tm, HD), F32)

        @pl.when(i == 0)
        def _():
            dw_acc[hh] = jnp.zeros((HD, HD), F32)

        draw = jnp.concatenate(
            [(a_ref[:, 0:DN] * ATT_SCALE).astype(BF16),
             _rope128_bwd(a_ref[:, DN:HD] * ATT_SCALE, cs_ref[...]).astype(BF16)], axis=-1)
        dcq_ref[...] += _dot(draw, w_ref[...], NT)
        dw_acc[hh] += _dot(cq_ref[...], draw, TN)

        @pl.when(i == ns - 1)
        def _():
            dw_ref[hh] = dw_acc[hh].astype(BF16)

    row = pl.BlockSpec((tm, HD), lambda i, hh: (i, 0))
    return pl.pallas_call(
        body, name=name, grid=(ns, nh),
        in_specs=[pl.BlockSpec((None, tm, HD), lambda i, hh: (hh, i, 0)), pl.BlockSpec((tm, DN), lambda i, hh: (i, 0)),
                  pl.BlockSpec((None, HD, HD), lambda i, hh: (hh, 0, 0)), row],
        out_specs=[row, pl.BlockSpec((nh, HD, HD), lambda i, hh: (0, 0, 0))],
        out_shape=[jax.ShapeDtypeStruct((s, HD), F32), jax.ShapeDtypeStruct((nh, HD, HD), BF16)],
        scratch_shapes=[pltpu.VMEM((nh, HD, HD), F32)],
        compiler_params=_cp("arbitrary", "arbitrary"),
    )(dq_acc, cs, w256, cq)


def flash_bwd(name, q, k, v2, do2, lse_row, prev=None):
    nh, s, _ = q.shape
    tk, tq = _tile(s, TQ), _tile(s, TC)
    nqc = s // tq
    per = tk // tq
    nt = tq // LANES
    nkt = s // tk
    has_prev = prev is not None

    def body(*refs):
        k_ref, v_ref, q_ref, do_ref, lse_ref = refs[:5]
        dk_ref, dv_ref, dq_hbm, s_sc, d_sc, p_sc, ds_sc, dk_acc, dv_acc, dq_sc, sem = refs[-11:]
        hh, i = pl.program_id(0), pl.program_id(1)
        dk_acc[...] = jnp.zeros((tk, HD), F32)
        dv_acc[...] = jnp.zeros((tk, DN), F32)

        @pl.when(i == 0)
        def _():
            dq_sc[...] = jnp.zeros((s, HD), F32)

        def p_rows(r0, j, masked):
            rows = pl.ds(r0, RB)
            for t in range(nt):
                cols = slice(t * LANES, (t + 1) * LANES)
                x = s_sc[rows, cols]
                if masked:
                    x = _causal(x, i * tk + r0, j * tq + t * LANES, keys_on_rows=True)
                p = jnp.exp2(x - jnp.tile(lse_ref[j][:, cols], (RB // 8, 1)))
                p_sc[rows, cols] = p.astype(BF16)
                ds_sc[rows, cols] = (p * d_sc[rows, cols]).astype(BF16)

        def step(j, masked, nk):
            start = pl.multiple_of(j * tq, tq)
            qj = q_ref[pl.ds(start, tq), :]
            kt = k_ref[0:nk, :]
            s_sc[0:nk, :] = _dot(kt, qj, NT)
            d_sc[0:nk, :] = _dot(v_ref[0:nk, :], do_ref[pl.ds(start, tq), :], NT)
            for r in range(nk // RB):
                p_rows(r * RB, j, masked)
            dv_acc[0:nk, :] += _dot(p_sc[0:nk, :], do_ref[pl.ds(start, tq), 0:DN], NN)
            ds = ds_sc[0:nk, :]
            dk_acc[0:nk, :] += _dot(ds, qj, NN)
            dq_sc[pl.ds(start, tq), :] += _dot(ds, kt, TN)

        def full_step(m, carry):
            step(nqc - 1 - m, False, tk)
            return carry

        lax.fori_loop(0, nqc - (i + 1) * per, full_step, 0)
        for dd in reversed(range(per)):
            step(i * per + dd, True, (dd + 1) * tq)
        if has_prev:
            dk_ref[...] = dk_acc[...] * LN2 + refs[5][...]
            dv_ref[...] = dv_acc[...] + refs[6][...]
        else:
            dk_ref[...] = dk_acc[...] * LN2
            dv_ref[...] = dv_acc[...]

        @pl.when(i == nkt - 1)
        def _():
            cp = pltpu.make_async_copy(dq_sc, dq_hbm.at[hh], sem)
            cp.start()
            cp.wait()

    kblk = pl.BlockSpec((None, tk, HD), lambda hh, i: (hh, i, 0))
    vblk = pl.BlockSpec((None, tk, DN), lambda hh, i: (hh, i, 0))
    whole = pl.BlockSpec((None, s, HD), lambda hh, i: (hh, 0, 0), pipeline_mode=pl.Buffered(1))
    in_specs = [kblk, kblk, whole, whole, pl.BlockSpec((None, nqc, 8, tq), lambda hh, i: (hh, 0, 0, 0))]
    args = [k, v2, q, do2, lse_row]
    if has_prev:
        in_specs += [kblk, vblk]
        args += list(prev)
    big = jax.ShapeDtypeStruct((nh, s, HD), F32)
    return pl.pallas_call(
        body, name=name, grid=(nh, nkt), in_specs=in_specs, out_specs=[kblk, vblk, ANY],
        out_shape=[big, jax.ShapeDtypeStruct((nh, s, DN), F32), big],
        scratch_shapes=[pltpu.VMEM((tk, tq), F32)] * 2 + [pltpu.VMEM((tk, tq), BF16)] * 2
        + [pltpu.VMEM((tk, HD), F32), pltpu.VMEM((tk, DN), F32), pltpu.VMEM((s, HD), F32), pltpu.SemaphoreType.DMA],
        compiler_params=pltpu.CompilerParams(dimension_semantics=("arbitrary", "arbitrary"), vmem_limit_bytes=BIG_VMEM),
    )(*args)


def loss_head(name, h, gain, target):
    s, d = h.shape
    tm = _tile(s, 1024)

    def body(h_ref, g_ref, t_ref, sq_ref, dh_ref, dg_ref):
        @pl.when(pl.program_id(0) == 0)
        def _():
            sq_ref[...] = jnp.zeros((1, d), F32)
            dg_ref[...] = jnp.zeros((1, d), F32)

        xv = h_ref[...]
        g = g_ref[...]
        rinv = _rinv(xv)
        xhat = xv * rinv
        err = xhat * g - t_ref[...]
        sq_ref[...] += jnp.sum(err * err, axis=0, keepdims=True)
        dy = err / d
        dg_ref[...] += jnp.sum(dy * xhat, axis=0, keepdims=True)
        dxh = dy * g
        dh_ref[...] = rinv * (dxh - xhat * jnp.mean(dxh * xhat, axis=-1, keepdims=True))

    row = pl.BlockSpec((tm, d), lambda i: (i, 0))
    vec = pl.BlockSpec((1, d), lambda i: (0, 0))
    return pl.pallas_call(
        body, name=name, grid=(s // tm,), in_specs=[row, vec, row], out_specs=[vec, row, vec],
        out_shape=[jax.ShapeDtypeStruct((1, d), F32), jax.ShapeDtypeStruct((s, d), F32),
                   jax.ShapeDtypeStruct((1, d), F32)],
        compiler_params=_cp("arbitrary"),
    )(h, gain, target)


def _adamw_math(w, g, m, v):
    m2 = ADAM_B1 * m + (1.0 - ADAM_B1) * g
    v2 = ADAM_B2 * v + (1.0 - ADAM_B2) * (g * g)
    mh = m2 / (1.0 - ADAM_B1 ** ADAM_STEP)
    vh = v2 / (1.0 - ADAM_B2 ** ADAM_STEP)
    delta = -ADAM_LR * (mh / (jnp.sqrt(vh) + ADAM_EPS) + ADAM_WD * w)
    return delta, m2, v2


def adamw(name, parts, w, m, v):
    rows, cols = w.shape
    tr = _tile(rows, max(8, (1 << 18) // cols))
    stacked = parts[0].ndim == 3
    npart = len(parts)

    def body(*refs):
        p_refs = refs[:npart]
        w_ref, m_ref, v_ref, g_ref, d_ref, m2_ref, v2_ref = refs[npart:]
        if stacked:
            g = p_refs[0][0]
            for n in range(1, parts[0].shape[0]):
                g = g + p_refs[0][n]
        else:
            g = p_refs[0][...]
            for r in p_refs[1:]:
                g = g + r[...]
        delta, m2, v2 = _adamw_math(w_ref[...], g, m_ref[...], v_ref[...])
        g_ref[...] = g
        d_ref[...] = delta
        m2_ref[...] = m2
        v2_ref[...] = v2

    blk = pl.BlockSpec((tr, cols), lambda i: (i, 0))
    pblk = pl.BlockSpec((parts[0].shape[0], tr, cols), lambda i: (0, i, 0)) if stacked else blk
    sds = jax.ShapeDtypeStruct((rows, cols), F32)
    return pl.pallas_call(
        body, name=name, grid=(rows // tr,), in_specs=[pblk] * npart + [blk] * 3,
        out_specs=[blk] * 4, out_shape=[sds] * 4, compiler_params=_cp("parallel"),
    )(*parts, w, m, v)


def sum4(name, x):
    _, a, r, c = x.shape
    tr = _tile(r, max(8, (1 << 18) // c))

    def body(x_ref, o_ref):
        o_ref[...] = ((x_ref[3].astype(F32) + x_ref[0].astype(F32)) + x_ref[1].astype(F32)) + x_ref[2].astype(F32)

    return pl.pallas_call(
        body, name=name, grid=(a, r // tr),
        in_specs=[pl.BlockSpec((4, None, tr, c), lambda i, j: (0, i, j, 0))],
        out_specs=pl.BlockSpec((None, tr, c), lambda i, j: (i, j, 0)),
        out_shape=jax.ShapeDtypeStruct((a, r, c), F32), compiler_params=_cp("parallel", "parallel"),
    )(x)


ANY = pl.BlockSpec(memory_space=pl.ANY)


def _mesh_pos():
    return lax.axis_index("x"), lax.axis_index("y"), lax.axis_index("c")


def _chip_peers(x, y):
    return [(1 - x, y), (x, 1 - y), (1 - x, 1 - y)]


class _NoCarry:
    arrays = ()

    def split(self, refs, n_in, n_out, n_scratch):
        return (refs[:n_in], (), refs[n_in:n_in + n_out], (), refs[n_in + n_out:n_in + n_out + n_scratch], ())

    def in_specs(self):
        return []

    out_specs = out_shapes = scratch = in_specs

    def at_first_step(self, grid, ins, outs, sems):
        pass

    at_last_step = at_first_step


class ChipCopies:
    def __init__(self, kind, arrays):
        assert kind in ("gather", "exchange")
        self.kind, self.arrays, self.n = kind, tuple(arrays), len(arrays)

    def split(self, refs, n_in, n_out, n_scratch):
        n, pos, parts = self.n, 0, []
        for count in (n_in, n, n_out, n, n_scratch, 3):
            parts.append(refs[pos:pos + count])
            pos += count
        return parts

    def in_specs(self):
        return [ANY] * self.n

    out_specs = in_specs

    def out_shapes(self):
        if self.kind == "gather":
            return [jax.ShapeDtypeStruct((a.shape[0], N_CHIPS) + a.shape[1:], a.dtype) for a in self.arrays]
        return [jax.ShapeDtypeStruct((N_CHIPS, a.shape[0]) + a.shape[2:], a.dtype) for a in self.arrays]

    def scratch(self):
        n = self.n
        return [pltpu.SemaphoreType.DMA((3 * n,)), pltpu.SemaphoreType.DMA((3 * n,)), pltpu.SemaphoreType.DMA((n,))]

    def _copies(self, ins, outs, sems):
        send_sems, recv_sems, loc_sems = sems
        x, y, c = _mesh_pos()
        me = 2 * x + y
        peers = _chip_peers(x, y)
        gather = self.kind == "gather"

        def remote(a, j, block):
            px, py = peers[j]
            src = ins[a] if gather else ins[a].at[:, 2 * px + py]
            dst = outs[a].at[:, block] if gather else outs[a].at[j]
            return pltpu.make_async_remote_copy(
                src_ref=src, dst_ref=dst, send_sem=send_sems.at[a * 3 + j], recv_sem=recv_sems.at[a * 3 + j],
                device_id=(px, py, c), device_id_type=MESH)

        pairs = [(a, j) for a in range(self.n) for j in range(3)]
        if gather:
            local = [pltpu.make_async_copy(ins[a], outs[a].at[:, me], loc_sems.at[a]) for a in range(self.n)]
        else:
            local = [pltpu.make_async_copy(ins[a].at[:, me], outs[a].at[3], loc_sems.at[a]) for a in range(self.n)]
        sends = [remote(a, j, me) for a, j in pairs]
        recvs = [remote(a, j, 2 * peers[j][0] + peers[j][1]) for a, j in pairs]
        return local, sends, recvs

    def start(self, ins, outs, sems):
        local, sends, _ = self._copies(ins, outs, sems)
        for cp in local + sends:
            cp.start()

    def wait(self, ins, outs, sems):
        local, sends, recvs = self._copies(ins, outs, sems)
        for cp in recvs:
            cp.wait_recv()
        for cp in sends:
            cp.wait_send()
        for cp in local:
            cp.wait()

    @staticmethod
    def _is_step(grid, last):
        cond = None
        for ax, size in enumerate(grid):
            c = pl.program_id(ax) == (size - 1 if last else 0)
            cond = c if cond is None else cond & c
        return cond

    def at_first_step(self, grid, ins, outs, sems):
        @pl.when(self._is_step(grid, False))
        def _():
            self.start(ins, outs, sems)

    def at_last_step(self, grid, ins, outs, sems):
        @pl.when(self._is_step(grid, True))
        def _():
            self.wait(ins, outs, sems)


def chip_copies(name, copies):
    n = copies.n

    def body(*refs):
        ins, outs, sems = refs[:n], refs[n:2 * n], refs[2 * n:]
        copies.start(ins, outs, sems)
        copies.wait(ins, outs, sems)

    return pl.pallas_call(
        body, name=name, in_specs=copies.in_specs(), out_specs=copies.out_specs(),
        out_shape=copies.out_shapes(), scratch_shapes=copies.scratch(),
    )(*copies.arrays)


def sibling_exchange(name, arrs):
    n = len(arrs)

    def body(*refs):
        ins, outs = refs[:n], refs[n:2 * n]
        send_sems, recv_sems = refs[2 * n:]
        x, y, c = _mesh_pos()
        copies = [pltpu.make_async_remote_copy(
            src_ref=ins[a], dst_ref=outs[a], send_sem=send_sems.at[a], recv_sem=recv_sems.at[a],
            device_id=(x, y, 1 - c), device_id_type=MESH) for a in range(n)]
        for cp in copies:
            cp.start()
        for cp in copies:
            cp.wait_recv()
        for cp in copies:
            cp.wait_send()

    return pl.pallas_call(
        body, name=name, in_specs=[ANY] * n, out_specs=[ANY] * n,
        out_shape=[jax.ShapeDtypeStruct(a.shape, a.dtype) for a in arrs],
        scratch_shapes=[pltpu.SemaphoreType.DMA((n,)), pltpu.SemaphoreType.DMA((n,))],
    )(*arrs)


def all_gather8(name, buf):
    r, cdim = buf.shape

    def body(in_ref, out_ref, send_sems, recv_sems, loc_sem):
        x, y, c = _mesh_pos()
        flips = [(fx, fy, fc) for fx in (0, 1) for fy in (0, 1) for fc in (0, 1)][1:]

        def peer(f):
            return tuple((1 - p) if b else p for p, b in zip((x, y, c), f))

        def remote(j, slot):
            return pltpu.make_async_remote_copy(
                src_ref=in_ref, dst_ref=out_ref.at[slot], send_sem=send_sems.at[j], recv_sem=recv_sems.at[j],
                device_id=peer(flips[j]), device_id_type=MESH)

        me = 4 * x + 2 * y + c
        local = pltpu.make_async_copy(in_ref, out_ref.at[me], loc_sem)
        local.start()
        sends = [remote(j, me) for j in range(7)]
        for cp in sends:
            cp.start()
        for j in range(7):
            px, py, pc = peer(flips[j])
            remote(j, 4 * px + 2 * py + pc).wait_recv()
        for cp in sends:
            cp.wait_send()
        local.wait()

    return pl.pallas_call(
        body, name=name, in_specs=[ANY], out_specs=ANY,
        out_shape=jax.ShapeDtypeStruct((8, r, cdim), buf.dtype),
        scratch_shapes=[pltpu.SemaphoreType.DMA((7,)), pltpu.SemaphoreType.DMA((7,)), pltpu.SemaphoreType.DMA],
    )(buf)


def _rot_cols(w):
    half = w.shape[-1] // 2
    return jnp.concatenate([-w[..., half:], w[..., :half]], axis=-1)


def _fold_rot(g):
    rot = g[..., DN + DR:]
    half = DR // 2
    return jnp.concatenate([g[..., :DN], g[..., DN:DN + DR] + jnp.concatenate([rot[..., half:], -rot[..., :half]], -1)], -1)


def _with_rot(w):
    return jnp.concatenate([w, _rot_cols(w[..., DN:])], axis=-1)


def _rope_table(s):
    pos = jnp.arange(s, dtype=F32)
    inv_freq = ROPE_THETA ** (-jnp.arange(0, DR, 2, dtype=F32) / DR)
    ang = pos[:, None] * inv_freq[None, :]
    cos, sin = jnp.cos(ang), jnp.sin(ang)
    return jnp.concatenate([cos, cos, sin, sin], axis=-1)


def _row_stats(x, tq):
    nh, s, _ = x.shape
    return jnp.broadcast_to(x[:, :, 0].reshape(nh, s // tq, 1, tq), (nh, s // tq, 8, tq))


REPL_NAMES = ["ffn_pre_norm", "mix_norm", "ffn_post_norm", "kv_in_norm", "final_norm", "ckv_norm",
              "q_lora_norm", "pool_scale_full", "w_uk", "w_uv"]
PACK_COLS = 1024


def _pack_rows(arrs):
    rows, counts = [], []
    for a in arrs:
        flat = a.reshape(-1).astype(F32)
        n = -(-flat.shape[0] // PACK_COLS)
        flat = jnp.pad(flat, (0, n * PACK_COLS - flat.shape[0]))
        rows.append(flat.reshape(n, PACK_COLS))
        counts.append(n)
    total = sum(counts)
    pad = -total % 8
    if pad:
        rows.append(jnp.zeros((pad, PACK_COLS), F32))
    return jnp.concatenate(rows, axis=0), counts


def _unpack_rows(buf, counts, shapes):
    out, r = [], 0
    for n, shp in zip(counts, shapes):
        size = 1
        for dd in shp:
            size *= dd
        out.append(buf[r:r + n].reshape(-1)[:size].reshape(shp))
        r += n
    return out


def kernel(x, ffn_pre_norm, ffn_pre_wg, ffn_pre_wu, ffn_pre_wd, mix_norm, ffn_post_norm, ffn_post_wg, ffn_post_wu, ffn_post_wd, pool_w, pool_scale, kv_in_norm, w_dkv, ckv_norm, w_uk, w_uv, q_lora_norm, w_dq, w_uq, w_o, final_norm, loss_target, m_ffn_pre_norm, m_ffn_pre_wg, m_ffn_pre_wu, m_ffn_pre_wd, m_mix_norm, m_ffn_post_norm, m_ffn_post_wg, m_ffn_post_wu, m_ffn_post_wd, m_pool_w, m_pool_scale, m_kv_in_norm, m_w_dkv, m_ckv_norm, m_w_uk, m_w_uv, m_q_lora_norm, m_w_dq, m_w_uq, m_w_o, m_final_norm, v_ffn_pre_norm, v_ffn_pre_wg, v_ffn_pre_wu, v_ffn_pre_wd, v_mix_norm, v_ffn_post_norm, v_ffn_post_wg, v_ffn_post_wu, v_ffn_post_wd, v_pool_w, v_pool_scale, v_kv_in_norm, v_w_dkv, v_ckv_norm, v_w_uk, v_w_uv, v_q_lora_norm, v_w_dq, v_w_uq, v_w_o, v_final_norm):
    weights = dict(ffn_pre_norm=ffn_pre_norm, ffn_pre_wg=ffn_pre_wg, ffn_pre_wu=ffn_pre_wu, ffn_pre_wd=ffn_pre_wd,
                   mix_norm=mix_norm, ffn_post_norm=ffn_post_norm, ffn_post_wg=ffn_post_wg, ffn_post_wu=ffn_post_wu,
                   ffn_post_wd=ffn_post_wd, pool_w=pool_w, pool_scale=pool_scale, kv_in_norm=kv_in_norm, w_dkv=w_dkv,
                   ckv_norm=ckv_norm, w_uk=w_uk, w_uv=w_uv, q_lora_norm=q_lora_norm, w_dq=w_dq, w_uq=w_uq, w_o=w_o,
                   final_norm=final_norm)
    mom1 = dict(ffn_pre_norm=m_ffn_pre_norm, ffn_pre_wg=m_ffn_pre_wg, ffn_pre_wu=m_ffn_pre_wu, ffn_pre_wd=m_ffn_pre_wd,
                mix_norm=m_mix_norm, ffn_post_norm=m_ffn_post_norm, ffn_post_wg=m_ffn_post_wg, ffn_post_wu=m_ffn_post_wu,
                ffn_post_wd=m_ffn_post_wd, pool_w=m_pool_w, pool_scale=m_pool_scale, kv_in_norm=m_kv_in_norm,
                w_dkv=m_w_dkv, ckv_norm=m_ckv_norm, w_uk=m_w_uk, w_uv=m_w_uv, q_lora_norm=m_q_lora_norm, w_dq=m_w_dq,
                w_uq=m_w_uq, w_o=m_w_o, final_norm=m_final_norm)
    mom2 = dict(ffn_pre_norm=v_ffn_pre_norm, ffn_pre_wg=v_ffn_pre_wg, ffn_pre_wu=v_ffn_pre_wu, ffn_pre_wd=v_ffn_pre_wd,
                mix_norm=v_mix_norm, ffn_post_norm=v_ffn_post_norm, ffn_post_wg=v_ffn_post_wg, ffn_post_wu=v_ffn_post_wu,
                ffn_post_wd=v_ffn_post_wd, pool_w=v_pool_w, pool_scale=v_pool_scale, kv_in_norm=v_kv_in_norm,
                w_dkv=v_w_dkv, ckv_norm=v_ckv_norm, w_uk=v_w_uk, w_uv=v_w_uv, q_lora_norm=v_q_lora_norm, w_dq=v_w_dq,
                w_uq=v_w_uq, w_o=v_w_o, final_norm=v_final_norm)
    names = list(weights)

    h0 = x[0]
    target = loss_target[0]
    s, d = h0.shape
    nh = N_HEADS
    nb = DEPTH - N_A
    f_loc = ffn_pre_wd.shape[1]
    ffn_dim = N_CHIPS * f_loc
    my_chip = 2 * lax.axis_index("x") + lax.axis_index("y")

    ffn_in = {"pre": (ffn_pre_norm, ffn_pre_wg, ffn_pre_wu, ffn_pre_wd),
              "post": (ffn_post_norm, ffn_post_wg, ffn_post_wu, ffn_post_wd)}
    uq_loc = jnp.transpose(w_uq, (0, 2, 1, 3))
    shard = {
        "pool_w": pool_w.reshape(N_A * 4, PG // N_CHIPS, PG).astype(BF16),
        "pool_scale": pool_scale.reshape(N_A, 1, PG),
        "wdkv": _with_rot(w_dkv)[None].astype(BF16),
        "wdq": w_dq.astype(BF16),
        "wq": _with_rot(uq_loc).reshape(nb * nh, uq_loc.shape[2], HD).astype(BF16),
        "wo": w_o.astype(BF16),
    }
    for part, (_, wg_, wu_, wd_) in ffn_in.items():
        for l in range(DEPTH):
            shard[f"{part}_wg{l}"] = jnp.transpose(wg_[l]).astype(BF16)[None]
            shard[f"{part}_wu{l}"] = jnp.transpose(wu_[l]).astype(BF16)[None]
            shard[f"{part}_wd{l}"] = wd_[l].astype(BF16)[None]
    ffn_names = lambda part, l: [f"{part}_wg{l}", f"{part}_wu{l}", f"{part}_wd{l}"]
    gather_plan = {
        None: ffn_names("pre", 0),
        ("pre", 0): ["pool_w", "pool_scale"] + ffn_names("post", 0) + ffn_names("pre", 1),
        ("post", 0): ffn_names("post", 1) + ["wdkv"] + ffn_names("pre", 2),
        ("pre", 1): ffn_names("post", 2) + ["wdq", "wq", "wo"],
        ("post", 1): ffn_names("pre", 3) + ffn_names("post", 3),
    }
    full = {}

    def absorb(plan_names, outs):
        for n, g in zip(plan_names, outs):
            full[n] = g.reshape(g.shape[0], N_CHIPS * g.shape[2], g.shape[3])

    absorb(gather_plan[None], chip_copies("gather_first", ChipCopies("gather", [shard[n] for n in gather_plan[None]])))
    wuk_h = jnp.transpose(w_uk, (1, 0, 2)).astype(BF16)
    wuv_h = jnp.transpose(w_uv, (1, 0, 2)).astype(BF16)
    cs = _rope_table(s)

    def ffn_forward(part, l, hin):
        plan_names = gather_plan.get((part, l))
        carry = ChipCopies("gather", [shard[n] for n in plan_names]) if plan_names else None
        wg_, wu_, wd_ = [full[n][0] for n in ffn_names(part, l)]
        hout, carried = ffn_fwd(f"ffn_{part}_fwd{l}", hin, ffn_in[part][0][l:l + 1], wg_, wu_, wd_, carry)
        if plan_names:
            absorb(plan_names, carried)
        return hout

    saved = {}
    h = h0
    kv = None
    for l in range(DEPTH):
        saved["a", l] = h
        h = ffn_forward("pre", l, h)
        saved["b", l] = h
        if l == 0:
            pool_w_full = full["pool_w"].reshape(N_A, 4, PG, PG)
            pool_scale_full = full["pool_scale"].reshape(N_A, d)
        if l < N_A:
            h, y = pool_fwd(f"pool_fwd{l}", h, mix_norm[l:l + 1], pool_w_full, pool_scale_full[l:l + 1], l)
            saved["y", l] = y
        else:
            j = l - N_A
            u = rms_fwd(f"mix_norm_fwd{l}", h, mix_norm[l:l + 1])
            cq_raw = mm_nn(f"q_down{l}", u, wdq_full[j], F32, 2048, 256, 1024)
            cq = rms_fwd(f"q_norm_fwd{l}", cq_raw, q_lora_norm[j:j + 1])
            q = q_heads(f"q_heads{l}", cq, wq256[j], cs)
            o, lse = flash_fwd(f"flash_fwd{l}", q, kv[0], kv[1])
            saved["att", l] = (u, cq_raw, cq, q, o, lse)
            h = mm_nn(f"attn_out{l}", o, wo_full[j], F32, 1024, 1024, 1024, res=h)
        saved["c", l] = h
        h = ffn_forward("post", l, h)
        if l == N_A - 1:
            wdkv256 = full["wdkv"][0]
            wdq_full = full["wdq"]
            wq256 = full["wq"].reshape(nb, nh, HD, HD)
            wo_full = full["wo"]
            u_kv = rms_fwd("kv_in_norm_fwd", h, kv_in_norm[None])
            kv_raw = mm_nn("kv_down", u_kv, wdkv256, F32, 2048, 256, 1024)
            c_kv, kr = kv_post("kv_post", kv_raw, ckv_norm[None], cs)
            kv = kv_heads("kv_heads", c_kv, kr, wuk_h, wuv_h)
            saved["kv"] = (u_kv, kv_raw, c_kv)

    sq, dh, g_final = loss_head("loss_head", h, final_norm[None], target)
    loss = lax.psum(0.5 * jnp.sum(sq) / d, ("x", "y", "c"))

    grads = {}
    gvec = {n: [None] * DEPTH for n in ("ffn_pre_norm", "mix_norm", "ffn_post_norm")}
    g_pool_scale, g_qnorm = [None] * N_A, [None] * nb
    dkv = None
    gfull, partial = {}, {}
    split = lambda g: g.reshape(g.shape[0], N_CHIPS, g.shape[1] // N_CHIPS, g.shape[2])

    def layer_grad_names(l):
        extra = [f"pool_w{l}"] if l < N_A else [f"wdq{l - N_A}", f"wq{l - N_A}", f"wo{l - N_A}"]
        return ffn_names("pre", l) + ffn_names("post", l) + extra + (["wdkv"] if l == N_A - 1 else [])

    def chip_sums(plan_names, received):
        for n, r in zip(plan_names, received):
            partial[n] = sum4(f"chip_sum_{n}", r)

    def ffn_backward(part, l, hin, dh):
        plan_names = layer_grad_names(l + 1)[(0 if part == "post" else 1)::2] if l + 1 < DEPTH else []
        if part == "pre" and l == 0:
            plan_names = plan_names + layer_grad_names(0)[3:]
        carry = ChipCopies("exchange", [gfull[n] for n in plan_names]) if plan_names else None
        wg_, wu_, wd_ = [full[n][0] for n in ffn_names(part, l)]
        (dh, u, dob, act, da, db, dgain), received = ffn_bwd(
            f"ffn_{part}_bwd{l}", hin, ffn_in[part][0][l:l + 1], wg_, wu_, wd_, dh, carry)
        if plan_names:
            chip_sums(plan_names, received)
        gvec[f"ffn_{part}_norm"][l] = dgain[0]
        for n, lhs, rhs in zip(ffn_names(part, l), (da, db, act), (u, u, dob)):
            gfull[n] = split(mm_tn(f"d_{n}", lhs, rhs, BF16, ffn_dim // 2, d, 1024)[None])
        return dh

    for l in reversed(range(DEPTH)):
        if l == N_A - 1:
            u_kv, kv_raw, c_kv = saved["kv"]
            dk, dv = dkv
            dc, dkr, dwuk, dwuv = kv_heads_bwd("kv_heads_bwd", dk, dv, c_kv, wuk_h, wuv_h)
            grads["w_uk"] = jnp.transpose(dwuk, (1, 0, 2))
            grads["w_uv"] = jnp.transpose(dwuv, (1, 0, 2))
            dkv_raw, dg_ckv = kv_post_bwd("kv_post_bwd", kv_raw, ckv_norm[None], dc, dkr, cs)
            grads["ckv_norm"] = dg_ckv[0]
            du_kv = mm_nt("kv_du", dkv_raw, wdkv256, F32, 2048, 1024, 256)
            gfull["wdkv"] = split(mm_tn("kv_dwdkv", u_kv, dkv_raw, BF16, 1024, 256, 2048)[None])
            dh, dg_kvin = rms_bwd("kv_in_norm_bwd", saved["a", l + 1], kv_in_norm[None], du_kv, dres=dh)
            grads["kv_in_norm"] = dg_kvin[0]

        dh = ffn_backward("post", l, saved["c", l], dh)

        if l < N_A:
            dh, dz, dgain, dscale = pool_bwd(f"pool_bwd{l}", saved["b", l], mix_norm[l:l + 1], saved["y", l],
                                             pool_w_full, pool_scale_full[l:l + 1], l, dh)
            gvec["mix_norm"][l] = dgain[0]
            g_pool_scale[l] = dscale[0]
            tk = _tile(s, 2048)
            gfull[f"pool_w{l}"] = split(_mm(
                f"pool_dw{l}", saved["y", l], dz, grid=(4, 1, s // tk),
                a_block=(tk, PG), a_map=lambda g, jj, kk: (kk, g), b_block=(tk, PG), b_map=lambda g, jj, kk: (kk, g),
                o_block=(None, PG, PG), o_map=lambda g, jj, kk: (g, 0, 0),
                out_shape=(4, PG, PG), dims=TN, out_dtype=BF16))
        else:
            j = l - N_A
            u, cq_raw, cq, q, o, lse = saved["att", l]
            k_, v_ = kv
            do = mm_nt(f"attn_do{l}", dh, wo_full[j], BF16, 1024, 1024, 1024)
            gfull[f"wo{j}"] = split(mm_tn(f"attn_dwo{l}", o, dh, BF16, 1024, 1024, 1024)[None])
            do2 = attn_do2(f"attn_do2_{l}", o, do, nh)
            dk_, dv_, dq_acc = flash_bwd(f"flash_bwd{l}", q, k_, v_, do2, _row_stats(lse, _tile(s, TC)), prev=dkv)
            dkv = (dk_, dv_)
            dcq, dwq = q_heads_bwd(f"q_heads_bwd{l}", dq_acc, cs, wq256[j], cq)
            gfull[f"wq{j}"] = split(dwq)
            dcq_raw, dg_q = rms_bwd(f"q_norm_bwd{l}", cq_raw, q_lora_norm[j:j + 1], dcq)
            g_qnorm[j] = dg_q[0]
            du = mm_nt(f"q_du{l}", dcq_raw, wdq_full[j], F32, 2048, 1024, 256)
            gfull[f"wdq{j}"] = split(mm_tn(f"q_dwdq{l}", u, dcq_raw, BF16, 1024, 256, 2048)[None])
            dh, dgain = rms_bwd(f"mix_norm_bwd{l}", saved["b", l], mix_norm[l:l + 1], du, dres=dh)
            gvec["mix_norm"][l] = dgain[0]

        dh = ffn_backward("pre", l, saved["a", l], dh)

    grad_x = dh[None]

    last_names = layer_grad_names(0)[:3]
    chip_sums(last_names, chip_copies("exchange_last", ChipCopies("exchange", [gfull[n] for n in last_names])))
    part_names = sorted(partial)
    sibling = dict(zip(part_names, sibling_exchange("exchange_sibling", [partial[n] for n in part_names])))
    pieces = {
        "pool_w": [f"pool_w{l}" for l in range(N_A)], "w_dkv": ["wdkv"], "w_dq": [f"wdq{j}" for j in range(nb)],
        "w_uq": [f"wq{j}" for j in range(nb)], "w_o": [f"wo{j}" for j in range(nb)],
    }
    for part in ("pre", "post"):
        for m in ("wg", "wu", "wd"):
            pieces[f"ffn_{part}_{m}"] = [f"{part}_{m}{l}" for l in range(DEPTH)]

    def natural(name, src):
        p = jnp.concatenate([src[n] for n in pieces[name]], axis=0)
        if name in ("ffn_pre_wg", "ffn_pre_wu", "ffn_post_wg", "ffn_post_wu"):
            return jnp.transpose(p, (0, 2, 1))
        if name == "w_dkv":
            return _fold_rot(p[0])
        if name == "w_uq":
            return jnp.transpose(_fold_rot(p).reshape(nb, nh, p.shape[1], QK_DIM), (0, 2, 1, 3))
        return p.reshape(weights[name].shape)

    results = {}
    for n in pieces:
        shp = weights[n].shape
        as2d = lambda a: a.reshape(-1, shp[-1])
        outs = adamw(f"adamw_{n}", [as2d(natural(n, partial)), as2d(natural(n, sibling))],
                     as2d(weights[n]), as2d(mom1[n]), as2d(mom2[n]))
        results[n] = [o_.reshape(shp) for o_ in outs]

    grads.update(ffn_pre_norm=jnp.stack(gvec["ffn_pre_norm"]), mix_norm=jnp.stack(gvec["mix_norm"]),
                 ffn_post_norm=jnp.stack(gvec["ffn_post_norm"]), final_norm=g_final[0],
                 q_lora_norm=jnp.stack(g_qnorm), pool_scale_full=jnp.stack(g_pool_scale))
    repl_shapes = [grads[n].shape for n in REPL_NAMES]
    packed_g, counts = _pack_rows([grads[n] for n in REPL_NAMES])
    all_g = all_gather8("gather_small_grads", packed_g)
    zeros_ps = jnp.zeros((N_A, d), F32)
    pick = lambda src: [zeros_ps if n == "pool_scale_full" else src[n] for n in REPL_NAMES]
    packed = [_pack_rows(pick(src))[0] for src in (weights, mom1, mom2)]
    outs = adamw("adamw_replicated", [all_g], *packed)
    unpacked = [_unpack_rows(o_, counts, repl_shapes) for o_ in outs]
    for idx, n in enumerate(REPL_NAMES):
        results[n] = [u_[idx] for u_ in unpacked]
    g_ps = lax.dynamic_index_in_dim(results["pool_scale_full"][0].reshape(N_A, N_CHIPS, PG), my_chip, axis=1, keepdims=False)
    results["pool_scale"] = adamw("adamw_pool_scale", [g_ps], pool_scale, m_pool_scale, v_pool_scale)

    out = [loss, grad_x]
    for part in range(4):
        out += [results[n][part] for n in names]
    return tuple(out)
```

```python
import functools

import jax
import jax.numpy as jnp
from jax import lax
from jax.experimental import pallas as pl
from jax.experimental.pallas import tpu as pltpu

F32 = jnp.float32
BF16 = jnp.bfloat16
MESH = pl.DeviceIdType.MESH

DEPTH = 4
N_A = 2
N_HEADS = 16
DN = 128
DR = 64
HD = 256
QK_DIM = DN + DR
POOL_WINDOWS = (2, 4, 8, 16)
PG = 256
HALO = 16
N_CHIPS = 4
RMS_EPS = 1e-6
ROPE_THETA = 10000.0
ATT_SCALE = QK_DIM ** -0.5
LOG2E = 1.4426950408889634
LN2 = 0.6931471805599453
Q_PRESCALE = ATT_SCALE * LOG2E
NEG = -1e30
LANES = 128

ADAM_LR = 0.001
ADAM_B1 = 0.9
ADAM_B2 = 0.999
ADAM_EPS = 1e-08
ADAM_WD = 0.01
ADAM_STEP = 10

VMEM_BYTES_V7X = 64 * 1024 * 1024
VMEM_LIMIT = VMEM_BYTES_V7X * 3 // 4
BIG_VMEM = VMEM_BYTES_V7X * 7 // 8
FFN_RB = 64

NN = ((1,), (0,))
NT = ((1,), (1,))
TN = ((0,), (0,))


def _cp(*sem):
    return pltpu.CompilerParams(dimension_semantics=sem, vmem_limit_bytes=VMEM_LIMIT)


def _dot(a, b, dims):
    return lax.dot_general(a, b, (dims, ((), ())), preferred_element_type=F32)


def _tile(n, pref):
    if n <= pref:
        return n
    if n % pref == 0:
        return pref
    t = 1 << (pref.bit_length() - 1)
    while n % t:
        t //= 2
    return t


def _rinv(x):
    return lax.rsqrt(jnp.mean(x * x, axis=-1, keepdims=True) + RMS_EPS)


def _sigmoid(a):
    return 1.0 / (1.0 + jnp.exp(-a))


def _mm(name, a, b, *, grid, a_block, a_map, b_block, b_map, o_block, o_map, out_shape, dims,
        out_dtype, res=None):
    nax = len(grid)
    nk = grid[-1]
    acc_shape = tuple(d for d in o_block if d is not None)
    has_res = res is not None

    def body(*refs):
        a_ref, b_ref = refs[0], refs[1]
        o_ref, acc = refs[-2], refs[-1]
        k = pl.program_id(nax - 1)

        @pl.when(k == 0)
        def _():
            acc[...] = jnp.zeros(acc_shape, F32)

        acc[...] += _dot(a_ref[...].astype(BF16), b_ref[...].astype(BF16), dims)

        @pl.when(k == nk - 1)
        def _():
            r = acc[...]
            if has_res:
                r = r + refs[2][...].astype(F32)
            o_ref[...] = r.astype(out_dtype)

    in_specs = [pl.BlockSpec(a_block, a_map), pl.BlockSpec(b_block, b_map)]
    args = [a, b]
    if has_res:
        in_specs.append(pl.BlockSpec(o_block, o_map))
        args.append(res)
    return pl.pallas_call(
        body, name=name, grid=grid, in_specs=in_specs, out_specs=pl.BlockSpec(o_block, o_map),
        out_shape=jax.ShapeDtypeStruct(out_shape, out_dtype),
        scratch_shapes=[pltpu.VMEM(acc_shape, F32)],
        compiler_params=_cp(*(("parallel",) * (nax - 1) + ("arbitrary",))),
    )(*args)


def mm_nn(name, a, b, out_dtype, tm, tn, tk, res=None):
    (m, kd), n = a.shape, b.shape[1]
    tm, tn, tk = _tile(m, tm), _tile(n, tn), _tile(kd, tk)
    return _mm(name, a, b, grid=(m // tm, n // tn, kd // tk),
               a_block=(tm, tk), a_map=lambda i, j, k: (i, k),
               b_block=(tk, tn), b_map=lambda i, j, k: (k, j),
               o_block=(tm, tn), o_map=lambda i, j, k: (i, j),
               out_shape=(m, n), dims=NN, out_dtype=out_dtype, res=res)


def mm_nt(name, a, b, out_dtype, tm, tn, tk):
    (m, kd), n = a.shape, b.shape[0]
    tm, tn, tk = _tile(m, tm), _tile(n, tn), _tile(kd, tk)
    return _mm(name, a, b, grid=(m // tm, n // tn, kd // tk),
               a_block=(tm, tk), a_map=lambda i, j, k: (i, k),
               b_block=(tn, tk), b_map=lambda i, j, k: (j, k),
               o_block=(tm, tn), o_map=lambda i, j, k: (i, j),
               out_shape=(m, n), dims=NT, out_dtype=out_dtype)


def mm_tn(name, a, b, out_dtype, tm, tn, tk):
    (kd, m), n = a.shape, b.shape[1]
    tm, tn, tk = _tile(m, tm), _tile(n, tn), _tile(kd, tk)
    return _mm(name, a, b, grid=(m // tm, n // tn, kd // tk),
               a_block=(tk, tm), a_map=lambda i, j, k: (k, i),
               b_block=(tk, tn), b_map=lambda i, j, k: (k, j),
               o_block=(tm, tn), o_map=lambda i, j, k: (i, j),
               out_shape=(m, n), dims=TN, out_dtype=out_dtype)


def rms_fwd(name, x, gain):
    s, d = x.shape
    tm = _tile(s, 1024)

    def body(x_ref, g_ref, o_ref):
        xv = x_ref[...]
        o_ref[...] = (xv * _rinv(xv) * g_ref[...]).astype(BF16)

    return pl.pallas_call(
        body, name=name, grid=(s // tm,),
        in_specs=[pl.BlockSpec((tm, d), lambda i: (i, 0)), pl.BlockSpec((1, d), lambda i: (0, 0))],
        out_specs=pl.BlockSpec((tm, d), lambda i: (i, 0)),
        out_shape=jax.ShapeDtypeStruct((s, d), BF16), compiler_params=_cp("parallel"),
    )(x, gain)


def _rms_bwd_math(xv, g, du):
    rinv = _rinv(xv)
    xhat = xv * rinv
    dgain = jnp.sum(du * xhat, axis=0, keepdims=True)
    dxh = du * g
    dx = rinv * (dxh - xhat * jnp.mean(dxh * xhat, axis=-1, keepdims=True))
    return dx, dgain


def rms_bwd(name, x, gain, du, dres=None):
    s, d = x.shape
    tm = _tile(s, 1024)
    has_res = dres is not None

    def body(*refs):
        x_ref, g_ref, du_ref = refs[:3]
        dx_ref, dg_ref = refs[-2:]

        @pl.when(pl.program_id(0) == 0)
        def _():
            dg_ref[...] = jnp.zeros((1, d), F32)

        dx, dgain = _rms_bwd_math(x_ref[...], g_ref[...], du_ref[...].astype(F32))
        if has_res:
            dx = dx + refs[3][...]
        dx_ref[...] = dx
        dg_ref[...] += dgain

    row = pl.BlockSpec((tm, d), lambda i: (i, 0))
    vec = pl.BlockSpec((1, d), lambda i: (0, 0))
    args = [x, gain, du] + ([dres] if has_res else [])
    return pl.pallas_call(
        body, name=name, grid=(s // tm,),
        in_specs=[row, vec, row] + ([row] if has_res else []),
        out_specs=[row, vec],
        out_shape=[jax.ShapeDtypeStruct((s, d), F32), jax.ShapeDtypeStruct((1, d), F32)],
        compiler_params=_cp("arbitrary"),
    )(*args)


def ffn_fwd(name, h, gain, wgT, wuT, wd, carry=None):
    s, d = h.shape
    f = wd.shape[0]
    tm, tf = _tile(s, 1024), _tile(f, 256)
    nf = f // tf
    carry = carry or _NoCarry()
    grid = (s // tm, nf)

    def body(*refs):
        (h_ref, g_ref, wg_ref, wu_ref, wd_ref), c_in, (o_ref,), c_out, (u_sc, acc), c_sem = carry.split(refs, 5, 1, 2)
        j = pl.program_id(1)
        carry.at_first_step(grid, c_in, c_out, c_sem)

        @pl.when(j == 0)
        def _():
            xv = h_ref[...]
            u_sc[...] = (xv * _rinv(xv) * g_ref[...]).astype(BF16)
            acc[...] = jnp.zeros((tm, d), F32)

        u = u_sc[...]
        a = _dot(u, wg_ref[...], NT)
        b = _dot(u, wu_ref[...], NT)
        act = (a * _sigmoid(a) * b).astype(BF16)
        acc[...] += _dot(act, wd_ref[...], NN)

        @pl.when(j == nf - 1)
        def _():
            o_ref[...] = h_ref[...] + 0.5 * acc[...]

        carry.at_last_step(grid, c_in, c_out, c_sem)

    row = pl.BlockSpec((tm, d), lambda i, j: (i, 0))
    wsp = pl.BlockSpec((tf, d), lambda i, j: (j, 0))
    outs = pl.pallas_call(
        body, name=name, grid=grid,
        in_specs=[row, pl.BlockSpec((1, d), lambda i, j: (0, 0)), wsp, wsp, wsp] + carry.in_specs(),
        out_specs=[row] + carry.out_specs(), out_shape=[jax.ShapeDtypeStruct((s, d), F32)] + carry.out_shapes(),
        scratch_shapes=[pltpu.VMEM((tm, d), BF16), pltpu.VMEM((tm, d), F32)] + carry.scratch(),
        compiler_params=_cp("arbitrary", "arbitrary"),
    )(h, gain, wgT, wuT, wd, *carry.arrays)
    return outs[0], list(outs[1:])


def ffn_bwd(name, h, gain, wgT, wuT, wd, dout, carry=None):
    s, d = h.shape
    f = wd.shape[0]
    tm, tf = _tile(s, 1024), _tile(f, 256)
    nf = f // tf
    halves = 2 if tm % (2 * FFN_RB) == 0 else 1
    hm = tm // halves
    carry = carry or _NoCarry()
    grid = (s // tm, nf)

    def body(*refs):
        ((h_ref, g_ref, wg_ref, wu_ref, wd_ref, do_ref), c_in,
         (dh_ref, u_ref, dob_ref, act_ref, da_ref, db_ref, dg_ref), c_out,
         (du_acc, a_sc, b_sc, c_sc), c_sem) = carry.split(refs, 6, 7, 4)
        i, j = pl.program_id(0), pl.program_id(1)
        carry.at_first_step(grid, c_in, c_out, c_sem)

        @pl.when(j == 0)
        def _():
            xv = h_ref[...]
            u_ref[...] = (xv * _rinv(xv) * g_ref[...]).astype(BF16)
            dob_ref[...] = (0.5 * do_ref[...]).astype(BF16)
            du_acc[...] = jnp.zeros((tm, d), F32)

        @pl.when((i == 0) & (j == 0))
        def _():
            dg_ref[...] = jnp.zeros((1, d), F32)

        wg, wu = wg_ref[...], wu_ref[...]
        for part in range(halves):
            prow = pl.ds(part * hm, hm)
            u = u_ref[prow, :]
            a_sc[prow, :] = _dot(u, wg, NT)
            b_sc[prow, :] = _dot(u, wu, NT)
            c_sc[prow, :] = _dot(dob_ref[prow, :], wd_ref[...], NT)
            for r in range(hm // FFN_RB):
                rows = pl.ds(part * hm + r * FFN_RB, FFN_RB)
                for t in range(tf // LANES):
                    cols = slice(t * LANES, (t + 1) * LANES)
                    a, b, dact = a_sc[rows, cols], b_sc[rows, cols], c_sc[rows, cols]
                    sig = _sigmoid(a)
                    sa = a * sig
                    act_ref[rows, cols] = (sa * b).astype(BF16)
                    db_ref[rows, cols] = (dact * sa).astype(BF16)
                    da_ref[rows, cols] = (dact * b * (sig * (1.0 + a * (1.0 - sig)))).astype(BF16)
            du_acc[prow, :] += _dot(da_ref[prow, :], wg, NN) + _dot(db_ref[prow, :], wu, NN)

        @pl.when(j == nf - 1)
        def _():
            dx, dgain = _rms_bwd_math(h_ref[...], g_ref[...], du_acc[...])
            dh_ref[...] = do_ref[...] + dx
            dg_ref[...] += dgain

        carry.at_last_step(grid, c_in, c_out, c_sem)

    row = pl.BlockSpec((tm, d), lambda i, j: (i, 0))
    vec = pl.BlockSpec((1, d), lambda i, j: (0, 0))
    wsp = pl.BlockSpec((tf, d), lambda i, j: (j, 0))
    hid = pl.BlockSpec((tm, tf), lambda i, j: (i, j))
    sd = lambda dt: jax.ShapeDtypeStruct((s, d), dt)
    sf = jax.ShapeDtypeStruct((s, f), BF16)
    outs = pl.pallas_call(
        body, name=name, grid=grid,
        in_specs=[row, vec, wsp, wsp, wsp, row] + carry.in_specs(),
        out_specs=[row, row, row, hid, hid, hid, vec] + carry.out_specs(),
        out_shape=[sd(F32), sd(BF16), sd(BF16), sf, sf, sf, jax.ShapeDtypeStruct((1, d), F32)] + carry.out_shapes(),
        scratch_shapes=[pltpu.VMEM((tm, d), F32)] + [pltpu.VMEM((tm, tf), F32)] * 3 + carry.scratch(),
        compiler_params=pltpu.CompilerParams(dimension_semantics=("arbitrary", "arbitrary"),
                                             vmem_limit_bytes=BIG_VMEM),
    )(h, gain, wgT, wuT, wd, dout, *carry.arrays)
    return outs[:7], list(outs[7:])


def pool_fwd(name, h, gain, pw, scale, layer):
    s, d = h.shape
    tm = _tile(s, 512)

    def body(h_ref, g_ref, pw_ref, sc_ref, o_ref, y_ref, ext):
        i = pl.program_id(0)

        @pl.when(i == 0)
        def _():
            ext[0:HALO, :] = jnp.zeros((HALO, d), F32)

        xv = h_ref[...]
        ext[HALO:HALO + tm, :] = xv * _rinv(xv) * g_ref[...]
        pos = i * tm + lax.broadcasted_iota(jnp.int32, (tm, 1), 0)
        for g, w in enumerate(POOL_WINDOWS):
            lo, hi = g * PG, (g + 1) * PG
            ug = ext[HALO:HALO + tm, lo:hi]
            tot = ug
            for k in range(1, w):
                tot = tot + ext[HALO - k:HALO - k + tm, lo:hi]
            cnt = jnp.minimum(pos + 1, w).astype(F32)
            yb = (tot / cnt - ug).astype(BF16)
            y_ref[:, lo:hi] = yb
            o_ref[:, lo:hi] = xv[:, lo:hi] + _dot(yb, pw_ref[g], NN) * sc_ref[:, lo:hi]
        ext[0:HALO, :] = ext[tm:tm + HALO, :]

    row = pl.BlockSpec((tm, d), lambda i: (i, 0))
    vec = pl.BlockSpec((1, d), lambda i: (0, 0))
    return pl.pallas_call(
        body, name=name, grid=(s // tm,),
        in_specs=[row, vec, pl.BlockSpec((None, 4, PG, PG), lambda i: (layer, 0, 0, 0)), vec],
        out_specs=[row, row],
        out_shape=[jax.ShapeDtypeStruct((s, d), F32), jax.ShapeDtypeStruct((s, d), BF16)],
        scratch_shapes=[pltpu.VMEM((HALO + tm, d), F32)],
        compiler_params=_cp("arbitrary"),
    )(h, gain, pw, scale)


def pool_bwd(name, h, gain, y, pw, scale, layer, dout):
    s, d = h.shape
    tm = _tile(s, 512)
    ns = s // tm

    def body(h_ref, g_ref, y_ref, pw_ref, sc_ref, do_ref, dh_ref, dz_ref, dg_ref, dsc_ref, ext, du_sc):
        i = pl.program_id(0)
        t = ns - 1 - i

        @pl.when(i == 0)
        def _():
            ext[tm:tm + HALO, :] = jnp.zeros((HALO, d), F32)
            dg_ref[...] = jnp.zeros((1, d), F32)
            dsc_ref[...] = jnp.zeros((1, d), F32)

        pos = t * tm + lax.broadcasted_iota(jnp.int32, (tm, 1), 0)
        for g, w in enumerate(POOL_WINDOWS):
            lo, hi = g * PG, (g + 1) * PG
            dog = do_ref[:, lo:hi]
            z = _dot(y_ref[:, lo:hi], pw_ref[g], NN)
            dsc_ref[:, lo:hi] += jnp.sum(dog * z, axis=0, keepdims=True)
            dzb = (dog * sc_ref[:, lo:hi]).astype(BF16)
            dz_ref[:, lo:hi] = dzb
            dy = _dot(dzb, pw_ref[g], NT)
            cnt = jnp.minimum(pos + 1, w).astype(F32)
            ext[0:tm, lo:hi] = dy / cnt
            tot = ext[0:tm, lo:hi]
            for k in range(1, w):
                tot = tot + ext[k:k + tm, lo:hi]
            du_sc[:, lo:hi] = tot - dy
        ext[tm:tm + HALO, :] = ext[0:HALO, :]
        dx, dgain = _rms_bwd_math(h_ref[...], g_ref[...], du_sc[...])
        dh_ref[...] = do_ref[...] + dx
        dg_ref[...] += dgain

    row = pl.BlockSpec((tm, d), lambda i: (ns - 1 - i, 0))
    vec = pl.BlockSpec((1, d), lambda i: (0, 0))
    return pl.pallas_call(
        body, name=name, grid=(ns,),
        in_specs=[row, vec, row, pl.BlockSpec((None, 4, PG, PG), lambda i: (layer, 0, 0, 0)), vec, row],
        out_specs=[row, row, vec, vec],
        out_shape=[jax.ShapeDtypeStruct((s, d), F32), jax.ShapeDtypeStruct((s, d), BF16),
                   jax.ShapeDtypeStruct((1, d), F32), jax.ShapeDtypeStruct((1, d), F32)],
        scratch_shapes=[pltpu.VMEM((tm + HALO, d), F32), pltpu.VMEM((tm, d), F32)],
        compiler_params=_cp("arbitrary"),
    )(h, gain, y, pw, scale, dout)


def _rope128(raw_hi, cs):
    b = raw_hi * cs
    r = b + pltpu.roll(b, DR, 1)
    lane = lax.broadcasted_iota(jnp.int32, r.shape, 1)
    return jnp.where(lane < DR, r, 0.0)


def _rope128_bwd(t, cs):
    return (t + pltpu.roll(t, DR, 1)) * cs


def q_heads(name, cq, w256, cs):
    s = cq.shape[0]
    nh = w256.shape[0]
    tm = _tile(s, 1024)

    def body(cq_ref, w_ref, cs_ref, o_ref):
        r = _dot(cq_ref[...], w_ref[...], NN) * Q_PRESCALE
        o_ref[:, 0:DN] = r[:, 0:DN].astype(BF16)
        o_ref[:, DN:HD] = _rope128(r[:, DN:HD], cs_ref[...]).astype(BF16)

    return pl.pallas_call(
        body, name=name, grid=(s // tm, nh),
        in_specs=[pl.BlockSpec((tm, HD), lambda i, hh: (i, 0)),
                  pl.BlockSpec((None, HD, HD), lambda i, hh: (hh, 0, 0)),
                  pl.BlockSpec((tm, DN), lambda i, hh: (i, 0))],
        out_specs=pl.BlockSpec((None, tm, HD), lambda i, hh: (hh, i, 0)),
        out_shape=jax.ShapeDtypeStruct((nh, s, HD), BF16),
        compiler_params=_cp("parallel", "parallel"),
    )(cq, w256, cs)


def kv_post(name, kv_raw, gain, cs):
    s = kv_raw.shape[0]
    tm = _tile(s, 1024)

    def body(x_ref, g_ref, cs_ref, c_ref, kr_ref):
        c = x_ref[:, 0:DN]
        c_ref[...] = (c * _rinv(c) * g_ref[...]).astype(BF16)
        kr_ref[...] = _rope128(x_ref[:, DN:HD], cs_ref[...]).astype(BF16)

    half = pl.BlockSpec((tm, DN), lambda i: (i, 0))
    return pl.pallas_call(
        body, name=name, grid=(s // tm,),
        in_specs=[pl.BlockSpec((tm, HD), lambda i: (i, 0)), pl.BlockSpec((1, DN), lambda i: (0, 0)), half],
        out_specs=[half, half],
        out_shape=[jax.ShapeDtypeStruct((s, DN), BF16), jax.ShapeDtypeStruct((s, DN), BF16)],
        compiler_params=_cp("parallel"),
    )(kv_raw, gain, cs)


def kv_heads(name, c_kv, kr, wuk, wuv):
    s = c_kv.shape[0]
    nh = wuk.shape[0]
    tm = _tile(s, 1024)

    def body(c_ref, kr_ref, wk_ref, wv_ref, k_ref, v_ref):
        c = c_ref[...]
        k_ref[:, 0:DN] = _dot(c, wk_ref[...], NN).astype(BF16)
        k_ref[:, DN:HD] = kr_ref[...]
        v_ref[:, 0:DN] = _dot(c, wv_ref[...], NN).astype(BF16)
        v_ref[:, DN:HD] = jnp.ones((tm, DN), BF16)

    half = pl.BlockSpec((tm, DN), lambda i, hh: (i, 0))
    wsp = pl.BlockSpec((None, DN, DN), lambda i, hh: (hh, 0, 0))
    head = pl.BlockSpec((None, tm, HD), lambda i, hh: (hh, i, 0))
    return pl.pallas_call(
        body, name=name, grid=(s // tm, nh),
        in_specs=[half, half, wsp, wsp], out_specs=[head, head],
        out_shape=[jax.ShapeDtypeStruct((nh, s, HD), BF16), jax.ShapeDtypeStruct((nh, s, HD), BF16)],
        compiler_params=_cp("parallel", "parallel"),
    )(c_kv, kr, wuk, wuv)


def kv_heads_bwd(name, dk, dv, c_kv, wuk, wuv):
    nh, s, _ = dk.shape
    tm = _tile(s, 1024)

    def body(dk_ref, dv_ref, c_ref, wk_ref, wv_ref, dc_ref, dkr_ref, dwk_ref, dwv_ref):
        i, hh = pl.program_id(0), pl.program_id(1)

        @pl.when(hh == 0)
        def _():
            dc_ref[...] = jnp.zeros((tm, DN), F32)
            dkr_ref[...] = jnp.zeros((tm, DN), F32)

        @pl.when(i == 0)
        def _():
            dwk_ref[hh] = jnp.zeros((DN, DN), F32)
            dwv_ref[hh] = jnp.zeros((DN, DN), F32)

        dkn = dk_ref[:, 0:DN].astype(BF16)
        dvb = dv_ref[...].astype(BF16)
        c = c_ref[...]
        dc_ref[...] += _dot(dkn, wk_ref[...], NT) + _dot(dvb, wv_ref[...], NT)
        dkr_ref[...] += dk_ref[:, DN:HD]
        dwk_ref[hh] += _dot(c, dkn, TN)
        dwv_ref[hh] += _dot(c, dvb, TN)

    half = pl.BlockSpec((tm, DN), lambda i, hh: (i, 0))
    wsp = pl.BlockSpec((None, DN, DN), lambda i, hh: (hh, 0, 0))
    allw = pl.BlockSpec((nh, DN, DN), lambda i, hh: (0, 0, 0))
    return pl.pallas_call(
        body, name=name, grid=(s // tm, nh),
        in_specs=[pl.BlockSpec((None, tm, HD), lambda i, hh: (hh, i, 0)),
                  pl.BlockSpec((None, tm, DN), lambda i, hh: (hh, i, 0)), half, wsp, wsp],
        out_specs=[half, half, allw, allw],
        out_shape=[jax.ShapeDtypeStruct((s, DN), F32), jax.ShapeDtypeStruct((s, DN), F32),
                   jax.ShapeDtypeStruct((nh, DN, DN), F32), jax.ShapeDtypeStruct((nh, DN, DN), F32)],
        compiler_params=_cp("arbitrary", "arbitrary"),
    )(dk, dv, c_kv, wuk, wuv)


def kv_post_bwd(name, kv_raw, gain, dc, dkr, cs):
    s = kv_raw.shape[0]
    tm = _tile(s, 1024)

    def body(x_ref, g_ref, dc_ref, dkr_ref, cs_ref, o_ref, dg_ref):
        @pl.when(pl.program_id(0) == 0)
        def _():
            dg_ref[...] = jnp.zeros((1, DN), F32)

        dx, dgain = _rms_bwd_math(x_ref[:, 0:DN], g_ref[...], dc_ref[...])
        o_ref[:, 0:DN] = dx.astype(BF16)
        o_ref[:, DN:HD] = _rope128_bwd(dkr_ref[...], cs_ref[...]).astype(BF16)
        dg_ref[...] += dgain

    half = pl.BlockSpec((tm, DN), lambda i: (i, 0))
    vec = pl.BlockSpec((1, DN), lambda i: (0, 0))
    full = pl.BlockSpec((tm, HD), lambda i: (i, 0))
    return pl.pallas_call(
        body, name=name, grid=(s // tm,), in_specs=[full, vec, half, half, half], out_specs=[full, vec],
        out_shape=[jax.ShapeDtypeStruct((s, HD), BF16), jax.ShapeDtypeStruct((1, DN), F32)],
        compiler_params=_cp("arbitrary"),
    )(kv_raw, gain, dc, dkr, cs)


TQ = 1024
TC = 512
RB = 64


def _flash_cp():
    return pltpu.CompilerParams(dimension_semantics=("parallel", "arbitrary"), vmem_limit_bytes=BIG_VMEM)


def _causal(x, row0, col0, keys_on_rows=False):
    r = row0 + lax.broadcasted_iota(jnp.int32, x.shape, 0)
    c = col0 + lax.broadcasted_iota(jnp.int32, x.shape, 1)
    return jnp.where((r <= c) if keys_on_rows else (c <= r), x, NEG)


def _chunk_pipeline(n_pairs, chunk_of, scores, finish, bufs, last_from):
    b0, b1 = bufs
    scores(chunk_of(0), b0, 0)

    def pair(t, carry):
        m = 2 * t
        scores(chunk_of(m + 1), b1, 0)
        finish(chunk_of(m), b0, False, 0)
        scores(chunk_of(m + 2), b0, 0)
        finish(chunk_of(m + 1), b1, False, 0)
        return carry

    lax.fori_loop(0, n_pairs, pair, 0)
    m = 2 * n_pairs
    scores(chunk_of(m + 1), b1, last_from)
    finish(chunk_of(m), b0, True, 0)
    finish(chunk_of(m + 1), b1, True, last_from)


def flash_fwd(name, q, k, v2):
    nh, s, _ = q.shape
    tq, tk = _tile(s, TQ), _tile(s, TC)
    assert tq == 2 * tk
    nt = tk // LANES

    def body(q_ref, k_ref, v_ref, o_ref, lse_ref, s0_sc, s1_sc, p_sc, m_sc, acc):
        i = pl.program_id(1)
        m_sc[...] = jnp.full((tq, LANES), NEG, F32)
        acc[...] = jnp.zeros((tq, HD), F32)

        def scores(j, dst, row0):
            dst[row0:, :] = _dot(q_ref[row0:, :], k_ref[pl.ds(pl.multiple_of(j * tk, tk), tk), :], NT)

        def softmax_rows(src, r0, j, masked):
            rows = pl.ds(r0, RB)

            def tile(t):
                x = src[rows, t * LANES:(t + 1) * LANES]
                return _causal(x, i * tq + r0, j * tk + t * LANES) if masked else x

            mx = tile(0)
            for t in range(1, nt):
                mx = jnp.maximum(mx, tile(t))
            m_prev = m_sc[rows, :]
            m_new = jnp.maximum(m_prev, jnp.max(mx, axis=-1, keepdims=True))
            alpha = jnp.exp2(m_prev - m_new)
            for t in range(nt):
                p_sc[rows, t * LANES:(t + 1) * LANES] = jnp.exp2(tile(t) - m_new).astype(BF16)
            m_sc[rows, :] = m_new
            acc[rows, 0:DN] = alpha * acc[rows, 0:DN]
            acc[rows, DN:HD] = alpha * acc[rows, DN:HD]

        def finish(j, src, masked, row0):
            for r in range(row0 // RB, tq // RB):
                softmax_rows(src, r * RB, j, masked)
            acc[row0:, :] += _dot(p_sc[row0:, :], v_ref[pl.ds(pl.multiple_of(j * tk, tk), tk), :], NN)

        _chunk_pipeline(i, lambda m: m, scores, finish, (s0_sc, s1_sc), tk)
        l = acc[:, DN:HD]
        o_ref[...] = (acc[:, 0:DN] / l).astype(BF16)
        lse_ref[...] = m_sc[...] + jnp.log2(l)

    whole = pl.BlockSpec((None, s, HD), lambda hh, i: (hh, 0, 0))
    return pl.pallas_call(
        body, name=name, grid=(nh, s // tq),
        in_specs=[pl.BlockSpec((None, tq, HD), lambda hh, i: (hh, i, 0)), whole, whole],
        out_specs=[pl.BlockSpec((tq, DN), lambda hh, i: (i, hh)),
                   pl.BlockSpec((None, tq, DN), lambda hh, i: (hh, i, 0))],
        out_shape=[jax.ShapeDtypeStruct((s, nh * DN), BF16), jax.ShapeDtypeStruct((nh, s, DN), F32)],
        scratch_shapes=[pltpu.VMEM((tq, tk), F32), pltpu.VMEM((tq, tk), F32), pltpu.VMEM((tq, tk), BF16),
                        pltpu.VMEM((tq, LANES), F32), pltpu.VMEM((tq, HD), F32)],
        compiler_params=_flash_cp(),
    )(q, k, v2)


def attn_do2(name, o, do, nh):
    s = o.shape[0]
    tm = _tile(s, 512)

    def body(o_ref, do_ref, d_ref):
        lane = lax.broadcasted_iota(jnp.int32, (tm, DN), 1)
        for hh in range(nh):
            cols = slice(hh * DN, (hh + 1) * DN)
            dov = do_ref[:, cols]
            delta = jnp.sum(o_ref[:, cols].astype(F32) * dov.astype(F32), axis=-1, keepdims=True)
            hi = delta.astype(BF16).astype(F32)
            lo = delta - hi
            d_ref[hh, :, 0:DN] = dov
            d_ref[hh, :, DN:HD] = jnp.where(lane == 0, -hi, jnp.where(lane == 1, -lo, 0.0)).astype(BF16)

    blk = pl.BlockSpec((tm, nh * DN), lambda i: (i, 0))
    return pl.pallas_call(
        body, name=name, grid=(s // tm,), in_specs=[blk, blk],
        out_specs=pl.BlockSpec((nh, tm, HD), lambda i: (0, i, 0)),
        out_shape=jax.ShapeDtypeStruct((nh, s, HD), BF16), compiler_params=_cp("parallel"),
    )(o, do)


def q_heads_bwd(name, dq_acc, cs, w256, cq):
    nh, s, _ = dq_acc.shape
    tm = _tile(s, 1024)
    ns = s // tm

    def body(a_ref, cs_ref, w_ref, cq_ref, dcq_ref, dw_ref, dw_acc):
        i, hh = pl.program_id(0), pl.program_id(1)

        @pl.when(hh == 0)
        def _():
            dcq_ref[...] = jnp.zeros((tm, HD), F32)

        @pl.when(i == 0)
        def _():
            dw_acc[hh] = jnp.zeros((HD, HD), F32)

        draw = jnp.concatenate(
            [(a_ref[:, 0:DN] * ATT_SCALE).astype(BF16),
             _rope128_bwd(a_ref[:, DN:HD] * ATT_SCALE, cs_ref[...]).astype(BF16)], axis=-1)
        dcq_ref[...] += _dot(draw, w_ref[...], NT)
        dw_acc[hh] += _dot(cq_ref[...], draw, TN)

        @pl.when(i == ns - 1)
        def _():
            dw_ref[hh] = dw_acc[hh].astype(BF16)

    row = pl.BlockSpec((tm, HD), lambda i, hh: (i, 0))
    return pl.pallas_call(
        body, name=name, grid=(ns, nh),
        in_specs=[pl.BlockSpec((None, tm, HD), lambda i, hh: (hh, i, 0)), pl.BlockSpec((tm, DN), lambda i, hh: (i, 0)),
                  pl.BlockSpec((None, HD, HD), lambda i, hh: (hh, 0, 0)), row],
        out_specs=[row, pl.BlockSpec((nh, HD, HD), lambda i, hh: (0, 0, 0))],
        out_shape=[jax.ShapeDtypeStruct((s, HD), F32), jax.ShapeDtypeStruct((nh, HD, HD), BF16)],
        scratch_shapes=[pltpu.VMEM((nh, HD, HD), F32)],
        compiler_params=_cp("arbitrary", "arbitrary"),
    )(dq_acc, cs, w256, cq)


def flash_bwd(name, q, k, v2, do2, lse_row, prev=None):
    nh, s, _ = q.shape
    tk, tq = _tile(s, TQ), _tile(s, TC)
    nqc = s // tq
    per = tk // tq
    nt = tq // LANES
    nkt = s // tk
    has_prev = prev is not None

    def body(*refs):
        k_ref, v_ref, q_ref, do_ref, lse_ref = refs[:5]
        dk_ref, dv_ref, dq_hbm, s_sc, d_sc, p_sc, ds_sc, dk_acc, dv_acc, dq_sc, sem = refs[-11:]
        hh, i = pl.program_id(0), pl.program_id(1)
        dk_acc[...] = jnp.zeros((tk, HD), F32)
        dv_acc[...] = jnp.zeros((tk, DN), F32)

        @pl.when(i == 0)
        def _():
            dq_sc[...] = jnp.zeros((s, HD), F32)

        def p_rows(r0, j, masked):
            rows = pl.ds(r0, RB)
            for t in range(nt):
                cols = slice(t * LANES, (t + 1) * LANES)
                x = s_sc[rows, cols]
                if masked:
                    x = _causal(x, i * tk + r0, j * tq + t * LANES, keys_on_rows=True)
                p = jnp.exp2(x - jnp.tile(lse_ref[j][:, cols], (RB // 8, 1)))
                p_sc[rows, cols] = p.astype(BF16)
                ds_sc[rows, cols] = (p * d_sc[rows, cols]).astype(BF16)

        def step(j, masked):
            start = pl.multiple_of(j * tq, tq)
            qj = q_ref[pl.ds(start, tq), :]
            s_sc[...] = _dot(k_ref[...], qj, NT)
            d_sc[...] = _dot(v_ref[...], do_ref[pl.ds(start, tq), :], NT)
            for r in range(tk // RB):
                p_rows(r * RB, j, masked)
            dv_acc[...] += _dot(p_sc[...], do_ref[pl.ds(start, tq), 0:DN], NN)
            ds = ds_sc[...]
            dk_acc[...] += _dot(ds, qj, NN)
            dq_sc[pl.ds(start, tq), :] += _dot(ds, k_ref[...], TN)

        def full_step(m, carry):
            step(nqc - 1 - m, False)
            return carry

        lax.fori_loop(0, nqc - (i + 1) * per, full_step, 0)
        for dd in reversed(range(per)):
            step(i * per + dd, True)
        if has_prev:
            dk_ref[...] = dk_acc[...] * LN2 + refs[5][...]
            dv_ref[...] = dv_acc[...] + refs[6][...]
        else:
            dk_ref[...] = dk_acc[...] * LN2
            dv_ref[...] = dv_acc[...]

        @pl.when(i == nkt - 1)
        def _():
            cp = pltpu.make_async_copy(dq_sc, dq_hbm.at[hh], sem)
            cp.start()
            cp.wait()

    kblk = pl.BlockSpec((None, tk, HD), lambda hh, i: (hh, i, 0))
    vblk = pl.BlockSpec((None, tk, DN), lambda hh, i: (hh, i, 0))
    whole = pl.BlockSpec((None, s, HD), lambda hh, i: (hh, 0, 0), pipeline_mode=pl.Buffered(1))
    in_specs = [kblk, kblk, whole, whole, pl.BlockSpec((None, nqc, 8, tq), lambda hh, i: (hh, 0, 0, 0))]
    args = [k, v2, q, do2, lse_row]
    if has_prev:
        in_specs += [kblk, vblk]
        args += list(prev)
    big = jax.ShapeDtypeStruct((nh, s, HD), F32)
    return pl.pallas_call(
        body, name=name, grid=(nh, nkt), in_specs=in_specs, out_specs=[kblk, vblk, ANY],
        out_shape=[big, jax.ShapeDtypeStruct((nh, s, DN), F32), big],
        scratch_shapes=[pltpu.VMEM((tk, tq), F32)] * 2 + [pltpu.VMEM((tk, tq), BF16)] * 2
        + [pltpu.VMEM((tk, HD), F32), pltpu.VMEM((tk, DN), F32), pltpu.VMEM((s, HD), F32), pltpu.SemaphoreType.DMA],
        compiler_params=pltpu.CompilerParams(dimension_semantics=("arbitrary", "arbitrary"), vmem_limit_bytes=BIG_VMEM),
    )(*args)


def loss_head(name, h, gain, target):
    s, d = h.shape
    tm = _tile(s, 1024)

    def body(h_ref, g_ref, t_ref, sq_ref, dh_ref, dg_ref):
        @pl.when(pl.program_id(0) == 0)
        def _():
            sq_ref[...] = jnp.zeros((1, d), F32)
            dg_ref[...] = jnp.zeros((1, d), F32)

        xv = h_ref[...]
        g = g_ref[...]
        rinv = _rinv(xv)
        xhat = xv * rinv
        err = xhat * g - t_ref[...]
        sq_ref[...] += jnp.sum(err * err, axis=0, keepdims=True)
        dy = err / d
        dg_ref[...] += jnp.sum(dy * xhat, axis=0, keepdims=True)
        dxh = dy * g
        dh_ref[...] = rinv * (dxh - xhat * jnp.mean(dxh * xhat, axis=-1, keepdims=True))

    row = pl.BlockSpec((tm, d), lambda i: (i, 0))
    vec = pl.BlockSpec((1, d), lambda i: (0, 0))
    return pl.pallas_call(
        body, name=name, grid=(s // tm,), in_specs=[row, vec, row], out_specs=[vec, row, vec],
        out_shape=[jax.ShapeDtypeStruct((1, d), F32), jax.ShapeDtypeStruct((s, d), F32),
                   jax.ShapeDtypeStruct((1, d), F32)],
        compiler_params=_cp("arbitrary"),
    )(h, gain, target)


def _adamw_math(w, g, m, v):
    m2 = ADAM_B1 * m + (1.0 - ADAM_B1) * g
    v2 = ADAM_B2 * v + (1.0 - ADAM_B2) * (g * g)
    mh = m2 / (1.0 - ADAM_B1 ** ADAM_STEP)
    vh = v2 / (1.0 - ADAM_B2 ** ADAM_STEP)
    delta = -ADAM_LR * (mh / (jnp.sqrt(vh) + ADAM_EPS) + ADAM_WD * w)
    return delta, m2, v2


def adamw(name, parts, w, m, v):
    rows, cols = w.shape
    tr = _tile(rows, max(8, (1 << 18) // cols))
    stacked = parts[0].ndim == 3
    npart = len(parts)

    def body(*refs):
        p_refs = refs[:npart]
        w_ref, m_ref, v_ref, g_ref, d_ref, m2_ref, v2_ref = refs[npart:]
        if stacked:
            g = p_refs[0][0]
            for n in range(1, parts[0].shape[0]):
                g = g + p_refs[0][n]
        else:
            g = p_refs[0][...]
            for r in p_refs[1:]:
                g = g + r[...]
        delta, m2, v2 = _adamw_math(w_ref[...], g, m_ref[...], v_ref[...])
        g_ref[...] = g
        d_ref[...] = delta
        m2_ref[...] = m2
        v2_ref[...] = v2

    blk = pl.BlockSpec((tr, cols), lambda i: (i, 0))
    pblk = pl.BlockSpec((parts[0].shape[0], tr, cols), lambda i: (0, i, 0)) if stacked else blk
    sds = jax.ShapeDtypeStruct((rows, cols), F32)
    return pl.pallas_call(
        body, name=name, grid=(rows // tr,), in_specs=[pblk] * npart + [blk] * 3,
        out_specs=[blk] * 4, out_shape=[sds] * 4, compiler_params=_cp("parallel"),
    )(*parts, w, m, v)


def sum4(name, x):
    _, a, r, c = x.shape
    tr = _tile(r, max(8, (1 << 18) // c))

    def body(x_ref, o_ref):
        o_ref[...] = ((x_ref[3].astype(F32) + x_ref[0].astype(F32)) + x_ref[1].astype(F32)) + x_ref[2].astype(F32)

    return pl.pallas_call(
        body, name=name, grid=(a, r // tr),
        in_specs=[pl.BlockSpec((4, None, tr, c), lambda i, j: (0, i, j, 0))],
        out_specs=pl.BlockSpec((None, tr, c), lambda i, j: (i, j, 0)),
        out_shape=jax.ShapeDtypeStruct((a, r, c), F32), compiler_params=_cp("parallel", "parallel"),
    )(x)


ANY = pl.BlockSpec(memory_space=pl.ANY)


def _mesh_pos():
    return lax.axis_index("x"), lax.axis_index("y"), lax.axis_index("c")


def _chip_peers(x, y):
    return [(1 - x, y), (x, 1 - y), (1 - x, 1 - y)]


class _NoCarry:
    arrays = ()

    def split(self, refs, n_in, n_out, n_scratch):
        return (refs[:n_in], (), refs[n_in:n_in + n_out], (), refs[n_in + n_out:n_in + n_out + n_scratch], ())

    def in_specs(self):
        return []

    out_specs = out_shapes = scratch = in_specs

    def at_first_step(self, grid, ins, outs, sems):
        pass

    at_last_step = at_first_step


class ChipCopies:
    def __init__(self, kind, arrays):
        assert kind in ("gather", "exchange")
        self.kind, self.arrays, self.n = kind, tuple(arrays), len(arrays)

    def split(self, refs, n_in, n_out, n_scratch):
        n, pos, parts = self.n, 0, []
        for count in (n_in, n, n_out, n, n_scratch, 3):
            parts.append(refs[pos:pos + count])
            pos += count
        return parts

    def in_specs(self):
        return [ANY] * self.n

    out_specs = in_specs

    def out_shapes(self):
        if self.kind == "gather":
            return [jax.ShapeDtypeStruct((a.shape[0], N_CHIPS) + a.shape[1:], a.dtype) for a in self.arrays]
        return [jax.ShapeDtypeStruct((N_CHIPS, a.shape[0]) + a.shape[2:], a.dtype) for a in self.arrays]

    def scratch(self):
        n = self.n
        return [pltpu.SemaphoreType.DMA((3 * n,)), pltpu.SemaphoreType.DMA((3 * n,)), pltpu.SemaphoreType.DMA((n,))]

    def _copies(self, ins, outs, sems):
        send_sems, recv_sems, loc_sems = sems
        x, y, c = _mesh_pos()
        me = 2 * x + y
        peers = _chip_peers(x, y)
        gather = self.kind == "gather"

        def remote(a, j, block):
            px, py = peers[j]
            src = ins[a] if gather else ins[a].at[:, 2 * px + py]
            dst = outs[a].at[:, block] if gather else outs[a].at[j]
            return pltpu.make_async_remote_copy(
                src_ref=src, dst_ref=dst, send_sem=send_sems.at[a * 3 + j], recv_sem=recv_sems.at[a * 3 + j],
                device_id=(px, py, c), device_id_type=MESH)

        pairs = [(a, j) for a in range(self.n) for j in range(3)]
        if gather:
            local = [pltpu.make_async_copy(ins[a], outs[a].at[:, me], loc_sems.at[a]) for a in range(self.n)]
        else:
            local = [pltpu.make_async_copy(ins[a].at[:, me], outs[a].at[3], loc_sems.at[a]) for a in range(self.n)]
        sends = [remote(a, j, me) for a, j in pairs]
        recvs = [remote(a, j, 2 * peers[j][0] + peers[j][1]) for a, j in pairs]
        return local, sends, recvs

    def start(self, ins, outs, sems):
        local, sends, _ = self._copies(ins, outs, sems)
        for cp in local + sends:
            cp.start()

    def wait(self, ins, outs, sems):
        local, sends, recvs = self._copies(ins, outs, sems)
        for cp in recvs:
            cp.wait_recv()
        for cp in sends:
            cp.wait_send()
        for cp in local:
            cp.wait()

    @staticmethod
    def _is_step(grid, last):
        cond = None
        for ax, size in enumerate(grid):
            c = pl.program_id(ax) == (size - 1 if last else 0)
            cond = c if cond is None else cond & c
        return cond

    def at_first_step(self, grid, ins, outs, sems):
        @pl.when(self._is_step(grid, False))
        def _():
            self.start(ins, outs, sems)

    def at_last_step(self, grid, ins, outs, sems):
        @pl.when(self._is_step(grid, True))
        def _():
            self.wait(ins, outs, sems)


def chip_copies(name, copies):
    n = copies.n

    def body(*refs):
        ins, outs, sems = refs[:n], refs[n:2 * n], refs[2 * n:]
        copies.start(ins, outs, sems)
        copies.wait(ins, outs, sems)

    return pl.pallas_call(
        body, name=name, in_specs=copies.in_specs(), out_specs=copies.out_specs(),
        out_shape=copies.out_shapes(), scratch_shapes=copies.scratch(),
    )(*copies.arrays)


def sibling_exchange(name, arrs):
    n = len(arrs)

    def body(*refs):
        ins, outs = refs[:n], refs[n:2 * n]
        send_sems, recv_sems = refs[2 * n:]
        x, y, c = _mesh_pos()
        copies = [pltpu.make_async_remote_copy(
            src_ref=ins[a], dst_ref=outs[a], send_sem=send_sems.at[a], recv_sem=recv_sems.at[a],
            device_id=(x, y, 1 - c), device_id_type=MESH) for a in range(n)]
        for cp in copies:
            cp.start()
        for cp in copies:
            cp.wait_recv()
        for cp in copies:
            cp.wait_send()

    return pl.pallas_call(
        body, name=name, in_specs=[ANY] * n, out_specs=[ANY] * n,
        out_shape=[jax.ShapeDtypeStruct(a.shape, a.dtype) for a in arrs],
        scratch_shapes=[pltpu.SemaphoreType.DMA((n,)), pltpu.SemaphoreType.DMA((n,))],
    )(*arrs)


def all_gather8(name, buf):
    r, cdim = buf.shape

    def body(in_ref, out_ref, send_sems, recv_sems, loc_sem):
        x, y, c = _mesh_pos()
        flips = [(fx, fy, fc) for fx in (0, 1) for fy in (0, 1) for fc in (0, 1)][1:]

        def peer(f):
            return tuple((1 - p) if b else p for p, b in zip((x, y, c), f))

        def remote(j, slot):
            return pltpu.make_async_remote_copy(
                src_ref=in_ref, dst_ref=out_ref.at[slot], send_sem=send_sems.at[j], recv_sem=recv_sems.at[j],
                device_id=peer(flips[j]), device_id_type=MESH)

        me = 4 * x + 2 * y + c
        local = pltpu.make_async_copy(in_ref, out_ref.at[me], loc_sem)
        local.start()
        sends = [remote(j, me) for j in range(7)]
        for cp in sends:
            cp.start()
        for j in range(7):
            px, py, pc = peer(flips[j])
            remote(j, 4 * px + 2 * py + pc).wait_recv()
        for cp in sends:
            cp.wait_send()
        local.wait()

    return pl.pallas_call(
        body, name=name, in_specs=[ANY], out_specs=ANY,
        out_shape=jax.ShapeDtypeStruct((8, r, cdim), buf.dtype),
        scratch_shapes=[pltpu.SemaphoreType.DMA((7,)), pltpu.SemaphoreType.DMA((7,)), pltpu.SemaphoreType.DMA],
    )(buf)


def _rot_cols(w):
    half = w.shape[-1] // 2
    return jnp.concatenate([-w[..., half:], w[..., :half]], axis=-1)


def _fold_rot(g):
    rot = g[..., DN + DR:]
    half = DR // 2
    return jnp.concatenate([g[..., :DN], g[..., DN:DN + DR] + jnp.concatenate([rot[..., half:], -rot[..., :half]], -1)], -1)


def _with_rot(w):
    return jnp.concatenate([w, _rot_cols(w[..., DN:])], axis=-1)


def _rope_table(s):
    pos = jnp.arange(s, dtype=F32)
    inv_freq = ROPE_THETA ** (-jnp.arange(0, DR, 2, dtype=F32) / DR)
    ang = pos[:, None] * inv_freq[None, :]
    cos, sin = jnp.cos(ang), jnp.sin(ang)
    return jnp.concatenate([cos, cos, sin, sin], axis=-1)


def _row_stats(x, tq):
    nh, s, _ = x.shape
    return jnp.broadcast_to(x[:, :, 0].reshape(nh, s // tq, 1, tq), (nh, s // tq, 8, tq))


REPL_NAMES = ["ffn_pre_norm", "mix_norm", "ffn_post_norm", "kv_in_norm", "final_norm", "ckv_norm",
              "q_lora_norm", "pool_scale_full", "w_uk", "w_uv"]
PACK_COLS = 1024


def _pack_rows(arrs):
    rows, counts = [], []
    for a in arrs:
        flat = a.reshape(-1).astype(F32)
        n = -(-flat.shape[0] // PACK_COLS)
        flat = jnp.pad(flat, (0, n * PACK_COLS - flat.shape[0]))
        rows.append(flat.reshape(n, PACK_COLS))
        counts.append(n)
    total = sum(counts)
    pad = -total % 8
    if pad:
        rows.append(jnp.zeros((pad, PACK_COLS), F32))
    return jnp.concatenate(rows, axis=0), counts


def _unpack_rows(buf, counts, shapes):
    out, r = [], 0
    for n, shp in zip(counts, shapes):
        size = 1
        for dd in shp:
            size *= dd
        out.append(buf[r:r + n].reshape(-1)[:size].reshape(shp))
        r += n
    return out


def kernel(x, ffn_pre_norm, ffn_pre_wg, ffn_pre_wu, ffn_pre_wd, mix_norm, ffn_post_norm, ffn_post_wg, ffn_post_wu, ffn_post_wd, pool_w, pool_scale, kv_in_norm, w_dkv, ckv_norm, w_uk, w_uv, q_lora_norm, w_dq, w_uq, w_o, final_norm, loss_target, m_ffn_pre_norm, m_ffn_pre_wg, m_ffn_pre_wu, m_ffn_pre_wd, m_mix_norm, m_ffn_post_norm, m_ffn_post_wg, m_ffn_post_wu, m_ffn_post_wd, m_pool_w, m_pool_scale, m_kv_in_norm, m_w_dkv, m_ckv_norm, m_w_uk, m_w_uv, m_q_lora_norm, m_w_dq, m_w_uq, m_w_o, m_final_norm, v_ffn_pre_norm, v_ffn_pre_wg, v_ffn_pre_wu, v_ffn_pre_wd, v_mix_norm, v_ffn_post_norm, v_ffn_post_wg, v_ffn_post_wu, v_ffn_post_wd, v_pool_w, v_pool_scale, v_kv_in_norm, v_w_dkv, v_ckv_norm, v_w_uk, v_w_uv, v_q_lora_norm, v_w_dq, v_w_uq, v_w_o, v_final_norm):
    weights = dict(ffn_pre_norm=ffn_pre_norm, ffn_pre_wg=ffn_pre_wg, ffn_pre_wu=ffn_pre_wu, ffn_pre_wd=ffn_pre_wd,
                   mix_norm=mix_norm, ffn_post_norm=ffn_post_norm, ffn_post_wg=ffn_post_wg, ffn_post_wu=ffn_post_wu,
                   ffn_post_wd=ffn_post_wd, pool_w=pool_w, pool_scale=pool_scale, kv_in_norm=kv_in_norm, w_dkv=w_dkv,
                   ckv_norm=ckv_norm, w_uk=w_uk, w_uv=w_uv, q_lora_norm=q_lora_norm, w_dq=w_dq, w_uq=w_uq, w_o=w_o,
                   final_norm=final_norm)
    mom1 = dict(ffn_pre_norm=m_ffn_pre_norm, ffn_pre_wg=m_ffn_pre_wg, ffn_pre_wu=m_ffn_pre_wu, ffn_pre_wd=m_ffn_pre_wd,
                mix_norm=m_mix_norm, ffn_post_norm=m_ffn_post_norm, ffn_post_wg=m_ffn_post_wg, ffn_post_wu=m_ffn_post_wu,
                ffn_post_wd=m_ffn_post_wd, pool_w=m_pool_w, pool_scale=m_pool_scale, kv_in_norm=m_kv_in_norm,
                w_dkv=m_w_dkv, ckv_norm=m_ckv_norm, w_uk=m_w_uk, w_uv=m_w_uv, q_lora_norm=m_q_lora_norm, w_dq=m_w_dq,
                w_uq=m_w_uq, w_o=m_w_o, final_norm=m_final_norm)
    mom2 = dict(ffn_pre_norm=v_ffn_pre_norm, ffn_pre_wg=v_ffn_pre_wg, ffn_pre_wu=v_ffn_pre_wu, ffn_pre_wd=v_ffn_pre_wd,
                mix_norm=v_mix_norm, ffn_post_norm=v_ffn_post_norm, ffn_post_wg=v_ffn_post_wg, ffn_post_wu=v_ffn_post_wu,
                ffn_post_wd=v_ffn_post_wd, pool_w=v_pool_w, pool_scale=v_pool_scale, kv_in_norm=v_kv_in_norm,
                w_dkv=v_w_dkv, ckv_norm=v_ckv_norm, w_uk=v_w_uk, w_uv=v_w_uv, q_lora_norm=v_q_lora_norm, w_dq=v_w_dq,
                w_uq=v_w_uq, w_o=v_w_o, final_norm=v_final_norm)
    names = list(weights)

    h0 = x[0]
    target = loss_target[0]
    s, d = h0.shape
    nh = N_HEADS
    nb = DEPTH - N_A
    f_loc = ffn_pre_wd.shape[1]
    ffn_dim = N_CHIPS * f_loc
    my_chip = 2 * lax.axis_index("x") + lax.axis_index("y")

    ffn_in = {"pre": (ffn_pre_norm, ffn_pre_wg, ffn_pre_wu, ffn_pre_wd),
              "post": (ffn_post_norm, ffn_post_wg, ffn_post_wu, ffn_post_wd)}
    uq_loc = jnp.transpose(w_uq, (0, 2, 1, 3))
    shard = {
        "pool_w": pool_w.reshape(N_A * 4, PG // N_CHIPS, PG).astype(BF16),
        "pool_scale": pool_scale.reshape(N_A, 1, PG),
        "wdkv": _with_rot(w_dkv)[None].astype(BF16),
        "wdq": w_dq.astype(BF16),
        "wq": _with_rot(uq_loc).reshape(nb * nh, uq_loc.shape[2], HD).astype(BF16),
        "wo": w_o.astype(BF16),
    }
    for part, (_, wg_, wu_, wd_) in ffn_in.items():
        for l in range(DEPTH):
            shard[f"{part}_wg{l}"] = jnp.transpose(wg_[l]).astype(BF16)[None]
            shard[f"{part}_wu{l}"] = jnp.transpose(wu_[l]).astype(BF16)[None]
            shard[f"{part}_wd{l}"] = wd_[l].astype(BF16)[None]
    ffn_names = lambda part, l: [f"{part}_wg{l}", f"{part}_wu{l}", f"{part}_wd{l}"]
    gather_plan = {
        None: ffn_names("pre", 0),
        ("pre", 0): ["pool_w", "pool_scale"] + ffn_names("post", 0) + ffn_names("pre", 1),
        ("post", 0): ffn_names("post", 1) + ["wdkv"] + ffn_names("pre", 2),
        ("pre", 1): ffn_names("post", 2) + ["wdq", "wq", "wo"],
        ("post", 1): ffn_names("pre", 3) + ffn_names("post", 3),
    }
    full = {}

    def absorb(plan_names, outs):
        for n, g in zip(plan_names, outs):
            full[n] = g.reshape(g.shape[0], N_CHIPS * g.shape[2], g.shape[3])

    absorb(gather_plan[None], chip_copies("gather_first", ChipCopies("gather", [shard[n] for n in gather_plan[None]])))
    wuk_h = jnp.transpose(w_uk, (1, 0, 2)).astype(BF16)
    wuv_h = jnp.transpose(w_uv, (1, 0, 2)).astype(BF16)
    cs = _rope_table(s)

    def ffn_forward(part, l, hin):
        plan_names = gather_plan.get((part, l))
        carry = ChipCopies("gather", [shard[n] for n in plan_names]) if plan_names else None
        wg_, wu_, wd_ = [full[n][0] for n in ffn_names(part, l)]
        hout, carried = ffn_fwd(f"ffn_{part}_fwd{l}", hin, ffn_in[part][0][l:l + 1], wg_, wu_, wd_, carry)
        if plan_names:
            absorb(plan_names, carried)
        return hout

    saved = {}
    h = h0
    kv = None
    for l in range(DEPTH):
        saved["a", l] = h
        h = ffn_forward("pre", l, h)
        saved["b", l] = h
        if l == 0:
            pool_w_full = full["pool_w"].reshape(N_A, 4, PG, PG)
            pool_scale_full = full["pool_scale"].reshape(N_A, d)
        if l < N_A:
            h, y = pool_fwd(f"pool_fwd{l}", h, mix_norm[l:l + 1], pool_w_full, pool_scale_full[l:l + 1], l)
            saved["y", l] = y
        else:
            j = l - N_A
            u = rms_fwd(f"mix_norm_fwd{l}", h, mix_norm[l:l + 1])
            cq_raw = mm_nn(f"q_down{l}", u, wdq_full[j], F32, 2048, 256, 1024)
            cq = rms_fwd(f"q_norm_fwd{l}", cq_raw, q_lora_norm[j:j + 1])
            q = q_heads(f"q_heads{l}", cq, wq256[j], cs)
            o, lse = flash_fwd(f"flash_fwd{l}", q, kv[0], kv[1])
            saved["att", l] = (u, cq_raw, cq, q, o, lse)
            h = mm_nn(f"attn_out{l}", o, wo_full[j], F32, 1024, 1024, 1024, res=h)
        saved["c", l] = h
        h = ffn_forward("post", l, h)
        if l == N_A - 1:
            wdkv256 = full["wdkv"][0]
            wdq_full = full["wdq"]
            wq256 = full["wq"].reshape(nb, nh, HD, HD)
            wo_full = full["wo"]
            u_kv = rms_fwd("kv_in_norm_fwd", h, kv_in_norm[None])
            kv_raw = mm_nn("kv_down", u_kv, wdkv256, F32, 2048, 256, 1024)
            c_kv, kr = kv_post("kv_post", kv_raw, ckv_norm[None], cs)
            kv = kv_heads("kv_heads", c_kv, kr, wuk_h, wuv_h)
            saved["kv"] = (u_kv, kv_raw, c_kv)

    sq, dh, g_final = loss_head("loss_head", h, final_norm[None], target)
    loss = lax.psum(0.5 * jnp.sum(sq) / d, ("x", "y", "c"))

    grads = {}
    gvec = {n: [None] * DEPTH for n in ("ffn_pre_norm", "mix_norm", "ffn_post_norm")}
    g_pool_scale, g_qnorm = [None] * N_A, [None] * nb
    dkv = None
    gfull, partial = {}, {}
    split = lambda g: g.reshape(g.shape[0], N_CHIPS, g.shape[1] // N_CHIPS, g.shape[2])

    def layer_grad_names(l):
        extra = [f"pool_w{l}"] if l < N_A else [f"wdq{l - N_A}", f"wq{l - N_A}", f"wo{l - N_A}"]
        return ffn_names("pre", l) + ffn_names("post", l) + extra + (["wdkv"] if l == N_A - 1 else [])

    def chip_sums(plan_names, received):
        for n, r in zip(plan_names, received):
            partial[n] = sum4(f"chip_sum_{n}", r)

    def ffn_backward(part, l, hin, dh):
        plan_names = layer_grad_names(l + 1)[(0 if part == "post" else 1)::2] if l + 1 < DEPTH else []
        if part == "pre" and l == 0:
            plan_names = plan_names + layer_grad_names(0)[3:]
        carry = ChipCopies("exchange", [gfull[n] for n in plan_names]) if plan_names else None
        wg_, wu_, wd_ = [full[n][0] for n in ffn_names(part, l)]
        (dh, u, dob, act, da, db, dgain), received = ffn_bwd(
            f"ffn_{part}_bwd{l}", hin, ffn_in[part][0][l:l + 1], wg_, wu_, wd_, dh, carry)
        if plan_names:
            chip_sums(plan_names, received)
        gvec[f"ffn_{part}_norm"][l] = dgain[0]
        for n, lhs, rhs in zip(ffn_names(part, l), (da, db, act), (u, u, dob)):
            gfull[n] = split(mm_tn(f"d_{n}", lhs, rhs, BF16, ffn_dim // 2, d, 1024)[None])
        return dh

    for l in reversed(range(DEPTH)):
        if l == N_A - 1:
            u_kv, kv_raw, c_kv = saved["kv"]
            dk, dv = dkv
            dc, dkr, dwuk, dwuv = kv_heads_bwd("kv_heads_bwd", dk, dv, c_kv, wuk_h, wuv_h)
            grads["w_uk"] = jnp.transpose(dwuk, (1, 0, 2))
            grads["w_uv"] = jnp.transpose(dwuv, (1, 0, 2))
            dkv_raw, dg_ckv = kv_post_bwd("kv_post_bwd", kv_raw, ckv_norm[None], dc, dkr, cs)
            grads["ckv_norm"] = dg_ckv[0]
            du_kv = mm_nt("kv_du", dkv_raw, wdkv256, F32, 2048, 1024, 256)
            gfull["wdkv"] = split(mm_tn("kv_dwdkv", u_kv, dkv_raw, BF16, 1024, 256, 2048)[None])
            dh, dg_kvin = rms_bwd("kv_in_norm_bwd", saved["a", l + 1], kv_in_norm[None], du_kv, dres=dh)
            grads["kv_in_norm"] = dg_kvin[0]

        dh = ffn_backward("post", l, saved["c", l], dh)

        if l < N_A:
            dh, dz, dgain, dscale = pool_bwd(f"pool_bwd{l}", saved["b", l], mix_norm[l:l + 1], saved["y", l],
                                             pool_w_full, pool_scale_full[l:l + 1], l, dh)
            gvec["mix_norm"][l] = dgain[0]
            g_pool_scale[l] = dscale[0]
            tk = _tile(s, 2048)
            gfull[f"pool_w{l}"] = split(_mm(
                f"pool_dw{l}", saved["y", l], dz, grid=(4, 1, s // tk),
                a_block=(tk, PG), a_map=lambda g, jj, kk: (kk, g), b_block=(tk, PG), b_map=lambda g, jj, kk: (kk, g),
                o_block=(None, PG, PG), o_map=lambda g, jj, kk: (g, 0, 0),
                out_shape=(4, PG, PG), dims=TN, out_dtype=BF16))
        else:
            j = l - N_A
            u, cq_raw, cq, q, o, lse = saved["att", l]
            k_, v_ = kv
            do = mm_nt(f"attn_do{l}", dh, wo_full[j], BF16, 1024, 1024, 1024)
            gfull[f"wo{j}"] = split(mm_tn(f"attn_dwo{l}", o, dh, BF16, 1024, 1024, 1024)[None])
            do2 = attn_do2(f"attn_do2_{l}", o, do, nh)
            dk_, dv_, dq_acc = flash_bwd(f"flash_bwd{l}", q, k_, v_, do2, _row_stats(lse, _tile(s, TC)), prev=dkv)
            dkv = (dk_, dv_)
            dcq, dwq = q_heads_bwd(f"q_heads_bwd{l}", dq_acc, cs, wq256[j], cq)
            gfull[f"wq{j}"] = split(dwq)
            dcq_raw, dg_q = rms_bwd(f"q_norm_bwd{l}", cq_raw, q_lora_norm[j:j + 1], dcq)
            g_qnorm[j] = dg_q[0]
            du = mm_nt(f"q_du{l}", dcq_raw, wdq_full[j], F32, 2048, 1024, 256)
            gfull[f"wdq{j}"] = split(mm_tn(f"q_dwdq{l}", u, dcq_raw, BF16, 1024, 256, 2048)[None])
            dh, dgain = rms_bwd(f"mix_norm_bwd{l}", saved["b", l], mix_norm[l:l + 1], du, dres=dh)
            gvec["mix_norm"][l] = dgain[0]

        dh = ffn_backward("pre", l, saved["a", l], dh)

    grad_x = dh[None]

    last_names = layer_grad_names(0)[:3]
    chip_sums(last_names, chip_copies("exchange_last", ChipCopies("exchange", [gfull[n] for n in last_names])))
    part_names = sorted(partial)
    sibling = dict(zip(part_names, sibling_exchange("exchange_sibling", [partial[n] for n in part_names])))
    pieces = {
        "pool_w": [f"pool_w{l}" for l in range(N_A)], "w_dkv": ["wdkv"], "w_dq": [f"wdq{j}" for j in range(nb)],
        "w_uq": [f"wq{j}" for j in range(nb)], "w_o": [f"wo{j}" for j in range(nb)],
    }
    for part in ("pre", "post"):
        for m in ("wg", "wu", "wd"):
            pieces[f"ffn_{part}_{m}"] = [f"{part}_{m}{l}" for l in range(DEPTH)]

    def natural(name, src):
        p = jnp.concatenate([src[n] for n in pieces[name]], axis=0)
        if name in ("ffn_pre_wg", "ffn_pre_wu", "ffn_post_wg", "ffn_post_wu"):
            return jnp.transpose(p, (0, 2, 1))
        if name == "w_dkv":
            return _fold_rot(p[0])
        if name == "w_uq":
            return jnp.transpose(_fold_rot(p).reshape(nb, nh, p.shape[1], QK_DIM), (0, 2, 1, 3))
        return p.reshape(weights[name].shape)

    results = {}
    for n in pieces:
        shp = weights[n].shape
        as2d = lambda a: a.reshape(-1, shp[-1])
        outs = adamw(f"adamw_{n}", [as2d(natural(n, partial)), as2d(natural(n, sibling))],
                     as2d(weights[n]), as2d(mom1[n]), as2d(mom2[n]))
        results[n] = [o_.reshape(shp) for o_ in outs]

    grads.update(ffn_pre_norm=jnp.stack(gvec["ffn_pre_norm"]), mix_norm=jnp.stack(gvec["mix_norm"]),
                 ffn_post_norm=jnp.stack(gvec["ffn_post_norm"]), final_norm=g_final[0],
                 q_lora_norm=jnp.stack(g_qnorm), pool_scale_full=jnp.stack(g_pool_scale))
    repl_shapes = [grads[n].shape for n in REPL_NAMES]
    packed_g, counts = _pack_rows([grads[n] for n in REPL_NAMES])
    all_g = all_gather8("gather_small_grads", packed_g)
    zeros_ps = jnp.zeros((N_A, d), F32)
    pick = lambda src: [zeros_ps if n == "pool_scale_full" else src[n] for n in REPL_NAMES]
    packed = [_pack_rows(pick(src))[0] for src in (weights, mom1, mom2)]
    outs = adamw("adamw_replicated", [all_g], *packed)
    unpacked = [_unpack_rows(o_, counts, repl_shapes) for o_ in outs]
    for idx, n in enumerate(REPL_NAMES):
        results[n] = [u_[idx] for u_ in unpacked]
    g_ps = lax.dynamic_index_in_dim(results["pool_scale_full"][0].reshape(N_A, N_CHIPS, PG), my_chip, axis=1, keepdims=False)
    results["pool_scale"] = adamw("adamw_pool_scale", [g_ps], pool_scale, m_pool_scale, v_pool_scale)

    out = [loss, grad_x]
    for part in range(4):
        out += [results[n][part] for n in names]
    return tuple(out)
```

```python
import functools

import jax
import jax.numpy as jnp
from jax import lax
from jax.experimental import pallas as pl
from jax.experimental.pallas import tpu as pltpu

F32 = jnp.float32
BF16 = jnp.bfloat16
MESH = pl.DeviceIdType.MESH

DEPTH = 4
N_A = 2
N_HEADS = 16
DN = 128
DR = 64
HD = 256
QK_DIM = DN + DR
POOL_WINDOWS = (2, 4, 8, 16)
PG = 256
HALO = 16
N_CHIPS = 4
RMS_EPS = 1e-6
ROPE_THETA = 10000.0
ATT_SCALE = QK_DIM ** -0.5
LOG2E = 1.4426950408889634
LN2 = 0.6931471805599453
Q_PRESCALE = ATT_SCALE * LOG2E
NEG = -1e30
LANES = 128

ADAM_LR = 0.001
ADAM_B1 = 0.9
ADAM_B2 = 0.999
ADAM_EPS = 1e-08
ADAM_WD = 0.01
ADAM_STEP = 10

VMEM_BYTES_V7X = 64 * 1024 * 1024
VMEM_LIMIT = VMEM_BYTES_V7X * 3 // 4
BIG_VMEM = VMEM_BYTES_V7X * 7 // 8
FFN_RB = 64

NN = ((1,), (0,))
NT = ((1,), (1,))
TN = ((0,), (0,))


def _cp(*sem):
    return pltpu.CompilerParams(dimension_semantics=sem, vmem_limit_bytes=VMEM_LIMIT)


def _dot(a, b, dims):
    return lax.dot_general(a, b, (dims, ((), ())), preferred_element_type=F32)


def _tile(n, pref):
    if n <= pref:
        return n
    if n % pref == 0:
        return pref
    t = 1 << (pref.bit_length() - 1)
    while n % t:
        t //= 2
    return t


def _rinv(x):
    return lax.rsqrt(jnp.mean(x * x, axis=-1, keepdims=True) + RMS_EPS)


def _sigmoid(a):
    return 1.0 / (1.0 + jnp.exp(-a))


def _mm(name, a, b, *, grid, a_block, a_map, b_block, b_map, o_block, o_map, out_shape, dims,
        out_dtype, res=None):
    nax = len(grid)
    nk = grid[-1]
    acc_shape = tuple(d for d in o_block if d is not None)
    has_res = res is not None

    def body(*refs):
        a_ref, b_ref = refs[0], refs[1]
        o_ref, acc = refs[-2], refs[-1]
        k = pl.program_id(nax - 1)

        @pl.when(k == 0)
        def _():
            acc[...] = jnp.zeros(acc_shape, F32)

        acc[...] += _dot(a_ref[...].astype(BF16), b_ref[...].astype(BF16), dims)

        @pl.when(k == nk - 1)
        def _():
            r = acc[...]
            if has_res:
                r = r + refs[2][...].astype(F32)
            o_ref[...] = r.astype(out_dtype)

    in_specs = [pl.BlockSpec(a_block, a_map), pl.BlockSpec(b_block, b_map)]
    args = [a, b]
    if has_res:
        in_specs.append(pl.BlockSpec(o_block, o_map))
        args.append(res)
    return pl.pallas_call(
        body, name=name, grid=grid, in_specs=in_specs, out_specs=pl.BlockSpec(o_block, o_map),
        out_shape=jax.ShapeDtypeStruct(out_shape, out_dtype),
        scratch_shapes=[pltpu.VMEM(acc_shape, F32)],
        compiler_params=_cp(*(("parallel",) * (nax - 1) + ("arbitrary",))),
    )(*args)


def mm_nn(name, a, b, out_dtype, tm, tn, tk, res=None):
    (m, kd), n = a.shape, b.shape[1]
    tm, tn, tk = _tile(m, tm), _tile(n, tn), _tile(kd, tk)
    return _mm(name, a, b, grid=(m // tm, n // tn, kd // tk),
               a_block=(tm, tk), a_map=lambda i, j, k: (i, k),
               b_block=(tk, tn), b_map=lambda i, j, k: (k, j),
               o_block=(tm, tn), o_map=lambda i, j, k: (i, j),
               out_shape=(m, n), dims=NN, out_dtype=out_dtype, res=res)


def mm_nt(name, a, b, out_dtype, tm, tn, tk):
    (m, kd), n = a.shape, b.shape[0]
    tm, tn, tk = _tile(m, tm), _tile(n, tn), _tile(kd, tk)
    return _mm(name, a, b, grid=(m // tm, n // tn, kd // tk),
               a_block=(tm, tk), a_map=lambda i, j, k: (i, k),
               b_block=(tn, tk), b_map=lambda i, j, k: (j, k),
               o_block=(tm, tn), o_map=lambda i, j, k: (i, j),
               out_shape=(m, n), dims=NT, out_dtype=out_dtype)


def mm_tn(name, a, b, out_dtype, tm, tn, tk):
    (kd, m), n = a.shape, b.shape[1]
    tm, tn, tk = _tile(m, tm), _tile(n, tn), _tile(kd, tk)
    return _mm(name, a, b, grid=(m // tm, n // tn, kd // tk),
               a_block=(tk, tm), a_map=lambda i, j, k: (k, i),
               b_block=(tk, tn), b_map=lambda i, j, k: (k, j),
               o_block=(tm, tn), o_map=lambda i, j, k: (i, j),
               out_shape=(m, n), dims=TN, out_dtype=out_dtype)


def rms_fwd(name, x, gain):
    s, d = x.shape
    tm = _tile(s, 1024)

    def body(x_ref, g_ref, o_ref):
        xv = x_ref[...]
        o_ref[...] = (xv * _rinv(xv) * g_ref[...]).astype(BF16)

    return pl.pallas_call(
        body, name=name, grid=(s // tm,),
        in_specs=[pl.BlockSpec((tm, d), lambda i: (i, 0)), pl.BlockSpec((1, d), lambda i: (0, 0))],
        out_specs=pl.BlockSpec((tm, d), lambda i: (i, 0)),
        out_shape=jax.ShapeDtypeStruct((s, d), BF16), compiler_params=_cp("parallel"),
    )(x, gain)


def _rms_bwd_math(xv, g, du):
    rinv = _rinv(xv)
    xhat = xv * rinv
    dgain = jnp.sum(du * xhat, axis=0, keepdims=True)
    dxh = du * g
    dx = rinv * (dxh - xhat * jnp.mean(dxh * xhat, axis=-1, keepdims=True))
    return dx, dgain


def rms_bwd(name, x, gain, du, dres=None):
    s, d = x.shape
    tm = _tile(s, 1024)
    has_res = dres is not None

    def body(*refs):
        x_ref, g_ref, du_ref = refs[:3]
        dx_ref, dg_ref = refs[-2:]

        @pl.when(pl.program_id(0) == 0)
        def _():
            dg_ref[...] = jnp.zeros((1, d), F32)

        dx, dgain = _rms_bwd_math(x_ref[...], g_ref[...], du_ref[...].astype(F32))
        if has_res:
            dx = dx + refs[3][...]
        dx_ref[...] = dx
        dg_ref[...] += dgain

    row = pl.BlockSpec((tm, d), lambda i: (i, 0))
    vec = pl.BlockSpec((1, d), lambda i: (0, 0))
    args = [x, gain, du] + ([dres] if has_res else [])
    return pl.pallas_call(
        body, name=name, grid=(s // tm,),
        in_specs=[row, vec, row] + ([row] if has_res else []),
        out_specs=[row, vec],
        out_shape=[jax.ShapeDtypeStruct((s, d), F32), jax.ShapeDtypeStruct((1, d), F32)],
        compiler_params=_cp("arbitrary"),
    )(*args)


def ffn_fwd(name, h, gain, wgT, wuT, wd, carry=None):
    s, d = h.shape
    f = wd.shape[0]
    tm, tf = _tile(s, 1024), _tile(f, 256)
    nf = f // tf
    carry = carry or _NoCarry()
    grid = (s // tm, nf)

    def body(*refs):
        (h_ref, g_ref, wg_ref, wu_ref, wd_ref), c_in, (o_ref,), c_out, (u_sc, acc), c_sem = carry.split(refs, 5, 1, 2)
        j = pl.program_id(1)
        carry.at_first_step(grid, c_in, c_out, c_sem)

        @pl.when(j == 0)
        def _():
            xv = h_ref[...]
            u_sc[...] = (xv * _rinv(xv) * g_ref[...]).astype(BF16)
            acc[...] = jnp.zeros((tm, d), F32)

        u = u_sc[...]
        a = _dot(u, wg_ref[...], NT)
        b = _dot(u, wu_ref[...], NT)
        act = (a * _sigmoid(a) * b).astype(BF16)
        acc[...] += _dot(act, wd_ref[...], NN)

        @pl.when(j == nf - 1)
        def _():
            o_ref[...] = h_ref[...] + 0.5 * acc[...]

        carry.at_last_step(grid, c_in, c_out, c_sem)

    row = pl.BlockSpec((tm, d), lambda i, j: (i, 0))
    wsp = pl.BlockSpec((tf, d), lambda i, j: (j, 0))
    outs = pl.pallas_call(
        body, name=name, grid=grid,
        in_specs=[row, pl.BlockSpec((1, d), lambda i, j: (0, 0)), wsp, wsp, wsp] + carry.in_specs(),
        out_specs=[row] + carry.out_specs(), out_shape=[jax.ShapeDtypeStruct((s, d), F32)] + carry.out_shapes(),
        scratch_shapes=[pltpu.VMEM((tm, d), BF16), pltpu.VMEM((tm, d), F32)] + carry.scratch(),
        compiler_params=_cp("arbitrary", "arbitrary"),
    )(h, gain, wgT, wuT, wd, *carry.arrays)
    return outs[0], list(outs[1:])


def ffn_bwd(name, h, gain, wgT, wuT, wd, dout, carry=None):
    s, d = h.shape
    f = wd.shape[0]
    tm, tf = _tile(s, 1024), _tile(f, 256)
    nf = f // tf
    halves = 2 if tm % (2 * FFN_RB) == 0 else 1
    hm = tm // halves
    carry = carry or _NoCarry()
    grid = (s // tm, nf)

    def body(*refs):
        ((h_ref, g_ref, wg_ref, wu_ref, wd_ref, do_ref), c_in,
         (dh_ref, u_ref, dob_ref, act_ref, da_ref, db_ref, dg_ref), c_out,
         (du_acc, a_sc, b_sc, c_sc), c_sem) = carry.split(refs, 6, 7, 4)
        i, j = pl.program_id(0), pl.program_id(1)
        carry.at_first_step(grid, c_in, c_out, c_sem)

        @pl.when(j == 0)
        def _():
            xv = h_ref[...]
            u_ref[...] = (xv * _rinv(xv) * g_ref[...]).astype(BF16)
            dob_ref[...] = (0.5 * do_ref[...]).astype(BF16)
            du_acc[...] = jnp.zeros((tm, d), F32)

        @pl.when((i == 0) & (j == 0))
        def _():
            dg_ref[...] = jnp.zeros((1, d), F32)

        wg, wu = wg_ref[...], wu_ref[...]
        for part in range(halves):
            prow = pl.ds(part * hm, hm)
            u = u_ref[prow, :]
            a_sc[prow, :] = _dot(u, wg, NT)
            b_sc[prow, :] = _dot(u, wu, NT)
            c_sc[prow, :] = _dot(dob_ref[prow, :], wd_ref[...], NT)
            for r in range(hm // FFN_RB):
                rows = pl.ds(part * hm + r * FFN_RB, FFN_RB)
                for t in range(tf // LANES):
                    cols = slice(t * LANES, (t + 1) * LANES)
                    a, b, dact = a_sc[rows, cols], b_sc[rows, cols], c_sc[rows, cols]
                    sig = _sigmoid(a)
                    sa = a * sig
                    act_ref[rows, cols] = (sa * b).astype(BF16)
                    db_ref[rows, cols] = (dact * sa).astype(BF16)
                    da_ref[rows, cols] = (dact * b * (sig * (1.0 + a * (1.0 - sig)))).astype(BF16)
            du_acc[prow, :] += _dot(da_ref[prow, :], wg, NN) + _dot(db_ref[prow, :], wu, NN)

        @pl.when(j == nf - 1)
        def _():
            dx, dgain = _rms_bwd_math(h_ref[...], g_ref[...], du_acc[...])
            dh_ref[...] = do_ref[...] + dx
            dg_ref[...] += dgain

        carry.at_last_step(grid, c_in, c_out, c_sem)

    row = pl.BlockSpec((tm, d), lambda i, j: (i, 0))
    vec = pl.BlockSpec((1, d), lambda i, j: (0, 0))
    wsp = pl.BlockSpec((tf, d), lambda i, j: (j, 0))
    hid = pl.BlockSpec((tm, tf), lambda i, j: (i, j))
    sd = lambda dt: jax.ShapeDtypeStruct((s, d), dt)
    sf = jax.ShapeDtypeStruct((s, f), BF16)
    outs = pl.pallas_call(
        body, name=name, grid=grid,
        in_specs=[row, vec, wsp, wsp, wsp, row] + carry.in_specs(),
        out_specs=[row, row, row, hid, hid, hid, vec] + carry.out_specs(),
        out_shape=[sd(F32), sd(BF16), sd(BF16), sf, sf, sf, jax.ShapeDtypeStruct((1, d), F32)] + carry.out_shapes(),
        scratch_shapes=[pltpu.VMEM((tm, d), F32)] + [pltpu.VMEM((tm, tf), F32)] * 3 + carry.scratch(),
        compiler_params=pltpu.CompilerParams(dimension_semantics=("arbitrary", "arbitrary"),
                                             vmem_limit_bytes=BIG_VMEM),
    )(h, gain, wgT, wuT, wd, dout, *carry.arrays)
    return outs[:7], list(outs[7:])


def pool_fwd(name, h, gain, pw, scale, layer):
    s, d = h.shape
    tm = _tile(s, 512)

    def body(h_ref, g_ref, pw_ref, sc_ref, o_ref, y_ref, ext):
        i = pl.program_id(0)

        @pl.when(i == 0)
        def _():
            ext[0:HALO, :] = jnp.zeros((HALO, d), F32)

        xv = h_ref[...]
        ext[HALO:HALO + tm, :] = xv * _rinv(xv) * g_ref[...]
        pos = i * tm + lax.broadcasted_iota(jnp.int32, (tm, 1), 0)
        for g, w in enumerate(POOL_WINDOWS):
            lo, hi = g * PG, (g + 1) * PG
            ug = ext[HALO:HALO + tm, lo:hi]
            tot = ug
            for k in range(1, w):
                tot = tot + ext[HALO - k:HALO - k + tm, lo:hi]
            cnt = jnp.minimum(pos + 1, w).astype(F32)
            yb = (tot / cnt - ug).astype(BF16)
            y_ref[:, lo:hi] = yb
            o_ref[:, lo:hi] = xv[:, lo:hi] + _dot(yb, pw_ref[g], NN) * sc_ref[:, lo:hi]
        ext[0:HALO, :] = ext[tm:tm + HALO, :]

    row = pl.BlockSpec((tm, d), lambda i: (i, 0))
    vec = pl.BlockSpec((1, d), lambda i: (0, 0))
    return pl.pallas_call(
        body, name=name, grid=(s // tm,),
        in_specs=[row, vec, pl.BlockSpec((None, 4, PG, PG), lambda i: (layer, 0, 0, 0)), vec],
        out_specs=[row, row],
        out_shape=[jax.ShapeDtypeStruct((s, d), F32), jax.ShapeDtypeStruct((s, d), BF16)],
        scratch_shapes=[pltpu.VMEM((HALO + tm, d), F32)],
        compiler_params=_cp("arbitrary"),
    )(h, gain, pw, scale)


def pool_bwd(name, h, gain, y, pw, scale, layer, dout):
    s, d = h.shape
    tm = _tile(s, 512)
    ns = s // tm

    def body(h_ref, g_ref, y_ref, pw_ref, sc_ref, do_ref, dh_ref, dz_ref, dg_ref, dsc_ref, ext, du_sc):
        i = pl.program_id(0)
        t = ns - 1 - i

        @pl.when(i == 0)
        def _():
            ext[tm:tm + HALO, :] = jnp.zeros((HALO, d), F32)
            dg_ref[...] = jnp.zeros((1, d), F32)
            dsc_ref[...] = jnp.zeros((1, d), F32)

        pos = t * tm + lax.broadcasted_iota(jnp.int32, (tm, 1), 0)
        for g, w in enumerate(POOL_WINDOWS):
            lo, hi = g * PG, (g + 1) * PG
            dog = do_ref[:, lo:hi]
            z = _dot(y_ref[:, lo:hi], pw_ref[g], NN)
            dsc_ref[:, lo:hi] += jnp.sum(dog * z, axis=0, keepdims=True)
            dzb = (dog * sc_ref[:, lo:hi]).astype(BF16)
            dz_ref[:, lo:hi] = dzb
            dy = _dot(dzb, pw_ref[g], NT)
            cnt = jnp.minimum(pos + 1, w).astype(F32)
            ext[0:tm, lo:hi] = dy / cnt
            tot = ext[0:tm, lo:hi]
            for k in range(1, w):
                tot = tot + ext[k:k + tm, lo:hi]
            du_sc[:, lo:hi] = tot - dy
        ext[tm:tm + HALO, :] = ext[0:HALO, :]
        dx, dgain = _rms_bwd_math(h_ref[...], g_ref[...], du_sc[...])
        dh_ref[...] = do_ref[...] + dx
        dg_ref[...] += dgain

    row = pl.BlockSpec((tm, d), lambda i: (ns - 1 - i, 0))
    vec = pl.BlockSpec((1, d), lambda i: (0, 0))
    return pl.pallas_call(
        body, name=name, grid=(ns,),
        in_specs=[row, vec, row, pl.BlockSpec((None, 4, PG, PG), lambda i: (layer, 0, 0, 0)), vec, row],
        out_specs=[row, row, vec, vec],
        out_shape=[jax.ShapeDtypeStruct((s, d), F32), jax.ShapeDtypeStruct((s, d), BF16),
                   jax.ShapeDtypeStruct((1, d), F32), jax.ShapeDtypeStruct((1, d), F32)],
        scratch_shapes=[pltpu.VMEM((tm + HALO, d), F32), pltpu.VMEM((tm, d), F32)],
        compiler_params=_cp("arbitrary"),
    )(h, gain, y, pw, scale, dout)


def _rope128(raw_hi, cs):
    b = raw_hi * cs
    r = b + pltpu.roll(b, DR, 1)
    lane = lax.broadcasted_iota(jnp.int32, r.shape, 1)
    return jnp.where(lane < DR, r, 0.0)


def _rope128_bwd(t, cs):
    return (t + pltpu.roll(t, DR, 1)) * cs


def q_heads(name, cq, w256, cs):
    s = cq.shape[0]
    nh = w256.shape[0]
    tm = _tile(s, 1024)

    def body(cq_ref, w_ref, cs_ref, o_ref):
        r = _dot(cq_ref[...], w_ref[...], NN) * Q_PRESCALE
        o_ref[:, 0:DN] = r[:, 0:DN].astype(BF16)
        o_ref[:, DN:HD] = _rope128(r[:, DN:HD], cs_ref[...]).astype(BF16)

    return pl.pallas_call(
        body, name=name, grid=(s // tm, nh),
        in_specs=[pl.BlockSpec((tm, HD), lambda i, hh: (i, 0)),
                  pl.BlockSpec((None, HD, HD), lambda i, hh: (hh, 0, 0)),
                  pl.BlockSpec((tm, DN), lambda i, hh: (i, 0))],
        out_specs=pl.BlockSpec((None, tm, HD), lambda i, hh: (hh, i, 0)),
        out_shape=jax.ShapeDtypeStruct((nh, s, HD), BF16),
        compiler_params=_cp("parallel", "parallel"),
    )(cq, w256, cs)


def kv_post(name, kv_raw, gain, cs):
    s = kv_raw.shape[0]
    tm = _tile(s, 1024)

    def body(x_ref, g_ref, cs_ref, c_ref, kr_ref):
        c = x_ref[:, 0:DN]
        c_ref[...] = (c * _rinv(c) * g_ref[...]).astype(BF16)
        kr_ref[...] = _rope128(x_ref[:, DN:HD], cs_ref[...]).astype(BF16)

    half = pl.BlockSpec((tm, DN), lambda i: (i, 0))
    return pl.pallas_call(
        body, name=name, grid=(s // tm,),
        in_specs=[pl.BlockSpec((tm, HD), lambda i: (i, 0)), pl.BlockSpec((1, DN), lambda i: (0, 0)), half],
        out_specs=[half, half],
        out_shape=[jax.ShapeDtypeStruct((s, DN), BF16), jax.ShapeDtypeStruct((s, DN), BF16)],
        compiler_params=_cp("parallel"),
    )(kv_raw, gain, cs)


def kv_heads(name, c_kv, kr, wuk, wuv):
    s = c_kv.shape[0]
    nh = wuk.shape[0]
    tm = _tile(s, 1024)

    def body(c_ref, kr_ref, wk_ref, wv_ref, k_ref, v_ref):
        c = c_ref[...]
        k_ref[:, 0:DN] = _dot(c, wk_ref[...], NN).astype(BF16)
        k_ref[:, DN:HD] = kr_ref[...]
        v_ref[:, 0:DN] = _dot(c, wv_ref[...], NN).astype(BF16)
        v_ref[:, DN:HD] = jnp.ones((tm, DN), BF16)

    half = pl.BlockSpec((tm, DN), lambda i, hh: (i, 0))
    wsp = pl.BlockSpec((None, DN, DN), lambda i, hh: (hh, 0, 0))
    head = pl.BlockSpec((None, tm, HD), lambda i, hh: (hh, i, 0))
    return pl.pallas_call(
        body, name=name, grid=(s // tm, nh),
        in_specs=[half, half, wsp, wsp], out_specs=[head, head],
        out_shape=[jax.ShapeDtypeStruct((nh, s, HD), BF16), jax.ShapeDtypeStruct((nh, s, HD), BF16)],
        compiler_params=_cp("parallel", "parallel"),
    )(c_kv, kr, wuk, wuv)


def kv_heads_bwd(name, dk, dv, c_kv, wuk, wuv):
    nh, s, _ = dk.shape
    tm = _tile(s, 1024)

    def body(dk_ref, dv_ref, c_ref, wk_ref, wv_ref, dc_ref, dkr_ref, dwk_ref, dwv_ref):
        i, hh = pl.program_id(0), pl.program_id(1)

        @pl.when(hh == 0)
        def _():
            dc_ref[...] = jnp.zeros((tm, DN), F32)
            dkr_ref[...] = jnp.zeros((tm, DN), F32)

        @pl.when(i == 0)
        def _():
            dwk_ref[hh] = jnp.zeros((DN, DN), F32)
            dwv_ref[hh] = jnp.zeros((DN, DN), F32)

        dkn = dk_ref[:, 0:DN].astype(BF16)
        dvb = dv_ref[...].astype(BF16)
        c = c_ref[...]
        dc_ref[...] += _dot(dkn, wk_ref[...], NT) + _dot(dvb, wv_ref[...], NT)
        dkr_ref[...] += dk_ref[:, DN:HD]
        dwk_ref[hh] += _dot(c, dkn, TN)
        dwv_ref[hh] += _dot(c, dvb, TN)

    half = pl.BlockSpec((tm, DN), lambda i, hh: (i, 0))
    wsp = pl.BlockSpec((None, DN, DN), lambda i, hh: (hh, 0, 0))
    allw = pl.BlockSpec((nh, DN, DN), lambda i, hh: (0, 0, 0))
    return pl.pallas_call(
        body, name=name, grid=(s // tm, nh),
        in_specs=[pl.BlockSpec((None, tm, HD), lambda i, hh: (hh, i, 0)),
                  pl.BlockSpec((None, tm, DN), lambda i, hh: (hh, i, 0)), half, wsp, wsp],
        out_specs=[half, half, allw, allw],
        out_shape=[jax.ShapeDtypeStruct((s, DN), F32), jax.ShapeDtypeStruct((s, DN), F32),
                   jax.ShapeDtypeStruct((nh, DN, DN), F32), jax.ShapeDtypeStruct((nh, DN, DN), F32)],
        compiler_params=_cp("arbitrary", "arbitrary"),
    )(dk, dv, c_kv, wuk, wuv)


def kv_post_bwd(name, kv_raw, gain, dc, dkr, cs):
    s = kv_raw.shape[0]
    tm = _tile(s, 1024)

    def body(x_ref, g_ref, dc_ref, dkr_ref, cs_ref, o_ref, dg_ref):
        @pl.when(pl.program_id(0) == 0)
        def _():
            dg_ref[...] = jnp.zeros((1, DN), F32)

        dx, dgain = _rms_bwd_math(x_ref[:, 0:DN], g_ref[...], dc_ref[...])
        o_ref[:, 0:DN] = dx.astype(BF16)
        o_ref[:, DN:HD] = _rope128_bwd(dkr_ref[...], cs_ref[...]).astype(BF16)
        dg_ref[...] += dgain

    half = pl.BlockSpec((tm, DN), lambda i: (i, 0))
    vec = pl.BlockSpec((1, DN), lambda i: (0, 0))
    full = pl.BlockSpec((tm, HD), lambda i: (i, 0))
    return pl.pallas_call(
        body, name=name, grid=(s // tm,), in_specs=[full, vec, half, half, half], out_specs=[full, vec],
        out_shape=[jax.ShapeDtypeStruct((s, HD), BF16), jax.ShapeDtypeStruct((1, DN), F32)],
        compiler_params=_cp("arbitrary"),
    )(kv_raw, gain, dc, dkr, cs)


TQ = 1024
TC = 512
RB = 64


def _flash_cp():
    return pltpu.CompilerParams(dimension_semantics=("parallel", "arbitrary"), vmem_limit_bytes=BIG_VMEM)


def _causal(x, row0, col0, keys_on_rows=False):
    r = row0 + lax.broadcasted_iota(jnp.int32, x.shape, 0)
    c = col0 + lax.broadcasted_iota(jnp.int32, x.shape, 1)
    return jnp.where((r <= c) if keys_on_rows else (c <= r), x, NEG)


def _chunk_pipeline(n_pairs, chunk_of, scores, finish, bufs, last_from):
    b0, b1 = bufs
    scores(chunk_of(0), b0, 0)

    def pair_at(m):
        scores(chunk_of(m + 1), b1, 0)
        finish(chunk_of(m), b0, False, 0)
        scores(chunk_of(m + 2), b0, 0)
        finish(chunk_of(m + 1), b1, False, 0)

    def two_pairs(t, carry):
        pair_at(4 * t)
        pair_at(4 * t + 2)
        return carry

    half = lax.div(n_pairs, 2)
    lax.fori_loop(0, half, two_pairs, 0)

    @pl.when(lax.rem(n_pairs, 2) == 1)
    def _():
        pair_at(4 * half)

    m = 2 * n_pairs
    scores(chunk_of(m + 1), b1, last_from)
    finish(chunk_of(m), b0, True, 0)
    finish(chunk_of(m + 1), b1, True, last_from)


def flash_fwd(name, q, k, v2):
    nh, s, _ = q.shape
    tq, tk = _tile(s, TQ), _tile(s, TC)
    assert tq == 2 * tk
    nt = tk // LANES

    def body(q_ref, k_ref, v_ref, o_ref, lse_ref, s0_sc, s1_sc, p_sc, m_sc, acc):
        i = pl.program_id(1)
        m_sc[...] = jnp.full((tq, LANES), NEG, F32)
        acc[...] = jnp.zeros((tq, HD), F32)

        def scores(j, dst, row0):
            dst[row0:, :] = _dot(q_ref[row0:, :], k_ref[pl.ds(pl.multiple_of(j * tk, tk), tk), :], NT)

        def softmax_rows(src, r0, j, masked):
            rows = pl.ds(r0, RB)

            def tile(t):
                x = src[rows, t * LANES:(t + 1) * LANES]
                return _causal(x, i * tq + r0, j * tk + t * LANES) if masked else x

            mx = tile(0)
            for t in range(1, nt):
                mx = jnp.maximum(mx, tile(t))
            m_prev = m_sc[rows, :]
            m_new = jnp.maximum(m_prev, jnp.max(mx, axis=-1, keepdims=True))
            alpha = jnp.exp2(m_prev - m_new)
            for t in range(nt):
                p_sc[rows, t * LANES:(t + 1) * LANES] = jnp.exp2(tile(t) - m_new).astype(BF16)
            m_sc[rows, :] = m_new
            acc[rows, 0:DN] = alpha * acc[rows, 0:DN]
            acc[rows, DN:HD] = alpha * acc[rows, DN:HD]

        def finish(j, src, masked, row0):
            for r in range(row0 // RB, tq // RB):
                softmax_rows(src, r * RB, j, masked)
            acc[row0:, :] += _dot(p_sc[row0:, :], v_ref[pl.ds(pl.multiple_of(j * tk, tk), tk), :], NN)

        _chunk_pipeline(i, lambda m: m, scores, finish, (s0_sc, s1_sc), tk)
        l = acc[:, DN:HD]
        o_ref[...] = (acc[:, 0:DN] / l).astype(BF16)
        lse_ref[...] = m_sc[...] + jnp.log2(l)

    whole = pl.BlockSpec((None, s, HD), lambda hh, i: (hh, 0, 0))
    return pl.pallas_call(
        body, name=name, grid=(nh, s // tq),
        in_specs=[pl.BlockSpec((None, tq, HD), lambda hh, i: (hh, i, 0)), whole, whole],
        out_specs=[pl.BlockSpec((tq, DN), lambda hh, i: (i, hh)),
                   pl.BlockSpec((None, tq, DN), lambda hh, i: (hh, i, 0))],
        out_shape=[jax.ShapeDtypeStruct((s, nh * DN), BF16), jax.ShapeDtypeStruct((nh, s, DN), F32)],
        scratch_shapes=[pltpu.VMEM((tq, tk), F32), pltpu.VMEM((tq, tk), F32), pltpu.VMEM((tq, tk), BF16),
                        pltpu.VMEM((tq, LANES), F32), pltpu.VMEM((tq, HD), F32)],
        compiler_params=_flash_cp(),
    )(q, k, v2)


def attn_do2(name, o, do, nh):
    s = o.shape[0]
    tm = _tile(s, 512)

    def body(o_ref, do_ref, d_ref):
        lane = lax.broadcasted_iota(jnp.int32, (tm, DN), 1)
        for hh in range(nh):
            cols = slice(hh * DN, (hh + 1) * DN)
            dov = do_ref[:, cols]
            delta = jnp.sum(o_ref[:, cols].astype(F32) * dov.astype(F32), axis=-1, keepdims=True)
            hi = delta.astype(BF16).astype(F32)
            lo = delta - hi
            d_ref[hh, :, 0:DN] = dov
            d_ref[hh, :, DN:HD] = jnp.where(lane == 0, -hi, jnp.where(lane == 1, -lo, 0.0)).astype(BF16)

    blk = pl.BlockSpec((tm, nh * DN), lambda i: (i, 0))
    return pl.pallas_call(
        body, name=name, grid=(s // tm,), in_specs=[blk, blk],
        out_specs=pl.BlockSpec((nh, tm, HD), lambda i: (0, i, 0)),
        out_shape=jax.ShapeDtypeStruct((nh, s, HD), BF16), compiler_params=_cp("parallel"),
    )(o, do)


def q_heads_bwd(name, dq_acc, cs, w256, cq):
    nh, s, _ = dq_acc.shape
    tm = _tile(s, 1024)
    ns = s // tm

    def body(a_ref, cs_ref, w_ref, cq_ref, dcq_ref, dw_ref, dw_acc):
        i, hh = pl.program_id(0), pl.program_id(1)

        @pl.when(hh == 0)
        def _():
            dcq_ref[...] = jnp.zeros((tm, HD), F32)

        @pl.when(i == 0)
        def _():
            dw_acc[hh] = jnp.zeros((HD, HD), F32)

        draw = jnp.concatenate(
            [(a_ref[:, 0:DN] * ATT_SCALE).astype(BF16),
             _rope128_bwd(a_ref[:, DN:HD] * ATT_SCALE, cs_ref[...]).astype(BF16)], axis=-1)
        dcq_ref[...] += _dot(draw, w_ref[...], NT)
        dw_acc[hh] += _dot(cq_ref[...], draw, TN)

        @pl.when(i == ns - 1)
        def _():
            dw_ref[hh] = dw_acc[hh].astype(BF16)

    row = pl.BlockSpec((tm, HD), lambda i, hh: (i, 0))
    return pl.pallas_call(
        body, name=name, grid=(ns, nh),
        in_specs=[pl.BlockSpec((None, tm, HD), lambda i, hh: (hh, i, 0)), pl.BlockSpec((tm, DN), lambda i, hh: (i, 0)),
                  pl.BlockSpec((None, HD, HD), lambda i, hh: (hh, 0, 0)), row],
        out_specs=[row, pl.BlockSpec((nh, HD, HD), lambda i, hh: (0, 0, 0))],
        out_shape=[jax.ShapeDtypeStruct((s, HD), F32), jax.ShapeDtypeStruct((nh, HD, HD), BF16)],
        scratch_shapes=[pltpu.VMEM((nh, HD, HD), F32)],
        compiler_params=_cp("arbitrary", "arbitrary"),
    )(dq_acc, cs, w256, cq)


def flash_bwd(name, q, k, v2, do2, lse_row, prev=None):
    nh, s, _ = q.shape
    tk, tq = _tile(s, TQ), _tile(s, TC)
    nqc = s // tq
    per = tk // tq
    nt = tq // LANES
    nkt = s // tk
    has_prev = prev is not None

    def body(*refs):
        k_ref, v_ref, q_ref, do_ref, lse_ref = refs[:5]
        dk_ref, dv_ref, dq_hbm, s_sc, d_sc, p_sc, ds_sc, dk_acc, dv_acc, dq_sc, sem = refs[-11:]
        hh, i = pl.program_id(0), pl.program_id(1)
        dk_acc[...] = jnp.zeros((tk, HD), F32)
        dv_acc[...] = jnp.zeros((tk, DN), F32)

        @pl.when(i == 0)
        def _():
            dq_sc[...] = jnp.zeros((s, HD), F32)

        def p_rows(r0, j, masked):
            rows = pl.ds(r0, RB)
            for t in range(nt):
                cols = slice(t * LANES, (t + 1) * LANES)
                x = s_sc[rows, cols]
                if masked:
                    x = _causal(x, i * tk + r0, j * tq + t * LANES, keys_on_rows=True)
                p = jnp.exp2(x - jnp.tile(lse_ref[j][:, cols], (RB // 8, 1)))
                p_sc[rows, cols] = p.astype(BF16)
                ds_sc[rows, cols] = (p * d_sc[rows, cols]).astype(BF16)

        def step(j, masked):
            start = pl.multiple_of(j * tq, tq)
            qj = q_ref[pl.ds(start, tq), :]
            s_sc[...] = _dot(k_ref[...], qj, NT)
            d_sc[...] = _dot(v_ref[...], do_ref[pl.ds(start, tq), :], NT)
            for r in range(tk // RB):
                p_rows(r * RB, j, masked)
            dv_acc[...] += _dot(p_sc[...], do_ref[pl.ds(start, tq), 0:DN], NN)
            ds = ds_sc[...]
            dk_acc[...] += _dot(ds, qj, NN)
            dq_sc[pl.ds(start, tq), :] += _dot(ds, k_ref[...], TN)

        def full_step(m, carry):
            step(nqc - 1 - m, False)
            return carry

        lax.fori_loop(0, nqc - (i + 1) * per, full_step, 0)
        for dd in reversed(range(per)):
            step(i * per + dd, True)
        if has_prev:
            dk_ref[...] = dk_acc[...] * LN2 + refs[5][...]
            dv_ref[...] = dv_acc[...] + refs[6][...]
        else:
            dk_ref[...] = dk_acc[...] * LN2
            dv_ref[...] = dv_acc[...]

        @pl.when(i == nkt - 1)
        def _():
            cp = pltpu.make_async_copy(dq_sc, dq_hbm.at[hh], sem)
            cp.start()
            cp.wait()

    kblk = pl.BlockSpec((None, tk, HD), lambda hh, i: (hh, i, 0))
    vblk = pl.BlockSpec((None, tk, DN), lambda hh, i: (hh, i, 0))
    whole = pl.BlockSpec((None, s, HD), lambda hh, i: (hh, 0, 0), pipeline_mode=pl.Buffered(1))
    in_specs = [kblk, kblk, whole, whole, pl.BlockSpec((None, nqc, 8, tq), lambda hh, i: (hh, 0, 0, 0))]
    args = [k, v2, q, do2, lse_row]
    if has_prev:
        in_specs += [kblk, vblk]
        args += list(prev)
    big = jax.ShapeDtypeStruct((nh, s, HD), F32)
    return pl.pallas_call(
        body, name=name, grid=(nh, nkt), in_specs=in_specs, out_specs=[kblk, vblk, ANY],
        out_shape=[big, jax.ShapeDtypeStruct((nh, s, DN), F32), big],
        scratch_shapes=[pltpu.VMEM((tk, tq), F32)] * 2 + [pltpu.VMEM((tk, tq), BF16)] * 2
        + [pltpu.VMEM((tk, HD), F32), pltpu.VMEM((tk, DN), F32), pltpu.VMEM((s, HD), F32), pltpu.SemaphoreType.DMA],
        compiler_params=pltpu.CompilerParams(dimension_semantics=("arbitrary", "arbitrary"), vmem_limit_bytes=BIG_VMEM),
    )(*args)


def loss_head(name, h, gain, target):
    s, d = h.shape
    tm = _tile(s, 1024)

    def body(h_ref, g_ref, t_ref, sq_ref, dh_ref, dg_ref):
        @pl.when(pl.program_id(0) == 0)
        def _():
            sq_ref[...] = jnp.zeros((1, d), F32)
            dg_ref[...] = jnp.zeros((1, d), F32)

        xv = h_ref[...]
        g = g_ref[...]
        rinv = _rinv(xv)
        xhat = xv * rinv
        err = xhat * g - t_ref[...]
        sq_ref[...] += jnp.sum(err * err, axis=0, keepdims=True)
        dy = err / d
        dg_ref[...] += jnp.sum(dy * xhat, axis=0, keepdims=True)
        dxh = dy * g
        dh_ref[...] = rinv * (dxh - xhat * jnp.mean(dxh * xhat, axis=-1, keepdims=True))

    row = pl.BlockSpec((tm, d), lambda i: (i, 0))
    vec = pl.BlockSpec((1, d), lambda i: (0, 0))
    return pl.pallas_call(
        body, name=name, grid=(s // tm,), in_specs=[row, vec, row], out_specs=[vec, row, vec],
        out_shape=[jax.ShapeDtypeStruct((1, d), F32), jax.ShapeDtypeStruct((s, d), F32),
                   jax.ShapeDtypeStruct((1, d), F32)],
        compiler_params=_cp("arbitrary"),
    )(h, gain, target)


def _adamw_math(w, g, m, v):
    m2 = ADAM_B1 * m + (1.0 - ADAM_B1) * g
    v2 = ADAM_B2 * v + (1.0 - ADAM_B2) * (g * g)
    mh = m2 / (1.0 - ADAM_B1 ** ADAM_STEP)
    vh = v2 / (1.0 - ADAM_B2 ** ADAM_STEP)
    delta = -ADAM_LR * (mh / (jnp.sqrt(vh) + ADAM_EPS) + ADAM_WD * w)
    return delta, m2, v2


def adamw(name, parts, w, m, v):
    rows, cols = w.shape
    tr = _tile(rows, max(8, (1 << 18) // cols))
    stacked = parts[0].ndim == 3
    npart = len(parts)

    def body(*refs):
        p_refs = refs[:npart]
        w_ref, m_ref, v_ref, g_ref, d_ref, m2_ref, v2_ref = refs[npart:]
        if stacked:
            g = p_refs[0][0]
            for n in range(1, parts[0].shape[0]):
                g = g + p_refs[0][n]
        else:
            g = p_refs[0][...]
            for r in p_refs[1:]:
                g = g + r[...]
        delta, m2, v2 = _adamw_math(w_ref[...], g, m_ref[...], v_ref[...])
        g_ref[...] = g
        d_ref[...] = delta
        m2_ref[...] = m2
        v2_ref[...] = v2

    blk = pl.BlockSpec((tr, cols), lambda i: (i, 0))
    pblk = pl.BlockSpec((parts[0].shape[0], tr, cols), lambda i: (0, i, 0)) if stacked else blk
    sds = jax.ShapeDtypeStruct((rows, cols), F32)
    return pl.pallas_call(
        body, name=name, grid=(rows // tr,), in_specs=[pblk] * npart + [blk] * 3,
        out_specs=[blk] * 4, out_shape=[sds] * 4, compiler_params=_cp("parallel"),
    )(*parts, w, m, v)


def sum4(name, x):
    _, a, r, c = x.shape
    tr = _tile(r, max(8, (1 << 18) // c))

    def body(x_ref, o_ref):
        o_ref[...] = ((x_ref[3].astype(F32) + x_ref[0].astype(F32)) + x_ref[1].astype(F32)) + x_ref[2].astype(F32)

    return pl.pallas_call(
        body, name=name, grid=(a, r // tr),
        in_specs=[pl.BlockSpec((4, None, tr, c), lambda i, j: (0, i, j, 0))],
        out_specs=pl.BlockSpec((None, tr, c), lambda i, j: (i, j, 0)),
        out_shape=jax.ShapeDtypeStruct((a, r, c), F32), compiler_params=_cp("parallel", "parallel"),
    )(x)


ANY = pl.BlockSpec(memory_space=pl.ANY)


def _mesh_pos():
    return lax.axis_index("x"), lax.axis_index("y"), lax.axis_index("c")


def _chip_peers(x, y):
    return [(1 - x, y), (x, 1 - y), (1 - x, 1 - y)]


class _NoCarry:
    arrays = ()

    def split(self, refs, n_in, n_out, n_scratch):
        return (refs[:n_in], (), refs[n_in:n_in + n_out], (), refs[n_in + n_out:n_in + n_out + n_scratch], ())

    def in_specs(self):
        return []

    out_specs = out_shapes = scratch = in_specs

    def at_first_step(self, grid, ins, outs, sems):
        pass

    at_last_step = at_first_step


class ChipCopies:
    def __init__(self, kind, arrays):
        assert kind in ("gather", "exchange")
        self.kind, self.arrays, self.n = kind, tuple(arrays), len(arrays)

    def split(self, refs, n_in, n_out, n_scratch):
        n, pos, parts = self.n, 0, []
        for count in (n_in, n, n_out, n, n_scratch, 3):
            parts.append(refs[pos:pos + count])
            pos += count
        return parts

    def in_specs(self):
        return [ANY] * self.n

    out_specs = in_specs

    def out_shapes(self):
        if self.kind == "gather":
            return [jax.ShapeDtypeStruct((a.shape[0], N_CHIPS) + a.shape[1:], a.dtype) for a in self.arrays]
        return [jax.ShapeDtypeStruct((N_CHIPS, a.shape[0]) + a.shape[2:], a.dtype) for a in self.arrays]

    def scratch(self):
        n = self.n
        return [pltpu.SemaphoreType.DMA((3 * n,)), pltpu.SemaphoreType.DMA((3 * n,)), pltpu.SemaphoreType.DMA((n,))]

    def _copies(self, ins, outs, sems):
        send_sems, recv_sems, loc_sems = sems
        x, y, c = _mesh_pos()
        me = 2 * x + y
        peers = _chip_peers(x, y)
        gather = self.kind == "gather"

        def remote(a, j, block):
            px, py = peers[j]
            src = ins[a] if gather else ins[a].at[:, 2 * px + py]
            dst = outs[a].at[:, block] if gather else outs[a].at[j]
            return pltpu.make_async_remote_copy(
                src_ref=src, dst_ref=dst, send_sem=send_sems.at[a * 3 + j], recv_sem=recv_sems.at[a * 3 + j],
                device_id=(px, py, c), device_id_type=MESH)

        pairs = [(a, j) for a in range(self.n) for j in range(3)]
        if gather:
            local = [pltpu.make_async_copy(ins[a], outs[a].at[:, me], loc_sems.at[a]) for a in range(self.n)]
        else:
            local = [pltpu.make_async_copy(ins[a].at[:, me], outs[a].at[3], loc_sems.at[a]) for a in range(self.n)]
        sends = [remote(a, j, me) for a, j in pairs]
        recvs = [remote(a, j, 2 * peers[j][0] + peers[j][1]) for a, j in pairs]
        return local, sends, recvs

    def start(self, ins, outs, sems):
        local, sends, _ = self._copies(ins, outs, sems)
        for cp in local + sends:
            cp.start()

    def wait(self, ins, outs, sems):
        local, sends, recvs = self._copies(ins, outs, sems)
        for cp in recvs:
            cp.wait_recv()
        for cp in sends:
            cp.wait_send()
        for cp in local:
            cp.wait()

    @staticmethod
    def _is_step(grid, last):
        cond = None
        for ax, size in enumerate(grid):
            c = pl.program_id(ax) == (size - 1 if last else 0)
            cond = c if cond is None else cond & c
        return cond

    def at_first_step(self, grid, ins, outs, sems):
        @pl.when(self._is_step(grid, False))
        def _():
            self.start(ins, outs, sems)

    def at_last_step(self, grid, ins, outs, sems):
        @pl.when(self._is_step(grid, True))
        def _():
            self.wait(ins, outs, sems)


def chip_copies(name, copies):
    n = copies.n

    def body(*refs):
        ins, outs, sems = refs[:n], refs[n:2 * n], refs[2 * n:]
        copies.start(ins, outs, sems)
        copies.wait(ins, outs, sems)

    return pl.pallas_call(
        body, name=name, in_specs=copies.in_specs(), out_specs=copies.out_specs(),
        out_shape=copies.out_shapes(), scratch_shapes=copies.scratch(),
    )(*copies.arrays)


def sibling_exchange(name, arrs):
    n = len(arrs)

    def body(*refs):
        ins, outs = refs[:n], refs[n:2 * n]
        send_sems, recv_sems = refs[2 * n:]
        x, y, c = _mesh_pos()
        copies = [pltpu.make_async_remote_copy(
            src_ref=ins[a], dst_ref=outs[a], send_sem=send_sems.at[a], recv_sem=recv_sems.at[a],
            device_id=(x, y, 1 - c), device_id_type=MESH) for a in range(n)]
        for cp in copies:
            cp.start()
        for cp in copies:
            cp.wait_recv()
        for cp in copies:
            cp.wait_send()

    return pl.pallas_call(
        body, name=name, in_specs=[ANY] * n, out_specs=[ANY] * n,
        out_shape=[jax.ShapeDtypeStruct(a.shape, a.dtype) for a in arrs],
        scratch_shapes=[pltpu.SemaphoreType.DMA((n,)), pltpu.SemaphoreType.DMA((n,))],
    )(*arrs)


def all_gather8(name, buf):
    r, cdim = buf.shape

    def body(in_ref, out_ref, send_sems, recv_sems, loc_sem):
        x, y, c = _mesh_pos()
        flips = [(fx, fy, fc) for fx in (0, 1) for fy in (0, 1) for fc in (0, 1)][1:]

        def peer(f):
            return tuple((1 - p) if b else p for p, b in zip((x, y, c), f))

        def remote(j, slot):
            return pltpu.make_async_remote_copy(
                src_ref=in_ref, dst_ref=out_ref.at[slot], send_sem=send_sems.at[j], recv_sem=recv_sems.at[j],
                device_id=peer(flips[j]), device_id_type=MESH)

        me = 4 * x + 2 * y + c
        local = pltpu.make_async_copy(in_ref, out_ref.at[me], loc_sem)
        local.start()
        sends = [remote(j, me) for j in range(7)]
        for cp in sends:
            cp.start()
        for j in range(7):
            px, py, pc = peer(flips[j])
            remote(j, 4 * px + 2 * py + pc).wait_recv()
        for cp in sends:
            cp.wait_send()
        local.wait()

    return pl.pallas_call(
        body, name=name, in_specs=[ANY], out_specs=ANY,
        out_shape=jax.ShapeDtypeStruct((8, r, cdim), buf.dtype),
        scratch_shapes=[pltpu.SemaphoreType.DMA((7,)), pltpu.SemaphoreType.DMA((7,)), pltpu.SemaphoreType.DMA],
    )(buf)


def _rot_cols(w):
    half = w.shape[-1] // 2
    return jnp.concatenate([-w[..., half:], w[..., :half]], axis=-1)


def _fold_rot(g):
    rot = g[..., DN + DR:]
    half = DR // 2
    return jnp.concatenate([g[..., :DN], g[..., DN:DN + DR] + jnp.concatenate([rot[..., half:], -rot[..., :half]], -1)], -1)


def _with_rot(w):
    return jnp.concatenate([w, _rot_cols(w[..., DN:])], axis=-1)


def _rope_table(s):
    pos = jnp.arange(s, dtype=F32)
    inv_freq = ROPE_THETA ** (-jnp.arange(0, DR, 2, dtype=F32) / DR)
    ang = pos[:, None] * inv_freq[None, :]
    cos, sin = jnp.cos(ang), jnp.sin(ang)
    return jnp.concatenate([cos, cos, sin, sin], axis=-1)


def _row_stats(x, tq):
    nh, s, _ = x.shape
    return jnp.broadcast_to(x[:, :, 0].reshape(nh, s // tq, 1, tq), (nh, s // tq, 8, tq))


REPL_NAMES = ["ffn_pre_norm", "mix_norm", "ffn_post_norm", "kv_in_norm", "final_norm", "ckv_norm",
              "q_lora_norm", "pool_scale_full", "w_uk", "w_uv"]
PACK_COLS = 1024


def _pack_rows(arrs):
    rows, counts = [], []
    for a in arrs:
        flat = a.reshape(-1).astype(F32)
        n = -(-flat.shape[0] // PACK_COLS)
        flat = jnp.pad(flat, (0, n * PACK_COLS - flat.shape[0]))
        rows.append(flat.reshape(n, PACK_COLS))
        counts.append(n)
    total = sum(counts)
    pad = -total % 8
    if pad:
        rows.append(jnp.zeros((pad, PACK_COLS), F32))
    return jnp.concatenate(rows, axis=0), counts


def _unpack_rows(buf, counts, shapes):
    out, r = [], 0
    for n, shp in zip(counts, shapes):
        size = 1
        for dd in shp:
            size *= dd
        out.append(buf[r:r + n].reshape(-1)[:size].reshape(shp))
        r += n
    return out


def kernel(x, ffn_pre_norm, ffn_pre_wg, ffn_pre_wu, ffn_pre_wd, mix_norm, ffn_post_norm, ffn_post_wg, ffn_post_wu, ffn_post_wd, pool_w, pool_scale, kv_in_norm, w_dkv, ckv_norm, w_uk, w_uv, q_lora_norm, w_dq, w_uq, w_o, final_norm, loss_target, m_ffn_pre_norm, m_ffn_pre_wg, m_ffn_pre_wu, m_ffn_pre_wd, m_mix_norm, m_ffn_post_norm, m_ffn_post_wg, m_ffn_post_wu, m_ffn_post_wd, m_pool_w, m_pool_scale, m_kv_in_norm, m_w_dkv, m_ckv_norm, m_w_uk, m_w_uv, m_q_lora_norm, m_w_dq, m_w_uq, m_w_o, m_final_norm, v_ffn_pre_norm, v_ffn_pre_wg, v_ffn_pre_wu, v_ffn_pre_wd, v_mix_norm, v_ffn_post_norm, v_ffn_post_wg, v_ffn_post_wu, v_ffn_post_wd, v_pool_w, v_pool_scale, v_kv_in_norm, v_w_dkv, v_ckv_norm, v_w_uk, v_w_uv, v_q_lora_norm, v_w_dq, v_w_uq, v_w_o, v_final_norm):
    weights = dict(ffn_pre_norm=ffn_pre_norm, ffn_pre_wg=ffn_pre_wg, ffn_pre_wu=ffn_pre_wu, ffn_pre_wd=ffn_pre_wd,
                   mix_norm=mix_norm, ffn_post_norm=ffn_post_norm, ffn_post_wg=ffn_post_wg, ffn_post_wu=ffn_post_wu,
                   ffn_post_wd=ffn_post_wd, pool_w=pool_w, pool_scale=pool_scale, kv_in_norm=kv_in_norm, w_dkv=w_dkv,
                   ckv_norm=ckv_norm, w_uk=w_uk, w_uv=w_uv, q_lora_norm=q_lora_norm, w_dq=w_dq, w_uq=w_uq, w_o=w_o,
                   final_norm=final_norm)
    mom1 = dict(ffn_pre_norm=m_ffn_pre_norm, ffn_pre_wg=m_ffn_pre_wg, ffn_pre_wu=m_ffn_pre_wu, ffn_pre_wd=m_ffn_pre_wd,
                mix_norm=m_mix_norm, ffn_post_norm=m_ffn_post_norm, ffn_post_wg=m_ffn_post_wg, ffn_post_wu=m_ffn_post_wu,
                ffn_post_wd=m_ffn_post_wd, pool_w=m_pool_w, pool_scale=m_pool_scale, kv_in_norm=m_kv_in_norm,
                w_dkv=m_w_dkv, ckv_norm=m_ckv_norm, w_uk=m_w_uk, w_uv=m_w_uv, q_lora_norm=m_q_lora_norm, w_dq=m_w_dq,
                w_uq=m_w_uq, w_o=m_w_o, final_norm=m_final_norm)
    mom2 = dict(ffn_pre_norm=v_ffn_pre_norm, ffn_pre_wg=v_ffn_pre_wg, ffn_pre_wu=v_ffn_pre_wu, ffn_pre_wd=v_ffn_pre_wd,
                mix_norm=v_mix_norm, ffn_post_norm=v_ffn_post_norm, ffn_post_wg=v_ffn_post_wg, ffn_post_wu=v_ffn_post_wu,
                ffn_post_wd=v_ffn_post_wd, pool_w=v_pool_w, pool_scale=v_pool_scale, kv_in_norm=v_kv_in_norm,
                w_dkv=v_w_dkv, ckv_norm=v_ckv_norm, w_uk=v_w_uk, w_uv=v_w_uv, q_lora_norm=v_q_lora_norm, w_dq=v_w_dq,
                w_uq=v_w_uq, w_o=v_w_o, final_norm=v_final_norm)
    names = list(weights)

    h0 = x[0]
    target = loss_target[0]
    s, d = h0.shape
    nh = N_HEADS
    nb = DEPTH - N_A
    f_loc = ffn_pre_wd.shape[1]
    ffn_dim = N_CHIPS * f_loc
    my_chip = 2 * lax.axis_index("x") + lax.axis_index("y")

    ffn_in = {"pre": (ffn_pre_norm, ffn_pre_wg, ffn_pre_wu, ffn_pre_wd),
              "post": (ffn_post_norm, ffn_post_wg, ffn_post_wu, ffn_post_wd)}
    uq_loc = jnp.transpose(w_uq, (0, 2, 1, 3))
    shard = {
        "pool_w": pool_w.reshape(N_A * 4, PG // N_CHIPS, PG).astype(BF16),
        "pool_scale": pool_scale.reshape(N_A, 1, PG),
        "wdkv": _with_rot(w_dkv)[None].astype(BF16),
        "wdq": w_dq.astype(BF16),
        "wq": _with_rot(uq_loc).reshape(nb * nh, uq_loc.shape[2], HD).astype(BF16),
        "wo": w_o.astype(BF16),
    }
    for part, (_, wg_, wu_, wd_) in ffn_in.items():
        for l in range(DEPTH):
            shard[f"{part}_wg{l}"] = jnp.transpose(wg_[l]).astype(BF16)[None]
            shard[f"{part}_wu{l}"] = jnp.transpose(wu_[l]).astype(BF16)[None]
            shard[f"{part}_wd{l}"] = wd_[l].astype(BF16)[None]
    ffn_names = lambda part, l: [f"{part}_wg{l}", f"{part}_wu{l}", f"{part}_wd{l}"]
    gather_plan = {
        None: ffn_names("pre", 0),
        ("pre", 0): ["pool_w", "pool_scale"] + ffn_names("post", 0) + ffn_names("pre", 1),
        ("post", 0): ffn_names("post", 1) + ["wdkv"] + ffn_names("pre", 2),
        ("pre", 1): ffn_names("post", 2) + ["wdq", "wq", "wo"],
        ("post", 1): ffn_names("pre", 3) + ffn_names("post", 3),
    }
    full = {}

    def absorb(plan_names, outs):
        for n, g in zip(plan_names, outs):
            full[n] = g.reshape(g.shape[0], N_CHIPS * g.shape[2], g.shape[3])

    absorb(gather_plan[None], chip_copies("gather_first", ChipCopies("gather", [shard[n] for n in gather_plan[None]])))
    wuk_h = jnp.transpose(w_uk, (1, 0, 2)).astype(BF16)
    wuv_h = jnp.transpose(w_uv, (1, 0, 2)).astype(BF16)
    cs = _rope_table(s)

    def ffn_forward(part, l, hin):
        plan_names = gather_plan.get((part, l))
        carry = ChipCopies("gather", [shard[n] for n in plan_names]) if plan_names else None
        wg_, wu_, wd_ = [full[n][0] for n in ffn_names(part, l)]
        hout, carried = ffn_fwd(f"ffn_{part}_fwd{l}", hin, ffn_in[part][0][l:l + 1], wg_, wu_, wd_, carry)
        if plan_names:
            absorb(plan_names, carried)
        return hout

    saved = {}
    h = h0
    kv = None
    for l in range(DEPTH):
        saved["a", l] = h
        h = ffn_forward("pre", l, h)
        saved["b", l] = h
        if l == 0:
            pool_w_full = full["pool_w"].reshape(N_A, 4, PG, PG)
            pool_scale_full = full["pool_scale"].reshape(N_A, d)
        if l < N_A:
            h, y = pool_fwd(f"pool_fwd{l}", h, mix_norm[l:l + 1], pool_w_full, pool_scale_full[l:l + 1], l)
            saved["y", l] = y
        else:
            j = l - N_A
            u = rms_fwd(f"mix_norm_fwd{l}", h, mix_norm[l:l + 1])
            cq_raw = mm_nn(f"q_down{l}", u, wdq_full[j], F32, 2048, 256, 1024)
            cq = rms_fwd(f"q_norm_fwd{l}", cq_raw, q_lora_norm[j:j + 1])
            q = q_heads(f"q_heads{l}", cq, wq256[j], cs)
            o, lse = flash_fwd(f"flash_fwd{l}", q, kv[0], kv[1])
            saved["att", l] = (u, cq_raw, cq, q, o, lse)
            h = mm_nn(f"attn_out{l}", o, wo_full[j], F32, 1024, 1024, 1024, res=h)
        saved["c", l] = h
        h = ffn_forward("post", l, h)
        if l == N_A - 1:
            wdkv256 = full["wdkv"][0]
            wdq_full = full["wdq"]
            wq256 = full["wq"].reshape(nb, nh, HD, HD)
            wo_full = full["wo"]
            u_kv = rms_fwd("kv_in_norm_fwd", h, kv_in_norm[None])
            kv_raw = mm_nn("kv_down", u_kv, wdkv256, F32, 2048, 256, 1024)
            c_kv, kr = kv_post("kv_post", kv_raw, ckv_norm[None], cs)
            kv = kv_heads("kv_heads", c_kv, kr, wuk_h, wuv_h)
            saved["kv"] = (u_kv, kv_raw, c_kv)

    sq, dh, g_final = loss_head("loss_head", h, final_norm[None], target)
    loss = lax.psum(0.5 * jnp.sum(sq) / d, ("x", "y", "c"))

    grads = {}
    gvec = {n: [None] * DEPTH for n in ("ffn_pre_norm", "mix_norm", "ffn_post_norm")}
    g_pool_scale, g_qnorm = [None] * N_A, [None] * nb
    dkv = None
    gfull, partial = {}, {}
    split = lambda g: g.reshape(g.shape[0], N_CHIPS, g.shape[1] // N_CHIPS, g.shape[2])

    def layer_grad_names(l):
        extra = [f"pool_w{l}"] if l < N_A else [f"wdq{l - N_A}", f"wq{l - N_A}", f"wo{l - N_A}"]
        return ffn_names("pre", l) + ffn_names("post", l) + extra + (["wdkv"] if l == N_A - 1 else [])

    def chip_sums(plan_names, received):
        for n, r in zip(plan_names, received):
            partial[n] = sum4(f"chip_sum_{n}", r)

    def ffn_backward(part, l, hin, dh):
        plan_names = layer_grad_names(l + 1)[(0 if part == "post" else 1)::2] if l + 1 < DEPTH else []
        if part == "pre" and l == 0:
            plan_names = plan_names + layer_grad_names(0)[3:]
        carry = ChipCopies("exchange", [gfull[n] for n in plan_names]) if plan_names else None
        wg_, wu_, wd_ = [full[n][0] for n in ffn_names(part, l)]
        (dh, u, dob, act, da, db, dgain), received = ffn_bwd(
            f"ffn_{part}_bwd{l}", hin, ffn_in[part][0][l:l + 1], wg_, wu_, wd_, dh, carry)
        if plan_names:
            chip_sums(plan_names, received)
        gvec[f"ffn_{part}_norm"][l] = dgain[0]
        for n, lhs, rhs in zip(ffn_names(part, l), (da, db, act), (u, u, dob)):
            gfull[n] = split(mm_tn(f"d_{n}", lhs, rhs, BF16, ffn_dim // 2, d, 1024)[None])
        return dh

    for l in reversed(range(DEPTH)):
        if l == N_A - 1:
            u_kv, kv_raw, c_kv = saved["kv"]
            dk, dv = dkv
            dc, dkr, dwuk, dwuv = kv_heads_bwd("kv_heads_bwd", dk, dv, c_kv, wuk_h, wuv_h)
            grads["w_uk"] = jnp.transpose(dwuk, (1, 0, 2))
            grads["w_uv"] = jnp.transpose(dwuv, (1, 0, 2))
            dkv_raw, dg_ckv = kv_post_bwd("kv_post_bwd", kv_raw, ckv_norm[None], dc, dkr, cs)
            grads["ckv_norm"] = dg_ckv[0]
            du_kv = mm_nt("kv_du", dkv_raw, wdkv256, F32, 2048, 1024, 256)
            gfull["wdkv"] = split(mm_tn("kv_dwdkv", u_kv, dkv_raw, BF16, 1024, 256, 2048)[None])
            dh, dg_kvin = rms_bwd("kv_in_norm_bwd", saved["a", l + 1], kv_in_norm[None], du_kv, dres=dh)
            grads["kv_in_norm"] = dg_kvin[0]

        dh = ffn_backward("post", l, saved["c", l], dh)

        if l < N_A:
            dh, dz, dgain, dscale = pool_bwd(f"pool_bwd{l}", saved["b", l], mix_norm[l:l + 1], saved["y", l],
                                             pool_w_full, pool_scale_full[l:l + 1], l, dh)
            gvec["mix_norm"][l] = dgain[0]
            g_pool_scale[l] = dscale[0]
            tk = _tile(s, 2048)
            gfull[f"pool_w{l}"] = split(_mm(
                f"pool_dw{l}", saved["y", l], dz, grid=(4, 1, s // tk),
                a_block=(tk, PG), a_map=lambda g, jj, kk: (kk, g), b_block=(tk, PG), b_map=lambda g, jj, kk: (kk, g),
                o_block=(None, PG, PG), o_map=lambda g, jj, kk: (g, 0, 0),
                out_shape=(4, PG, PG), dims=TN, out_dtype=BF16))
        else:
            j = l - N_A
            u, cq_raw, cq, q, o, lse = saved["att", l]
            k_, v_ = kv
            do = mm_nt(f"attn_do{l}", dh, wo_full[j], BF16, 1024, 1024, 1024)
            gfull[f"wo{j}"] = split(mm_tn(f"attn_dwo{l}", o, dh, BF16, 1024, 1024, 1024)[None])
            do2 = attn_do2(f"attn_do2_{l}", o, do, nh)
            dk_, dv_, dq_acc = flash_bwd(f"flash_bwd{l}", q, k_, v_, do2, _row_stats(lse, _tile(s, TC)), prev=dkv)
            dkv = (dk_, dv_)
            dcq, dwq = q_heads_bwd(f"q_heads_bwd{l}", dq_acc, cs, wq256[j], cq)
            gfull[f"wq{j}"] = split(dwq)
            dcq_raw, dg_q = rms_bwd(f"q_norm_bwd{l}", cq_raw, q_lora_norm[j:j + 1], dcq)
            g_qnorm[j] = dg_q[0]
            du = mm_nt(f"q_du{l}", dcq_raw, wdq_full[j], F32, 2048, 1024, 256)
            gfull[f"wdq{j}"] = split(mm_tn(f"q_dwdq{l}", u, dcq_raw, BF16, 1024, 256, 2048)[None])
            dh, dgain = rms_bwd(f"mix_norm_bwd{l}", saved["b", l], mix_norm[l:l + 1], du, dres=dh)
            gvec["mix_norm"][l] = dgain[0]

        dh = ffn_backward("pre", l, saved["a", l], dh)

    grad_x = dh[None]

    last_names = layer_grad_names(0)[:3]
    chip_sums(last_names, chip_copies("exchange_last", ChipCopies("exchange", [gfull[n] for n in last_names])))
    part_names = sorted(partial)
    sibling = dict(zip(part_names, sibling_exchange("exchange_sibling", [partial[n] for n in part_names])))
    pieces = {
        "pool_w": [f"pool_w{l}" for l in range(N_A)], "w_dkv": ["wdkv"], "w_dq": [f"wdq{j}" for j in range(nb)],
        "w_uq": [f"wq{j}" for j in range(nb)], "w_o": [f"wo{j}" for j in range(nb)],
    }
    for part in ("pre", "post"):
        for m in ("wg", "wu", "wd"):
            pieces[f"ffn_{part}_{m}"] = [f"{part}_{m}{l}" for l in range(DEPTH)]

    def natural(name, src):
        p = jnp.concatenate([src[n] for n in pieces[name]], axis=0)
        if name in ("ffn_pre_wg", "ffn_pre_wu", "ffn_post_wg", "ffn_post_wu"):
            return jnp.transpose(p, (0, 2, 1))
        if name == "w_dkv":
            return _fold_rot(p[0])
        if name == "w_uq":
            return jnp.transpose(_fold_rot(p).reshape(nb, nh, p.shape[1], QK_DIM), (0, 2, 1, 3))
        return p.reshape(weights[name].shape)

    results = {}
    for n in pieces:
        shp = weights[n].shape
        as2d = lambda a: a.reshape(-1, shp[-1])
        outs = adamw(f"adamw_{n}", [as2d(natural(n, partial)), as2d(natural(n, sibling))],
                     as2d(weights[n]), as2d(mom1[n]), as2d(mom2[n]))
        results[n] = [o_.reshape(shp) for o_ in outs]

    grads.update(ffn_pre_norm=jnp.stack(gvec["ffn_pre_norm"]), mix_norm=jnp.stack(gvec["mix_norm"]),
                 ffn_post_norm=jnp.stack(gvec["ffn_post_norm"]), final_norm=g_final[0],
                 q_lora_norm=jnp.stack(g_qnorm), pool_scale_full=jnp.stack(g_pool_scale))
    repl_shapes = [grads[n].shape for n in REPL_NAMES]
    packed_g, counts = _pack_rows([grads[n] for n in REPL_NAMES])
    all_g = all_gather8("gather_small_grads", packed_g)
    zeros_ps = jnp.zeros((N_A, d), F32)
    pick = lambda src: [zeros_ps if n == "pool_scale_full" else src[n] for n in REPL_NAMES]
    packed = [_pack_rows(pick(src))[0] for src in (weights, mom1, mom2)]
    outs = adamw("adamw_replicated", [all_g], *packed)
    unpacked = [_unpack_rows(o_, counts, repl_shapes) for o_ in outs]
    for idx, n in enumerate(REPL_NAMES):
        results[n] = [u_[idx] for u_ in unpacked]
    g_ps = lax.dynamic_index_in_dim(results["pool_scale_full"][0].reshape(N_A, N_CHIPS, PG), my_chip, axis=1, keepdims=False)
    results["pool_scale"] = adamw("adamw_pool_scale", [g_ps], pool_scale, m_pool_scale, v_pool_scale)

    out = [loss, grad_x]
    for part in range(4):
        out += [results[n][part] for n in names]
    return tuple(out)
```
